```python
import math
import jax
import jax.numpy as jnp
from jax import lax
import numpy as np

D_MODEL = 1024
BATCH = 4
SEQ = 4096
DEPTH = 2

N_EVEN = (DEPTH + 1) // 2
N_ODD = DEPTH // 2
EPS = 1e-6
ROPE_THETA = 10000.0
Q_BLOCK = 128

MLA_HEADS = 8
MLA_Q_LORA = 384
MLA_KV_LORA = 256
MLA_NOPE = 64
MLA_ROPE = 32
MLA_V = 64
DIFF_HEADS = 4
DIFF_HD = 64
DIFF_V = 2 * DIFF_HD
A_IN = MLA_Q_LORA + MLA_KV_LORA + MLA_ROPE + 2 * DIFF_HEADS * 2 * DIFF_HD + DIFF_HEADS * DIFF_V
A_MIX = MLA_HEADS * MLA_V + DIFF_HEADS * DIFF_V

SSM_HEADS = 8
SSM_HEADDIM = 64
SSM_INNER = SSM_HEADS * SSM_HEADDIM
SSM_GROUPS = 2
SSM_STATE = 128
SSM_CONV = 4
SSM_CHUNK = 128
SSM_CONV_DIM = SSM_INNER + 2 * SSM_GROUPS * SSM_STATE
HG_HEADS = 4
HG_EXPAND = 128
HG_VDIM = 128
HG_KDIM_TOTAL = HG_HEADS * HG_EXPAND
HG_WIDTH = HG_HEADS * HG_VDIM
HG_CHUNK = 64
S_IN = SSM_INNER + SSM_CONV_DIM + SSM_HEADS + 2 * HG_KDIM_TOTAL + 2 * HG_WIDTH
S_MIX = SSM_INNER + HG_WIDTH

D_FF = -(-8 * D_MODEL // (3 * 256)) * 256

kernel_name = 'hybrid_mla_diffattn_ssd_hgrn2_block'

F32 = jnp.float32


def rms_norm(x, w):
    xf = x.astype(F32)
    y = xf * lax.rsqrt(jnp.mean(xf * xf, axis=-1, keepdims=True) + EPS)
    return (y * w.astype(F32)).astype(x.dtype)


def split_cols(y, sizes):
    offsets = [int(v) for v in np.cumsum(sizes)[:-1]]
    return jnp.split(y, offsets, axis=-1)


def rope_tables(seq_len, dim):
    inv_freq = 1.0 / (ROPE_THETA ** (jnp.arange(0, dim, 2, dtype=F32) / dim))
    ang = jnp.arange(seq_len, dtype=F32)[:, None] * inv_freq[None, :]
    ang = jnp.concatenate([ang, ang], axis=-1)
    return jnp.cos(ang), jnp.sin(ang)


def apply_rope(x, cos, sin):
    half = x.shape[-1] // 2
    x1, x2 = x[..., :half], x[..., half:]
    rot = jnp.concatenate([-x2, x1], axis=-1)
    return (x.astype(F32) * cos + rot.astype(F32) * sin).astype(x.dtype)


def causal_attention(q, k, v, scale):
    bsz, nh, s_len, dk = q.shape
    dv = v.shape[-1]
    nb = s_len // Q_BLOCK
    qb = q.reshape(bsz, nh, nb, Q_BLOCK, dk).transpose(2, 0, 1, 3, 4)
    kpos = jnp.arange(s_len)

    def one_block(args):
        qi, bi = args
        s = jnp.einsum('bhqd,bhkd->bhqk', qi, k, preferred_element_type=F32) * scale
        qpos = bi * Q_BLOCK + jnp.arange(Q_BLOCK)
        s = jnp.where(kpos[None, :] <= qpos[:, None], s, -jnp.inf)
        p = jax.nn.softmax(s, axis=-1)
        return jnp.einsum('bhqk,bhkd->bhqd', p.astype(v.dtype), v)

    out = lax.map(one_block, (qb, jnp.arange(nb)))
    return out.transpose(1, 2, 0, 3, 4).reshape(bsz, nh, s_len, dv)


def attention_mixer(h, w_in, q_norm, w_uq, kv_norm, w_ukv, lq1, lk1, lq2, lk2, subln, w_out, lambda_init):
    bsz, s_len, _ = h.shape
    proj = h @ w_in
    c_q, c_kv, k_rope, dq, dk, dv = split_cols(
        proj, [MLA_Q_LORA, MLA_KV_LORA, MLA_ROPE, DIFF_HEADS * 2 * DIFF_HD, DIFF_HEADS * 2 * DIFF_HD, DIFF_HEADS * DIFF_V])
    q = (rms_norm(c_q, q_norm) @ w_uq).reshape(bsz, s_len, MLA_HEADS, MLA_NOPE + MLA_ROPE).transpose(0, 2, 1, 3)
    q_nope, q_pe = q[..., :MLA_NOPE], q[..., MLA_NOPE:]
    kv = (rms_norm(c_kv, kv_norm) @ w_ukv).reshape(bsz, s_len, MLA_HEADS, MLA_NOPE + MLA_V).transpose(0, 2, 1, 3)
    k_nope, v_mla = kv[..., :MLA_NOPE], kv[..., MLA_NOPE:]
    cos_r, sin_r = rope_tables(s_len, MLA_ROPE)
    q_pe = apply_rope(q_pe, cos_r, sin_r)
    k_pe = apply_rope(k_rope[:, None], cos_r, sin_r)
    q_mla = jnp.concatenate([q_nope, q_pe], axis=-1)
    k_mla = jnp.concatenate([k_nope, jnp.broadcast_to(k_pe, (bsz, MLA_HEADS, s_len, MLA_ROPE))], axis=-1)
    o_mla = causal_attention(q_mla, k_mla, v_mla, (MLA_NOPE + MLA_ROPE) ** -0.5)
    o_mla = o_mla.transpose(0, 2, 1, 3).reshape(bsz, s_len, MLA_HEADS * MLA_V)
    cos_d, sin_d = rope_tables(s_len, DIFF_HD)
    dq = apply_rope(dq.reshape(bsz, s_len, DIFF_HEADS, 2, DIFF_HD).transpose(0, 3, 2, 1, 4), cos_d, sin_d)
    dk = apply_rope(dk.reshape(bsz, s_len, DIFF_HEADS, 2, DIFF_HD).transpose(0, 3, 2, 1, 4), cos_d, sin_d)
    dv = dv.reshape(bsz, s_len, DIFF_HEADS, DIFF_V).transpose(0, 2, 1, 3)
    v2 = jnp.broadcast_to(dv[:, None], (bsz, 2, DIFF_HEADS, s_len, DIFF_V)).reshape(bsz, 2 * DIFF_HEADS, s_len, DIFF_V)
    o2 = causal_attention(dq.reshape(bsz, 2 * DIFF_HEADS, s_len, DIFF_HD),
                          dk.reshape(bsz, 2 * DIFF_HEADS, s_len, DIFF_HD), v2, DIFF_HD ** -0.5)
    o2 = o2.reshape(bsz, 2, DIFF_HEADS, s_len, DIFF_V)
    lam = (jnp.exp(jnp.sum(lq1.astype(F32) * lk1.astype(F32))) - jnp.exp(jnp.sum(lq2.astype(F32) * lk2.astype(F32)))
           + lambda_init).astype(o2.dtype)
    o_diff = o2[:, 0] - lam * o2[:, 1]
    o_diff = rms_norm(o_diff, subln) * (1.0 - lambda_init)
    o_diff = o_diff.transpose(0, 2, 1, 3).reshape(bsz, s_len, DIFF_HEADS * DIFF_V)
    return jnp.concatenate([o_mla, o_diff], axis=-1) @ w_out


def causal_depthwise_conv(x, w, b):
    ch = x.shape[-1]
    y = lax.conv_general_dilated(x, w[:, None, :].astype(x.dtype), window_strides=(1,),
                                 padding=[(SSM_CONV - 1, 0)], dimension_numbers=('NWC', 'WIO', 'NWC'),
                                 feature_group_count=ch)
    return y + b


def ssd_chunked(xs, dt, a_head, b_in, c_in):
    bsz, s_len, _ = xs.shape
    nc, L = s_len // SSM_CHUNK, SSM_CHUNK
    G, R, P, N = SSM_GROUPS, SSM_HEADS // SSM_GROUPS, SSM_HEADDIM, SSM_STATE
    dtc = dt.reshape(bsz, nc, L, G, R)
    xh = xs.astype(F32).reshape(bsz, nc, L, G, R, P) * dtc[..., None]
    a = (dtc * a_head.reshape(G, R)).transpose(0, 1, 3, 4, 2)
    a_cs = jnp.cumsum(a, axis=-1)
    causal = jnp.tril(jnp.ones((L, L), dtype=bool))
    seg = jnp.exp(jnp.where(causal, a_cs[..., :, None] - a_cs[..., None, :], -jnp.inf))
    bc = b_in.astype(F32).reshape(bsz, nc, L, G, N)
    cc = c_in.astype(F32).reshape(bsz, nc, L, G, N)
    cb = jnp.einsum('bclgn,bcsgn->bcgls', cc, bc)
    y_diag = jnp.einsum('bcgls,bcgrls,bcsgrp->bclgrp', cb, seg, xh)
    decay_states = jnp.exp(a_cs[..., -1:] - a_cs)
    states = jnp.einsum('bclgn,bcgrl,bclgrp->bcgrpn', bc, decay_states, xh)
    chunk_decay = jnp.exp(a_cs[..., -1])

    def step(h_prev, inp):
        st, dec = inp
        return dec[..., None, None] * h_prev + st, h_prev

    h0 = jnp.zeros((bsz, G, R, P, N), F32)
    _, prev = lax.scan(step, h0, (states.transpose(1, 0, 2, 3, 4, 5), chunk_decay.transpose(1, 0, 2, 3)))
    prev = prev.transpose(1, 0, 2, 3, 4, 5)
    y_off = jnp.einsum('bclgn,bcgrpn,bcgrl->bclgrp', cc, prev, jnp.exp(a_cs))
    return (y_diag + y_off).reshape(bsz, s_len, SSM_HEADS, P)


def hgrn2_chunked(q, k, v, log_f):
    bsz, s_len, nh, kd = q.shape
    vd = v.shape[-1]
    nc, L = s_len // HG_CHUNK, HG_CHUNK

    def chunks(t):
        return t.astype(F32).reshape(bsz, nc, L, nh, t.shape[-1]).transpose(1, 0, 3, 2, 4)

    causal = jnp.tril(jnp.ones((L, L), dtype=bool))[None, None, :, :, None]

    def step(state, inp):
        qc, kc, vc, gc = inp
        g_cum = jnp.cumsum(gc, axis=2)
        g_last = g_cum[:, :, -1:, :]
        o_inter = jnp.einsum('bhlk,bhkv->bhlv', qc * jnp.exp(g_cum), state)
        decay = jnp.exp(jnp.where(causal, g_cum[:, :, :, None, :] - g_cum[:, :, None, :, :], -jnp.inf))
        scores = jnp.einsum('bhlk,bhlsk,bhsk->bhls', qc, decay, kc)
        o = o_inter + jnp.einsum('bhls,bhsv->bhlv', scores, vc)
        state = (jnp.exp(g_last[:, :, 0, :])[..., None] * state
                 + jnp.einsum('bhsk,bhsv->bhkv', kc * jnp.exp(g_last - g_cum), vc))
        return state, o

    s0 = jnp.zeros((bsz, nh, kd, vd), F32)
    _, o = lax.scan(step, s0, (chunks(q), chunks(k), chunks(v), chunks(log_f)))
    return o.transpose(1, 0, 3, 2, 4).reshape(bsz, s_len, nh, vd)


def recurrent_mixer(h, w_in, conv_w, conv_b, dt_bias, a_log, d_skip, ssm_norm, g_norm, lb, w_out):
    bsz, s_len, _ = h.shape
    proj = h @ w_in
    z, xbc, dt, hq, hf, hi, hg = split_cols(
        proj, [SSM_INNER, SSM_CONV_DIM, SSM_HEADS, HG_KDIM_TOTAL, HG_KDIM_TOTAL, HG_WIDTH, HG_WIDTH])
    xbc = jax.nn.silu(causal_depthwise_conv(xbc, conv_w, conv_b))
    xs, b_in, c_in = split_cols(xbc, [SSM_INNER, SSM_GROUPS * SSM_STATE, SSM_GROUPS * SSM_STATE])
    dt = jax.nn.softplus(dt.astype(F32) + dt_bias.astype(F32))
    a_head = -jnp.exp(a_log.astype(F32))
    y = ssd_chunked(xs, dt, a_head, b_in, c_in)
    y = y + d_skip.astype(F32)[:, None] * xs.astype(F32).reshape(bsz, s_len, SSM_HEADS, SSM_HEADDIM)
    y = y.astype(h.dtype).reshape(bsz, s_len, SSM_INNER) * jax.nn.silu(z)
    gsz = SSM_INNER // SSM_GROUPS
    y = rms_norm(y.reshape(bsz, s_len, SSM_GROUPS, gsz), ssm_norm.reshape(SSM_GROUPS, gsz)).reshape(bsz, s_len, SSM_INNER)
    lb = lb.astype(F32)
    xf = hf.astype(F32)
    log_f = jnp.logaddexp(jnp.log(lb), jnp.log1p(-lb) + jax.nn.log_sigmoid(xf))
    k_in = (1.0 - lb) * jax.nn.sigmoid(-xf)
    q = jax.nn.silu(hq).reshape(bsz, s_len, HG_HEADS, HG_EXPAND)
    o = hgrn2_chunked(q, k_in.reshape(bsz, s_len, HG_HEADS, HG_EXPAND), hi.reshape(bsz, s_len, HG_HEADS, HG_VDIM),
                      log_f.reshape(bsz, s_len, HG_HEADS, HG_EXPAND)).astype(h.dtype)
    o = rms_norm(o, g_norm) * jax.nn.silu(hg).reshape(bsz, s_len, HG_HEADS, HG_VDIM)
    o = o.reshape(bsz, s_len, HG_WIDTH)
    return jnp.concatenate([y, o], axis=-1) @ w_out


def swiglu(h, w_gate, w_up, w_down):
    return (jax.nn.silu(h @ w_gate) * (h @ w_up)) @ w_down


def setup_inputs(seed: int = 0) -> dict:
    key = jax.random.key(seed)
    ks = jax.random.split(key, 32)

    def nrm(k, shape, fan_in):
        return jax.random.normal(k, shape, F32) * (fan_in ** -0.5)

    def gain(k, shape):
        return 1.0 + 0.05 * jax.random.normal(k, shape, F32)

    dt0 = jnp.exp(jax.random.uniform(ks[20], (N_ODD, SSM_HEADS), F32, math.log(1e-3), math.log(1e-1)))
    return {
        'x': jax.random.normal(ks[0], (BATCH, SEQ, D_MODEL), F32),
        'norm_mix': gain(ks[1], (DEPTH, D_MODEL)),
        'norm_ffn': gain(ks[2], (DEPTH, D_MODEL)),
        'norm_final': gain(ks[3], (D_MODEL,)),
        'a_w_in': nrm(ks[4], (N_EVEN, D_MODEL, A_IN), D_MODEL),
        'a_q_norm': gain(ks[5], (N_EVEN, MLA_Q_LORA)),
        'a_w_uq': nrm(ks[6], (N_EVEN, MLA_Q_LORA, MLA_HEADS * (MLA_NOPE + MLA_ROPE)), MLA_Q_LORA),
        'a_kv_norm': gain(ks[7], (N_EVEN, MLA_KV_LORA)),
        'a_w_ukv': nrm(ks[8], (N_EVEN, MLA_KV_LORA, MLA_HEADS * (MLA_NOPE + MLA_V)), MLA_KV_LORA),
        'a_lq1': 0.1 * jax.random.normal(ks[9], (N_EVEN, DIFF_HD), F32),
        'a_lk1': 0.1 * jax.random.normal(ks[10], (N_EVEN, DIFF_HD), F32),
        'a_lq2': 0.1 * jax.random.normal(ks[11], (N_EVEN, DIFF_HD), F32),
        'a_lk2': 0.1 * jax.random.normal(ks[12], (N_EVEN, DIFF_HD), F32),
        'a_subln': gain(ks[13], (N_EVEN, DIFF_V)),
        'a_w_out': nrm(ks[14], (N_EVEN, A_MIX, D_MODEL), A_MIX),
        's_w_in': nrm(ks[15], (N_ODD, D_MODEL, S_IN), D_MODEL),
        's_conv_w': nrm(ks[16], (N_ODD, SSM_CONV, SSM_CONV_DIM), SSM_CONV),
        's_conv_b': 0.02 * jax.random.normal(ks[17], (N_ODD, SSM_CONV_DIM), F32),
        's_dt_bias': dt0 + jnp.log(-jnp.expm1(-dt0)),
        's_a_log': jnp.log(jax.random.uniform(ks[18], (N_ODD, SSM_HEADS), F32, 1.0, 16.0)),
        's_d': gain(ks[19], (N_ODD, SSM_HEADS)),
        's_norm': gain(ks[21], (N_ODD, SSM_INNER)),
        'h_g_norm': gain(ks[22], (N_ODD, HG_VDIM)),
        'h_lower_bound': 0.1 * jax.random.normal(ks[23], (DEPTH, HG_KDIM_TOTAL), F32),
        's_w_out': nrm(ks[24], (N_ODD, S_MIX, D_MODEL), S_MIX),
        'ffn_gate': nrm(ks[25], (DEPTH, D_MODEL, D_FF), D_MODEL),
        'ffn_up': nrm(ks[26], (DEPTH, D_MODEL, D_FF), D_MODEL),
        'ffn_down': nrm(ks[27], (DEPTH, D_FF, D_MODEL), D_FF),
    }


def reference(x, norm_mix, norm_ffn, norm_final, a_w_in, a_q_norm, a_w_uq, a_kv_norm, a_w_ukv,
              a_lq1, a_lk1, a_lq2, a_lk2, a_subln, a_w_out, s_w_in, s_conv_w, s_conv_b, s_dt_bias,
              s_a_log, s_d, s_norm, h_g_norm, h_lower_bound, s_w_out, ffn_gate, ffn_up, ffn_down):
    p_lb = jax.nn.softmax(h_lower_bound.astype(F32), axis=0)
    lb_all = jnp.cumsum(p_lb, axis=0) - p_lb[0:1]
    for l in range(DEPTH):
        hn = rms_norm(x, norm_mix[l])
        i = l // 2
        if l % 2 == 0:
            lambda_init = 0.8 - 0.6 * math.exp(-0.3 * l)
            m = attention_mixer(hn, a_w_in[i], a_q_norm[i], a_w_uq[i], a_kv_norm[i], a_w_ukv[i],
                                a_lq1[i], a_lk1[i], a_lq2[i], a_lk2[i], a_subln[i], a_w_out[i], lambda_init)
        else:
            m = recurrent_mixer(hn, s_w_in[i], s_conv_w[i], s_conv_b[i], s_dt_bias[i], s_a_log[i], s_d[i],
                                s_norm[i], h_g_norm[i], lb_all[l], s_w_out[i])
        x = x + m
        x = x + swiglu(rms_norm(x, norm_ffn[l]), ffn_gate[l], ffn_up[l], ffn_down[l])
    return rms_norm(x, norm_final)
```

```python
import functools
import math

import numpy as np
import jax
import jax.numpy as jnp
from jax import lax
from jax.experimental import pallas as pl
from jax.experimental.pallas import tpu as pltpu

F32 = jnp.float32
BF16 = jnp.bfloat16

D_MODEL = 1024
EPS = 1e-6
ROPE_THETA = 10000.0

MLA_HEADS = 8
MLA_Q_LORA = 384
MLA_KV_LORA = 256
MLA_NOPE = 64
MLA_ROPE = 32
MLA_V = 64
DIFF_HEADS = 4
DIFF_HD = 64
DIFF_V = 2 * DIFF_HD

SSM_HEADS = 8
SSM_HEADDIM = 64
SSM_INNER = SSM_HEADS * SSM_HEADDIM
SSM_GROUPS = 2
SSM_STATE = 128
SSM_CONV = 4
SSM_CHUNK = 128
SSM_CONV_DIM = SSM_INNER + 2 * SSM_GROUPS * SSM_STATE
HG_HEADS = 4
HG_EXPAND = 128
HG_VDIM = 128
HG_KDIM_TOTAL = HG_HEADS * HG_EXPAND
HG_WIDTH = HG_HEADS * HG_VDIM
HG_CHUNK = 64

LANES = 128
VMEM_LIMIT = 52 * 1024 * 1024

NT_DIMS = (((1,), (1,)), ((), ()))
TN_DIMS = (((0,), (0,)), ((), ()))


def _dot(a, b):
    return jnp.dot(a, b, preferred_element_type=F32)


def _dot_nt(a, b):
    return lax.dot_general(a, b, NT_DIMS, preferred_element_type=F32)


def _dot_tn(a, b):
    return lax.dot_general(a, b, TN_DIMS, preferred_element_type=F32)


def _rms(x, w):
    return x * lax.rsqrt(jnp.mean(x * x, axis=-1, keepdims=True) + EPS) * w


def _silu(x):
    return x * jax.nn.sigmoid(x)


def _split3(a):
    hi = a.astype(BF16)
    r = a - hi.astype(F32)
    mid = r.astype(BF16)
    lo = (r - mid.astype(F32)).astype(BF16)
    return hi, mid, lo


def _dot01_left(m01, a):
    hi, mid, lo = _split3(a)
    return (_dot(m01, hi) + _dot(m01, mid)) + _dot(m01, lo)


def _dot01_right(a, m01):
    hi, mid, lo = _split3(a)
    return (_dot(hi, m01) + _dot(mid, m01)) + _dot(lo, m01)


def _rope(t, cos, sin_a, sin_b, half):
    return (t * cos + pltpu.roll(t, LANES - half, 1) * sin_a + pltpu.roll(t, half, 1) * sin_b)


A_CQ = 0
A_CKV = A_CQ + MLA_Q_LORA
A_DQ = A_CKV + MLA_KV_LORA
A_DK = A_DQ + DIFF_HEADS * 2 * DIFF_HD
A_DV = A_DK + DIFF_HEADS * 2 * DIFF_HD
A_KR = A_DV + DIFF_HEADS * DIFF_V
A_IN_PACKED = A_KR + LANES


def _attn_inproj_kernel(x_ref, nw_ref, win_ref, qn_ref, wuq_ref, kvn_ref, wuk_ref, wuv_ref,
                        cm_ref, sam_ref, sbm_ref, cd_ref, sad_ref, sbd_ref,
                        q_ref, k_ref, v_ref, dq_ref, dk_ref, dv_ref):
    hn = _rms(x_ref[0], nw_ref[...]).astype(BF16)
    proj = _dot(hn, win_ref[...])
    cq = _rms(proj[:, A_CQ:A_CKV], qn_ref[...]).astype(BF16)
    ckv = _rms(proj[:, A_CKV:A_DQ], kvn_ref[...]).astype(BF16)
    q = _dot(cq, wuq_ref[...])
    kn = _dot(ckv, wuk_ref[...])
    v_ref[0] = _dot(ckv, wuv_ref[...]).astype(BF16)
    cm, sam, sbm = cm_ref[...], sam_ref[...], sbm_ref[...]
    kpe = _rope(proj[:, A_KR:A_KR + LANES], cm, sam, sbm, MLA_ROPE // 2)
    for h in range(MLA_HEADS):
        sl = slice(h * LANES, (h + 1) * LANES)
        q_ref[0, h] = _rope(q[:, sl], cm, sam, sbm, MLA_ROPE // 2).astype(BF16)
        k_ref[0, h] = (kn[:, sl] + kpe).astype(BF16)
    cd, sad, sbd = cd_ref[...], sad_ref[...], sbd_ref[...]
    for g in range(DIFF_HEADS):
        sl = slice(g * LANES, (g + 1) * LANES)
        dq = proj[:, A_DQ + g * LANES:A_DQ + (g + 1) * LANES]
        dk = proj[:, A_DK + g * LANES:A_DK + (g + 1) * LANES]
        dq_ref[0, :, sl] = (_rope(dq, cd, sad, sbd, DIFF_HD // 2) * (DIFF_HD ** -0.5)).astype(BF16)
        dk_ref[0, :, sl] = _rope(dk, cd, sad, sbd, DIFF_HD // 2).astype(BF16)
    dv_ref[0] = proj[:, A_DV:A_KR].astype(BF16)


def _rope_tables(seq_len, dim, lane_offsets):
    half = dim // 2
    inv_freq = 1.0 / (ROPE_THETA ** (jnp.arange(0, dim, 2, dtype=F32) / dim))
    ang = jnp.arange(seq_len, dtype=F32)[:, None] * inv_freq[None, :]
    cos, sin = jnp.cos(ang), jnp.sin(ang)
    zero = jnp.zeros_like(sin)
    c = jnp.ones((seq_len, LANES), F32)
    sa = jnp.zeros((seq_len, LANES), F32)
    sb = jnp.zeros((seq_len, LANES), F32)
    for off in lane_offsets:
        c = c.at[:, off:off + dim].set(jnp.concatenate([cos, cos], axis=-1))
        sa = sa.at[:, off:off + dim].set(jnp.concatenate([-sin, zero], axis=-1))
        sb = sb.at[:, off:off + dim].set(jnp.concatenate([zero, sin], axis=-1))
    return c, sa, sb


def _const_spec(shape):
    nd = len(shape)
    return pl.BlockSpec(shape, lambda *_: (0,) * nd, pipeline_mode=pl.Buffered(1))


def _attn_inproj(x, norm_w, w_in, q_norm, w_uq, kv_norm, w_ukv, ts):
    bsz, s_len, _ = x.shape
    cq, ckv, kr, dq, dk, dv = jnp.split(
        w_in, [int(v) for v in np.cumsum([MLA_Q_LORA, MLA_KV_LORA, MLA_ROPE, 512, 512])], axis=-1)
    kr_pad = jnp.pad(kr, ((0, 0), (MLA_NOPE, LANES - MLA_NOPE - MLA_ROPE)))
    win_p = jnp.concatenate([cq, ckv, dq, dk, dv, kr_pad], axis=-1).astype(BF16)
    scale = (MLA_NOPE + MLA_ROPE) ** -0.5
    wuq_p = jnp.pad((w_uq * scale).reshape(MLA_Q_LORA, MLA_HEADS, MLA_NOPE + MLA_ROPE),
                    ((0, 0), (0, 0), (0, LANES - MLA_NOPE - MLA_ROPE)))
    wuq_p = wuq_p.reshape(MLA_Q_LORA, MLA_HEADS * LANES).astype(BF16)
    wkv = w_ukv.reshape(MLA_KV_LORA, MLA_HEADS, MLA_NOPE + MLA_V)
    wuk_p = jnp.pad(wkv[..., :MLA_NOPE], ((0, 0), (0, 0), (0, LANES - MLA_NOPE)))
    wuk_p = wuk_p.reshape(MLA_KV_LORA, MLA_HEADS * LANES).astype(BF16)
    wuv = wkv[..., MLA_NOPE:].reshape(MLA_KV_LORA, MLA_HEADS * MLA_V).astype(BF16)
    tabs_m = _rope_tables(s_len, MLA_ROPE, (MLA_NOPE,))
    tabs_d = _rope_tables(s_len, DIFF_HD, (0, DIFF_HD))

    row = lambda b, i: (b, i, 0)
    tab = pl.BlockSpec((ts, LANES), lambda b, i: (i, 0))
    head_major = pl.BlockSpec((1, MLA_HEADS, ts, LANES), lambda b, i: (b, 0, i, 0))
    wide = pl.BlockSpec((1, ts, 512), row)
    return pl.pallas_call(
        _attn_inproj_kernel,
        grid=(bsz, s_len // ts),
        in_specs=[pl.BlockSpec((1, ts, D_MODEL), row), _const_spec((1, D_MODEL)),
                  _const_spec(win_p.shape), _const_spec((1, MLA_Q_LORA)), _const_spec(wuq_p.shape),
                  _const_spec((1, MLA_KV_LORA)), _const_spec(wuk_p.shape), _const_spec(wuv.shape),
                  tab, tab, tab, tab, tab, tab],
        out_specs=[head_major, head_major, wide, wide, wide, wide],
        out_shape=[jax.ShapeDtypeStruct((bsz, MLA_HEADS, s_len, LANES), BF16),
                   jax.ShapeDtypeStruct((bsz, MLA_HEADS, s_len, LANES), BF16),
                   jax.ShapeDtypeStruct((bsz, s_len, 512), BF16),
                   jax.ShapeDtypeStruct((bsz, s_len, 512), BF16),
                   jax.ShapeDtypeStruct((bsz, s_len, 512), BF16),
                   jax.ShapeDtypeStruct((bsz, s_len, 512), BF16)],
        compiler_params=pltpu.CompilerParams(dimension_semantics=("parallel", "parallel"),
                                             vmem_limit_bytes=VMEM_LIMIT),
        name="attn_inproj",
    )(x, norm_w.reshape(1, -1), win_p, q_norm.reshape(1, -1), wuq_p, kv_norm.reshape(1, -1), wuk_p, wuv,
      *tabs_m, *tabs_d)


def _softmax_block(s, vblk, m_ref, l_ref, acc_ref):
    m_prev = m_ref[...]
    m_new = jnp.maximum(m_prev, jnp.max(s, axis=-1, keepdims=True))
    alpha = jnp.exp(m_prev - m_new)
    p = jnp.exp(s - m_new)
    l_ref[...] = alpha * l_ref[...] + jnp.sum(p, axis=-1, keepdims=True)
    acc_ref[...] = alpha * acc_ref[...] + _dot(p.astype(BF16), vblk)
    m_ref[...] = m_new


def _causal_mask(s, tq):
    rows, cols = s.shape
    r = lax.broadcasted_iota(jnp.int32, (rows, cols), 0)
    c = lax.broadcasted_iota(jnp.int32, (rows, cols), 1)
    if rows != tq:
        r = r % tq
    return jnp.where(c <= r, s, -jnp.inf)


def _mla_attn_kernel(q_ref, k_ref, v_ref, o_ref, m_sc, l_sc, acc_sc, *, tq):
    i = pl.program_id(2)
    m_sc[...] = jnp.full(m_sc.shape, -jnp.inf, F32)
    l_sc[...] = jnp.zeros(l_sc.shape, F32)
    acc_sc[...] = jnp.zeros(acc_sc.shape, F32)

    def block(j, masked):
        start = pl.multiple_of(j * tq, tq)
        vblk = v_ref[0, pl.ds(start, tq), :]
        for h in range(2):
            s = _dot_nt(q_ref[0, h], k_ref[0, h, pl.ds(start, tq), :])
            if masked:
                s = _causal_mask(s, tq)
            _softmax_block(s, vblk, m_sc.at[h], l_sc.at[h], acc_sc.at[h])

    def body(j, carry):
        block(j, False)
        return carry

    lax.fori_loop(0, i, body, 0)
    block(i, True)
    o0 = acc_sc[0] / l_sc[0]
    o1 = acc_sc[1] / l_sc[1]
    lane = lax.broadcasted_iota(jnp.int32, o0.shape, 1)
    o_ref[0] = jnp.where(lane < MLA_V, o0, o1).astype(BF16)


def _mla_attention(q, k, v, tq):
    bsz, nh, s_len, _ = q.shape
    return pl.pallas_call(
        functools.partial(_mla_attn_kernel, tq=tq),
        grid=(bsz, nh // 2, s_len // tq),
        in_specs=[pl.BlockSpec((1, 2, tq, LANES), lambda b, p, i: (b, p, i, 0)),
                  pl.BlockSpec((1, 2, s_len, LANES), lambda b, p, i: (b, p, 0, 0)),
                  pl.BlockSpec((1, s_len, LANES), lambda b, p, i: (b, 0, p))],
        out_specs=pl.BlockSpec((1, tq, LANES), lambda b, p, i: (b, i, p)),
        out_shape=jax.ShapeDtypeStruct((bsz, s_len, nh * MLA_V), BF16),
        scratch_shapes=[pltpu.VMEM((2, tq, 1), F32), pltpu.VMEM((2, tq, 1), F32),
                        pltpu.VMEM((2, tq, LANES), F32)],
        compiler_params=pltpu.CompilerParams(dimension_semantics=("parallel", "parallel", "arbitrary"),
                                             vmem_limit_bytes=VMEM_LIMIT),
        name="mla_attention",
    )(q, k, v)


def _diff_attn_kernel(q_ref, k_ref, v_ref, lq1_ref, lk1_ref, lq2_ref, lk2_ref, sub_ref, o_ref,
                      qs_sc, m_sc, l_sc, acc_sc, *, tq, lambda_init):
    i = pl.program_id(2)
    m_sc[...] = jnp.full(m_sc.shape, -jnp.inf, F32)
    l_sc[...] = jnp.zeros(l_sc.shape, F32)
    acc_sc[...] = jnp.zeros(acc_sc.shape, F32)
    q = q_ref[0]
    lane = lax.broadcasted_iota(jnp.int32, q.shape, 1)
    zero = jnp.zeros_like(q)
    qs_sc[0:tq] = jnp.where(lane < DIFF_HD, q, zero)
    qs_sc[tq:2 * tq] = jnp.where(lane < DIFF_HD, zero, q)

    def block(j, masked):
        start = pl.multiple_of(j * tq, tq)
        s = _dot_nt(qs_sc[...], k_ref[0, pl.ds(start, tq), :])
        if masked:
            s = _causal_mask(s, tq)
        _softmax_block(s, v_ref[0, pl.ds(start, tq), :], m_sc, l_sc, acc_sc)

    def body(j, carry):
        block(j, False)
        return carry

    lax.fori_loop(0, i, body, 0)
    block(i, True)
    o = acc_sc[...] / l_sc[...]
    lam = (jnp.exp(jnp.sum(lq1_ref[...] * lk1_ref[...], axis=-1, keepdims=True))
           - jnp.exp(jnp.sum(lq2_ref[...] * lk2_ref[...], axis=-1, keepdims=True)) + lambda_init)
    od = o[0:tq] - lam * o[tq:2 * tq]
    o_ref[0] = (_rms(od, sub_ref[...]) * (1.0 - lambda_init)).astype(BF16)


def _diff_attention(dq, dk, dv, lq1, lk1, lq2, lk2, subln, lambda_init, tq):
    bsz, s_len, _ = dq.shape
    qspec = pl.BlockSpec((1, tq, LANES), lambda b, h, i: (b, i, h))
    kvspec = pl.BlockSpec((1, s_len, LANES), lambda b, h, i: (b, 0, h))
    vec = lambda n: pl.BlockSpec((1, n), lambda b, h, i: (0, 0))
    return pl.pallas_call(
        functools.partial(_diff_attn_kernel, tq=tq, lambda_init=lambda_init),
        grid=(bsz, DIFF_HEADS, s_len // tq),
        in_specs=[qspec, kvspec, kvspec, vec(DIFF_HD), vec(DIFF_HD), vec(DIFF_HD), vec(DIFF_HD), vec(DIFF_V)],
        out_specs=qspec,
        out_shape=jax.ShapeDtypeStruct((bsz, s_len, DIFF_HEADS * DIFF_V), BF16),
        scratch_shapes=[pltpu.VMEM((2 * tq, LANES), BF16), pltpu.VMEM((2 * tq, 1), F32),
                        pltpu.VMEM((2 * tq, 1), F32), pltpu.VMEM((2 * tq, LANES), F32)],
        compiler_params=pltpu.CompilerParams(dimension_semantics=("parallel", "parallel", "arbitrary"),
                                             vmem_limit_bytes=VMEM_LIMIT),
        name="diff_attention",
    )(dq, dk, dv, lq1.reshape(1, -1), lk1.reshape(1, -1), lq2.reshape(1, -1), lk2.reshape(1, -1),
      subln.reshape(1, -1))


def _outproj_ffn_kernel(x_ref, ma_ref, mb_ref, woa_ref, wob_ref, nf_ref, wg_ref, wu_ref, wd_ref, fin_ref,
                        o_ref, *, final_norm):
    x1 = x_ref[...] + _dot(ma_ref[...], woa_ref[...]) + _dot(mb_ref[...], wob_ref[...])
    h = _rms(x1, nf_ref[...]).astype(BF16)
    g = _dot(h, wg_ref[...])
    u = _dot(h, wu_ref[...])
    a = (_silu(g) * u).astype(BF16)
    x2 = x1 + _dot(a, wd_ref[...])
    if final_norm:
        x2 = _rms(x2, fin_ref[...])
    o_ref[...] = x2


def _outproj_ffn(x2d, mix_a, mix_b, w_out, norm_ffn, w_gate, w_up, w_down, norm_final, final_norm, tm):
    n_tok = x2d.shape[0]
    half = mix_a.shape[1]
    d_ff = w_gate.shape[1]
    woa = w_out[:half].astype(BF16)
    wob = w_out[half:].astype(BF16)
    row = lambda i: (i, 0)
    return pl.pallas_call(
        functools.partial(_outproj_ffn_kernel, final_norm=final_norm),
        grid=(n_tok // tm,),
        in_specs=[pl.BlockSpec((tm, D_MODEL), row), pl.BlockSpec((tm, half), row), pl.BlockSpec((tm, half), row),
                  _const_spec(woa.shape), _const_spec(wob.shape), _const_spec((1, D_MODEL)),
                  _const_spec((D_MODEL, d_ff)), _const_spec((D_MODEL, d_ff)), _const_spec((d_ff, D_MODEL)),
                  _const_spec((1, D_MODEL))],
        out_specs=pl.BlockSpec((tm, D_MODEL), row),
        out_shape=jax.ShapeDtypeStruct((n_tok, D_MODEL), F32),
        compiler_params=pltpu.CompilerParams(dimension_semantics=("parallel",), vmem_limit_bytes=VMEM_LIMIT),
        name="outproj_ffn",
    )(x2d, mix_a, mix_b, woa, wob, norm_ffn.reshape(1, -1), w_gate.astype(BF16), w_up.astype(BF16),
      w_down.astype(BF16), norm_final.reshape(1, -1))


S_Z = 0
S_XBC = S_Z + SSM_INNER
S_HQ = S_XBC + SSM_CONV_DIM
S_HF = S_HQ + HG_KDIM_TOTAL
S_HI = S_HF + HG_KDIM_TOTAL
S_HG = S_HI + HG_WIDTH
S_DT = S_HG + HG_WIDTH
S_IN_PACKED = S_DT + LANES


def _rec_inproj_kernel(x_ref, nw_ref, win_ref, z_ref, xbc_ref, hq_ref, hf_ref, hi_ref, hg_ref, dt_ref):
    hn = _rms(x_ref[...], nw_ref[...]).astype(BF16)
    proj = _dot(hn, win_ref[...])
    z_ref[...] = proj[:, S_Z:S_XBC].astype(BF16)
    xbc_ref[...] = proj[:, S_XBC:S_HQ]
    hq_ref[...] = proj[:, S_HQ:S_HF].astype(BF16)
    hf_ref[...] = proj[:, S_HF:S_HI]
    hi_ref[...] = proj[:, S_HI:S_HG].astype(BF16)
    hg_ref[...] = proj[:, S_HG:S_DT].astype(BF16)
    dt_ref[...] = proj[:, S_DT:S_IN_PACKED]


def _rec_inproj(x2d, norm_w, w_in, tm):
    n_tok = x2d.shape[0]
    z, xbc, dt, hq, hf, hi, hg = jnp.split(
        w_in, [int(v) for v in np.cumsum([SSM_INNER, SSM_CONV_DIM, SSM_HEADS, 512, 512, 512])], axis=-1)
    dt_pad = jnp.pad(dt, ((0, 0), (0, LANES - SSM_HEADS)))
    win_p = jnp.concatenate([z, xbc, hq, hf, hi, hg, dt_pad], axis=-1).astype(BF16)
    row = lambda i: (i, 0)
    widths = [(SSM_INNER, BF16), (SSM_CONV_DIM, F32), (512, BF16), (512, F32), (512, BF16), (512, BF16),
              (LANES, F32)]
    return pl.pallas_call(
        _rec_inproj_kernel,
        grid=(n_tok // tm,),
        in_specs=[pl.BlockSpec((tm, D_MODEL), row), _const_spec((1, D_MODEL)), _const_spec(win_p.shape)],
        out_specs=[pl.BlockSpec((tm, w), row) for w, _ in widths],
        out_shape=[jax.ShapeDtypeStruct((n_tok, w), dt_) for w, dt_ in widths],
        compiler_params=pltpu.CompilerParams(dimension_semantics=("parallel",), vmem_limit_bytes=VMEM_LIMIT),
        name="rec_inproj",
    )(x2d, norm_w.reshape(1, -1), win_p)


CONV_PAD = 8


def _ssd_kernel(xbc_ref, dt_ref, z_ref, cw_ref, cb_ref, dtb_ref, ah_ref, dsk_ref, nw_ref, tri_ref, exp_ref,
                y_ref, xpad_sc, st_sc):
    L = SSM_CHUNK
    heads_per_group = SSM_HEADS // SSM_GROUPS
    gw = heads_per_group * SSM_HEADDIM

    @pl.when(pl.program_id(1) == 0)
    def _():
        xpad_sc[0:CONV_PAD] = jnp.zeros((CONV_PAD, SSM_CONV_DIM), F32)
        st_sc[...] = jnp.zeros(st_sc.shape, F32)

    xt = xbc_ref[0]
    xpad_sc[CONV_PAD:CONV_PAD + L] = xt
    conv = cb_ref[...] + cw_ref[SSM_CONV - 1:SSM_CONV] * xt
    for d in range(1, SSM_CONV):
        conv = conv + cw_ref[SSM_CONV - 1 - d:SSM_CONV - d] * xpad_sc[CONV_PAD - d:CONV_PAD - d + L]
    xpad_sc[0:CONV_PAD] = xt[L - CONV_PAD:L]
    xc = _silu(conv)
    xs = xc[:, 0:SSM_INNER]
    b_in = xc[:, SSM_INNER:SSM_INNER + SSM_GROUPS * SSM_STATE].astype(BF16)
    c_in = xc[:, SSM_INNER + SSM_GROUPS * SSM_STATE:].astype(BF16)

    dt = jax.nn.softplus(dt_ref[0] + dtb_ref[...])
    a = dt * ah_ref[...]
    a_cs = _dot01_left(tri_ref[...], a)
    a_cs_t = a_cs.T
    dt_e = _dot01_right(dt, exp_ref[...])
    acs_e = _dot01_right(a_cs, exp_ref[...])
    alast_e = acs_e[L - 1:L, :]
    xdt = xs * dt_e
    xdec = (xdt * jnp.exp(alast_e - acs_e)).astype(BF16)
    xdt_b = xdt.astype(BF16)
    eacs = jnp.exp(acs_e)
    r = lax.broadcasted_iota(jnp.int32, (L, L), 0)
    c = lax.broadcasted_iota(jnp.int32, (L, L), 1)
    causal = c <= r
    lane = lax.broadcasted_iota(jnp.int32, (L, LANES), 1)

    ys = []
    for g in range(SSM_GROUPS):
        bg = b_in[:, g * SSM_STATE:(g + 1) * SSM_STATE]
        cg = c_in[:, g * SSM_STATE:(g + 1) * SSM_STATE]
        cb = _dot_nt(cg, bg)
        st_prev = st_sc[g]
        y_off = _dot(cg, st_prev.astype(BF16)) * eacs[:, g * gw:(g + 1) * gw]
        st_sc[g] = st_prev * jnp.exp(alast_e[:, g * gw:(g + 1) * gw]) + _dot_tn(bg, xdec[:, g * gw:(g + 1) * gw])
        for pr in range(heads_per_group // 2):
            xpair = xdt_b[:, g * gw + pr * LANES:g * gw + (pr + 1) * LANES]
            res = []
            for hh in range(2):
                h = g * heads_per_group + 2 * pr + hh
                seg = jnp.exp(jnp.minimum(a_cs[:, h:h + 1] - a_cs_t[h:h + 1, :], 0.0))
                m = jnp.where(causal, cb * seg, 0.0).astype(BF16)
                res.append(_dot(m, xpair))
            ys.append(jnp.where(lane < SSM_HEADDIM, res[0], res[1]) + y_off[:, pr * LANES:(pr + 1) * LANES])
    y = jnp.concatenate(ys, axis=1) + dsk_ref[...] * xs
    y = y * _silu(z_ref[0].astype(F32))
    nw = nw_ref[...]
    for g in range(SSM_GROUPS):
        sl = slice(g * gw, (g + 1) * gw)
        y_ref[0, :, sl] = _rms(y[:, sl], nw[:, sl]).astype(BF16)


def _ssd(xbc, dt, z, conv_w, conv_b, dt_bias, a_log, d_skip, ssm_norm):
    bsz, s_len, _ = xbc.shape
    L = SSM_CHUNK
    pad = lambda v: jnp.pad(v.astype(F32), (0, LANES - SSM_HEADS)).reshape(1, LANES)
    a_head = -jnp.exp(a_log.astype(F32))
    tri = jnp.asarray(np.tril(np.ones((L, L), np.float32)), BF16)
    expand = np.zeros((LANES, SSM_INNER), np.float32)
    for h in range(SSM_HEADS):
        expand[h, h * SSM_HEADDIM:(h + 1) * SSM_HEADDIM] = 1.0
    expand = jnp.asarray(expand, BF16)
    dsk = jnp.repeat(d_skip.astype(F32), SSM_HEADDIM).reshape(1, SSM_INNER)
    chunk = lambda w: pl.BlockSpec((1, L, w), lambda b, c: (b, c, 0))
    const = lambda shape: pl.BlockSpec(shape, lambda b, c: (0,) * len(shape))
    return pl.pallas_call(
        _ssd_kernel,
        grid=(bsz, s_len // L),
        in_specs=[chunk(SSM_CONV_DIM), chunk(LANES), chunk(SSM_INNER), const((SSM_CONV, SSM_CONV_DIM)),
                  const((1, SSM_CONV_DIM)), const((1, LANES)), const((1, LANES)), const((1, SSM_INNER)),
                  const((1, SSM_INNER)), const((L, L)), const((LANES, SSM_INNER))],
        out_specs=chunk(SSM_INNER),
        out_shape=jax.ShapeDtypeStruct((bsz, s_len, SSM_INNER), BF16),
        scratch_shapes=[pltpu.VMEM((CONV_PAD + L, SSM_CONV_DIM), F32),
                        pltpu.VMEM((SSM_GROUPS, SSM_STATE, SSM_INNER // SSM_GROUPS), F32)],
        compiler_params=pltpu.CompilerParams(dimension_semantics=("parallel", "arbitrary"),
                                             vmem_limit_bytes=VMEM_LIMIT),
        name="ssd_scan",
    )(xbc, dt, z, conv_w, conv_b.reshape(1, -1), pad(dt_bias), pad(a_head), dsk, ssm_norm.reshape(1, -1),
      tri, expand)


HG_LEVELS = int(math.log2(HG_CHUNK))


def _hgrn_tables():
    C = HG_CHUNK
    idx = np.arange(C)
    mats = [np.tril(np.ones((C, C), np.float32))]
    masks = []
    for lev in range(HG_LEVELS):
        h = C >> (lev + 1)
        mid = (idx // (2 * h)) * (2 * h) + h - 1
        upper = (idx % (2 * h)) >= h
        j = idx[None, :]
        dq = (upper[:, None] & (j > mid[:, None]) & (j <= idx[:, None])).astype(np.float32)
        ek = ((~upper)[:, None] & (j > idx[:, None]) & (j <= mid[:, None])).astype(np.float32)
        mats += [dq, ek]
        same = (idx[:, None] // (2 * h)) == (idx[None, :] // (2 * h))
        masks.append((same & upper[:, None] & (~upper)[None, :]).astype(np.float32))
    masks.append(np.eye(C, dtype=np.float32))
    return np.concatenate(mats, axis=0), np.stack(masks, axis=0)


def _hgrn_kernel(hq_ref, hf_ref, hi_ref, hg_ref, lb_ref, gn_ref, wall_ref, masks_ref, o_ref, st_sc):
    C = HG_CHUNK

    @pl.when(pl.program_id(1) == 0)
    def _():
        st_sc[...] = jnp.zeros(st_sc.shape, F32)

    lb = lb_ref[...]
    xf = hf_ref[0]
    g = jnp.log(lb + (1.0 - lb) * jax.nn.sigmoid(xf))
    kin = (1.0 - lb) * jax.nn.sigmoid(-xf)
    q = _silu(hq_ref[0].astype(F32))
    sums = _dot01_left(wall_ref[...], g)
    gcum = sums[0:C]
    glast = gcum[C - 1:C]
    q_in = (q * jnp.exp(gcum)).astype(BF16)
    k_out = (kin * jnp.exp(glast - gcum)).astype(BF16)
    v = hi_ref[0]
    gate = _silu(hg_ref[0].astype(F32))
    for h in range(HG_HEADS):
        sl = slice(h * HG_EXPAND, (h + 1) * HG_EXPAND)
        qh, kh = q[:, sl], kin[:, sl]
        scores = masks_ref[HG_LEVELS] * _dot_nt(qh.astype(BF16), kh.astype(BF16))
        for lev in range(HG_LEVELS):
            dq = sums[C * (1 + 2 * lev):C * (2 + 2 * lev), sl]
            ek = sums[C * (2 + 2 * lev):C * (3 + 2 * lev), sl]
            qt = (qh * jnp.exp(dq)).astype(BF16)
            kt = (kh * jnp.exp(ek)).astype(BF16)
            scores = scores + masks_ref[lev] * _dot_nt(qt, kt)
        st = st_sc[h]
        vh = v[:, sl]
        o = _dot(scores.astype(BF16), vh) + _dot_nt(q_in[:, sl], st.astype(BF16))
        st_sc[h] = st * jnp.exp(glast[:, sl]) + _dot_tn(vh, k_out[:, sl])
        o_ref[0, :, sl] = (_rms(o, gn_ref[...]) * gate[:, sl]).astype(BF16)


def _hgrn2(hq, hf, hi, hg, lb, g_norm):
    bsz, s_len, _ = hq.shape
    C = HG_CHUNK
    wall, masks = _hgrn_tables()
    wall = jnp.asarray(wall, BF16)
    masks = jnp.asarray(masks, F32)
    chunk = pl.BlockSpec((1, C, HG_WIDTH), lambda b, c: (b, c, 0))
    const = lambda shape: pl.BlockSpec(shape, lambda b, c: (0,) * len(shape))
    return pl.pallas_call(
        _hgrn_kernel,
        grid=(bsz, s_len // C),
        in_specs=[chunk, chunk, chunk, chunk, const((1, HG_KDIM_TOTAL)), const((1, HG_VDIM)),
                  const(wall.shape), const(masks.shape)],
        out_specs=chunk,
        out_shape=jax.ShapeDtypeStruct((bsz, s_len, HG_WIDTH), BF16),
        scratch_shapes=[pltpu.VMEM((HG_HEADS, HG_VDIM, HG_EXPAND), F32)],
        compiler_params=pltpu.CompilerParams(dimension_semantics=("parallel", "arbitrary"),
                                             vmem_limit_bytes=VMEM_LIMIT),
        name="hgrn2_scan",
    )(hq, hf, hi, hg, lb.reshape(1, -1), g_norm.reshape(1, -1), wall, masks)


def kernel(x, norm_mix, norm_ffn, norm_final, a_w_in, a_q_norm, a_w_uq, a_kv_norm, a_w_ukv, a_lq1, a_lk1, a_lq2, a_lk2, a_subln, a_w_out, s_w_in, s_conv_w, s_conv_b, s_dt_bias, s_a_log, s_d, s_norm, h_g_norm, h_lower_bound, s_w_out, ffn_gate, ffn_up, ffn_down):
    bsz, s_len, _ = x.shape
    n_tok = bsz * s_len
    depth = norm_mix.shape[0]
    assert depth == 2 and s_len % 256 == 0
    p_lb = jax.nn.softmax(h_lower_bound.astype(F32), axis=0)
    lb_all = jnp.cumsum(p_lb, axis=0) - p_lb[0:1]

    lambda_init = 0.8 - 0.6 * math.exp(-0.3 * 0)
    q, k, v, dq, dk, dv = _attn_inproj(x, norm_mix[0], a_w_in[0], a_q_norm[0], a_w_uq[0], a_kv_norm[0],
                                       a_w_ukv[0], ts=256)
    o_mla = _mla_attention(q, k, v, tq=256)
    o_diff = _diff_attention(dq, dk, dv, a_lq1[0], a_lk1[0], a_lq2[0], a_lk2[0], a_subln[0], lambda_init, tq=256)
    x2d = _outproj_ffn(x.reshape(n_tok, D_MODEL), o_mla.reshape(n_tok, -1), o_diff.reshape(n_tok, -1),
                       a_w_out[0], norm_ffn[0], ffn_gate[0], ffn_up[0], ffn_down[0], norm_final,
                       final_norm=False, tm=256)

    z, xbc, hq, hf, hi, hg, dt = _rec_inproj(x2d, norm_mix[1], s_w_in[0], tm=256)
    seq = lambda t: t.reshape(bsz, s_len, t.shape[-1])
    y = _ssd(seq(xbc), seq(dt), seq(z), s_conv_w[0], s_conv_b[0], s_dt_bias[0], s_a_log[0], s_d[0], s_norm[0])
    o = _hgrn2(seq(hq), seq(hf), seq(hi), seq(hg), lb_all[1], h_g_norm[0])
    x2d = _outproj_ffn(x2d, y.reshape(n_tok, -1), o.reshape(n_tok, -1), s_w_out[0], norm_ffn[1], ffn_gate[1],
                       ffn_up[1], ffn_down[1], norm_final, final_norm=True, tm=256)
    return x2d.reshape(bsz, s_len, D_MODEL)
```

```python
import functools
import math

import numpy as np
import jax
import jax.numpy as jnp
from jax import lax
from jax.experimental import pallas as pl
from jax.experimental.pallas import tpu as pltpu

F32 = jnp.float32
BF16 = jnp.bfloat16

D_MODEL = 1024
EPS = 1e-6
ROPE_THETA = 10000.0

MLA_HEADS = 8
MLA_Q_LORA = 384
MLA_KV_LORA = 256
MLA_NOPE = 64
MLA_ROPE = 32
MLA_V = 64
DIFF_HEADS = 4
DIFF_HD = 64
DIFF_V = 2 * DIFF_HD

SSM_HEADS = 8
SSM_HEADDIM = 64
SSM_INNER = SSM_HEADS * SSM_HEADDIM
SSM_GROUPS = 2
SSM_STATE = 128
SSM_CONV = 4
SSM_CHUNK = 128
SSM_CONV_DIM = SSM_INNER + 2 * SSM_GROUPS * SSM_STATE
HG_HEADS = 4
HG_EXPAND = 128
HG_VDIM = 128
HG_KDIM_TOTAL = HG_HEADS * HG_EXPAND
HG_WIDTH = HG_HEADS * HG_VDIM
HG_CHUNK = 64

LANES = 128
VMEM_LIMIT = 52 * 1024 * 1024

NT_DIMS = (((1,), (1,)), ((), ()))
TN_DIMS = (((0,), (0,)), ((), ()))


def _dot(a, b):
    return jnp.dot(a, b, preferred_element_type=F32)


def _dot_nt(a, b):
    return lax.dot_general(a, b, NT_DIMS, preferred_element_type=F32)


def _dot_tn(a, b):
    return lax.dot_general(a, b, TN_DIMS, preferred_element_type=F32)


def _rms(x, w):
    return x * lax.rsqrt(jnp.mean(x * x, axis=-1, keepdims=True) + EPS) * w


def _silu(x):
    return x * jax.nn.sigmoid(x)


def _split3(a):
    hi = a.astype(BF16)
    r = a - hi.astype(F32)
    mid = r.astype(BF16)
    lo = (r - mid.astype(F32)).astype(BF16)
    return hi, mid, lo


def _dot01_left(m01, a):
    hi, mid, lo = _split3(a)
    return (_dot(m01, hi) + _dot(m01, mid)) + _dot(m01, lo)


def _dot01_right(a, m01):
    hi, mid, lo = _split3(a)
    return (_dot(hi, m01) + _dot(mid, m01)) + _dot(lo, m01)


def _rope(t, cos, sin_a, sin_b, half):
    return (t * cos + pltpu.roll(t, LANES - half, 1) * sin_a + pltpu.roll(t, half, 1) * sin_b)


A_CQ = 0
A_CKV = A_CQ + MLA_Q_LORA
A_DQ = A_CKV + MLA_KV_LORA
A_DK = A_DQ + DIFF_HEADS * 2 * DIFF_HD
A_KR = A_DK + DIFF_HEADS * 2 * DIFF_HD
A_IN_PACKED = A_KR + LANES


def _attn_inproj_kernel(x_ref, nw_ref, win_ref, wdvt_ref, qn_ref, wuq_ref, kvn_ref, wuk_ref, wuvt_ref,
                        cm_ref, sam_ref, sbm_ref, cd_ref, sad_ref, sbd_ref,
                        q_ref, k_ref, vt_ref, dq_ref, dk_ref, dvt_ref):
    hn = _rms(x_ref[0], nw_ref[...]).astype(BF16)
    proj = _dot(hn, win_ref[...])
    cq = _rms(proj[:, A_CQ:A_CKV], qn_ref[...]).astype(BF16)
    ckv = _rms(proj[:, A_CKV:A_DQ], kvn_ref[...]).astype(BF16)
    q = _dot(cq, wuq_ref[...])
    kn = _dot(ckv, wuk_ref[...])
    vt_ref[0] = _dot_nt(wuvt_ref[...], ckv).astype(BF16)
    dvt_ref[0] = _dot_nt(wdvt_ref[...], hn).astype(BF16)
    cm, sam, sbm = cm_ref[...], sam_ref[...], sbm_ref[...]
    kpe = _rope(proj[:, A_KR:A_KR + LANES], cm, sam, sbm, MLA_ROPE // 2)
    for h in range(MLA_HEADS):
        sl = slice(h * LANES, (h + 1) * LANES)
        q_ref[0, h] = _rope(q[:, sl], cm, sam, sbm, MLA_ROPE // 2).astype(BF16)
        k_ref[0, h] = (kn[:, sl] + kpe).astype(BF16)
    cd, sad, sbd = cd_ref[...], sad_ref[...], sbd_ref[...]
    for g in range(DIFF_HEADS):
        sl = slice(g * LANES, (g + 1) * LANES)
        dq = proj[:, A_DQ + g * LANES:A_DQ + (g + 1) * LANES]
        dk = proj[:, A_DK + g * LANES:A_DK + (g + 1) * LANES]
        dq_ref[0, :, sl] = (_rope(dq, cd, sad, sbd, DIFF_HD // 2) * (DIFF_HD ** -0.5)).astype(BF16)
        dk_ref[0, :, sl] = _rope(dk, cd, sad, sbd, DIFF_HD // 2).astype(BF16)


def _rope_tables(seq_len, dim, lane_offsets):
    half = dim // 2
    inv_freq = 1.0 / (ROPE_THETA ** (jnp.arange(0, dim, 2, dtype=F32) / dim))
    ang = jnp.arange(seq_len, dtype=F32)[:, None] * inv_freq[None, :]
    cos, sin = jnp.cos(ang), jnp.sin(ang)
    zero = jnp.zeros_like(sin)
    c = jnp.ones((seq_len, LANES), F32)
    sa = jnp.zeros((seq_len, LANES), F32)
    sb = jnp.zeros((seq_len, LANES), F32)
    for off in lane_offsets:
        c = c.at[:, off:off + dim].set(jnp.concatenate([cos, cos], axis=-1))
        sa = sa.at[:, off:off + dim].set(jnp.concatenate([-sin, zero], axis=-1))
        sb = sb.at[:, off:off + dim].set(jnp.concatenate([zero, sin], axis=-1))
    return c, sa, sb


def _const_spec(shape):
    nd = len(shape)
    return pl.BlockSpec(shape, lambda *_: (0,) * nd, pipeline_mode=pl.Buffered(1))


def _attn_inproj(x, norm_w, w_in, q_norm, w_uq, kv_norm, w_ukv, ts):
    bsz, s_len, _ = x.shape
    cq, ckv, kr, dq, dk, dv = jnp.split(
        w_in, [int(v) for v in np.cumsum([MLA_Q_LORA, MLA_KV_LORA, MLA_ROPE, 512, 512])], axis=-1)
    kr_pad = jnp.pad(kr, ((0, 0), (MLA_NOPE, LANES - MLA_NOPE - MLA_ROPE)))
    win_p = jnp.concatenate([cq, ckv, dq, dk, kr_pad], axis=-1).astype(BF16)
    wdvt = dv.T.astype(BF16)
    scale = (MLA_NOPE + MLA_ROPE) ** -0.5
    wuq_p = jnp.pad((w_uq * scale).reshape(MLA_Q_LORA, MLA_HEADS, MLA_NOPE + MLA_ROPE),
                    ((0, 0), (0, 0), (0, LANES - MLA_NOPE - MLA_ROPE)))
    wuq_p = wuq_p.reshape(MLA_Q_LORA, MLA_HEADS * LANES).astype(BF16)
    wkv = w_ukv.reshape(MLA_KV_LORA, MLA_HEADS, MLA_NOPE + MLA_V)
    wuk_p = jnp.pad(wkv[..., :MLA_NOPE], ((0, 0), (0, 0), (0, LANES - MLA_NOPE)))
    wuk_p = wuk_p.reshape(MLA_KV_LORA, MLA_HEADS * LANES).astype(BF16)
    wuvt = wkv[..., MLA_NOPE:].reshape(MLA_KV_LORA, MLA_HEADS * MLA_V).T.astype(BF16)
    tabs_m = _rope_tables(s_len, MLA_ROPE, (MLA_NOPE,))
    tabs_d = _rope_tables(s_len, DIFF_HD, (0, DIFF_HD))

    row = lambda b, i: (b, i, 0)
    tab = pl.BlockSpec((ts, LANES), lambda b, i: (i, 0))
    head_major = pl.BlockSpec((1, MLA_HEADS, ts, LANES), lambda b, i: (b, 0, i, 0))
    wide = pl.BlockSpec((1, ts, 512), row)
    transposed = pl.BlockSpec((1, 512, ts), lambda b, i: (b, 0, i))
    return pl.pallas_call(
        _attn_inproj_kernel,
        grid=(bsz, s_len // ts),
        in_specs=[pl.BlockSpec((1, ts, D_MODEL), row), _const_spec((1, D_MODEL)),
                  _const_spec(win_p.shape), _const_spec(wdvt.shape), _const_spec((1, MLA_Q_LORA)),
                  _const_spec(wuq_p.shape), _const_spec((1, MLA_KV_LORA)), _const_spec(wuk_p.shape),
                  _const_spec(wuvt.shape), tab, tab, tab, tab, tab, tab],
        out_specs=[head_major, head_major, transposed, wide, wide, transposed],
        out_shape=[jax.ShapeDtypeStruct((bsz, MLA_HEADS, s_len, LANES), BF16),
                   jax.ShapeDtypeStruct((bsz, MLA_HEADS, s_len, LANES), BF16),
                   jax.ShapeDtypeStruct((bsz, 512, s_len), BF16),
                   jax.ShapeDtypeStruct((bsz, s_len, 512), BF16),
                   jax.ShapeDtypeStruct((bsz, s_len, 512), BF16),
                   jax.ShapeDtypeStruct((bsz, 512, s_len), BF16)],
        compiler_params=pltpu.CompilerParams(dimension_semantics=("parallel", "parallel"),
                                             vmem_limit_bytes=VMEM_LIMIT),
        name="attn_inproj",
    )(x, norm_w.reshape(1, -1), win_p, wdvt, q_norm.reshape(1, -1), wuq_p, kv_norm.reshape(1, -1), wuk_p, wuvt,
      *tabs_m, *tabs_d)


def _attn_sweep(i, tq, n_blocks, fill_scores, vt_ref, sa_sc, sb_sc, m_sc, l_sc, acc_sc):
    nq = sa_sc.shape[1]
    m_sc[...] = jnp.full(m_sc.shape, -jnp.inf, F32)
    l_sc[...] = jnp.zeros(l_sc.shape, F32)
    acc_sc[...] = jnp.zeros(acc_sc.shape, F32)

    def start_of(blk):
        return pl.multiple_of(jnp.minimum(blk, n_blocks - 1) * tq, tq)

    def process(s, start):
        m_prev = m_sc[...]
        m_new = jnp.maximum(m_prev, jnp.max(s, axis=0, keepdims=True))
        alpha = jnp.exp(m_prev - m_new)
        p = jnp.exp(s - m_new)
        l_sc[...] = alpha * l_sc[...] + jnp.sum(p, axis=0, keepdims=True)
        acc_sc[...] = alpha * acc_sc[...] + _dot(vt_ref[0, :, pl.ds(start, tq)], p.astype(BF16))
        m_sc[...] = m_new

    def masked(s, blk):
        key = blk * tq + lax.broadcasted_iota(jnp.int32, (tq, nq), 0)
        qry = i * tq + (lax.broadcasted_iota(jnp.int32, (tq, nq), 1) & (tq - 1))
        return jnp.where(key <= qry, s, -jnp.inf)

    fill_scores(sa_sc, start_of(0))

    def body(jj, carry):
        a = 2 * jj
        fill_scores(sb_sc, start_of(a + 1))
        process(sa_sc[...], start_of(a))
        fill_scores(sa_sc, start_of(a + 2))
        process(sb_sc[...], start_of(a + 1))
        return carry

    n_main = i // 2
    lax.fori_loop(0, n_main, body, 0)
    a = 2 * n_main
    fill_scores(sb_sc, start_of(a + 1))
    process(masked(sa_sc[...], a), start_of(a))
    process(masked(sb_sc[...], a + 1), start_of(a + 1))


def _attn_scratch(tq):
    nq = 2 * tq
    return [pltpu.VMEM((tq, nq), F32), pltpu.VMEM((tq, nq), F32), pltpu.VMEM((1, nq), F32),
            pltpu.VMEM((1, nq), F32), pltpu.VMEM((LANES, nq), F32)]


def _mla_attn_kernel(q_ref, k_ref, vt_ref, o_ref, sa_sc, sb_sc, m_sc, l_sc, acc_sc, *, tq, n_blocks):
    def fill_scores(buf, start):
        for h in range(2):
            buf[:, h * tq:(h + 1) * tq] = _dot_nt(k_ref[0, h, pl.ds(start, tq), :], q_ref[0, h])

    _attn_sweep(pl.program_id(2), tq, n_blocks, fill_scores, vt_ref, sa_sc, sb_sc, m_sc, l_sc, acc_sc)
    o_t = jnp.concatenate([acc_sc[0:MLA_V, 0:tq] / l_sc[:, 0:tq],
                           acc_sc[MLA_V:2 * MLA_V, tq:2 * tq] / l_sc[:, tq:2 * tq]], axis=0)
    o_ref[0] = o_t.T.astype(BF16)


def _mla_attention(q, k, vt, tq):
    bsz, nh, s_len, _ = q.shape
    return pl.pallas_call(
        functools.partial(_mla_attn_kernel, tq=tq, n_blocks=s_len // tq),
        grid=(bsz, nh // 2, s_len // tq),
        in_specs=[pl.BlockSpec((1, 2, tq, LANES), lambda b, p, i: (b, p, i, 0)),
                  pl.BlockSpec((1, 2, s_len, LANES), lambda b, p, i: (b, p, 0, 0)),
                  pl.BlockSpec((1, LANES, s_len), lambda b, p, i: (b, p, 0))],
        out_specs=pl.BlockSpec((1, tq, LANES), lambda b, p, i: (b, i, p)),
        out_shape=jax.ShapeDtypeStruct((bsz, s_len, nh * MLA_V), BF16),
        scratch_shapes=_attn_scratch(tq),
        compiler_params=pltpu.CompilerParams(dimension_semantics=("parallel", "parallel", "arbitrary"),
                                             vmem_limit_bytes=VMEM_LIMIT),
        name="mla_attention",
    )(q, k, vt)


def _diff_attn_kernel(q_ref, k_ref, vt_ref, lq1_ref, lk1_ref, lq2_ref, lk2_ref, sub_ref, o_ref,
                      qs_sc, sa_sc, sb_sc, m_sc, l_sc, acc_sc, *, tq, n_blocks, lambda_init):
    q = q_ref[0]
    lane = lax.broadcasted_iota(jnp.int32, q.shape, 1)
    zero = jnp.zeros_like(q)
    qs_sc[0:tq] = jnp.where(lane < DIFF_HD, q, zero)
    qs_sc[tq:2 * tq] = jnp.where(lane < DIFF_HD, zero, q)

    def fill_scores(buf, start):
        buf[...] = _dot_nt(k_ref[0, pl.ds(start, tq), :], qs_sc[...])

    _attn_sweep(pl.program_id(2), tq, n_blocks, fill_scores, vt_ref, sa_sc, sb_sc, m_sc, l_sc, acc_sc)
    o_t = acc_sc[...] / l_sc[...]
    lam = (jnp.exp(jnp.sum(lq1_ref[...] * lk1_ref[...], axis=-1, keepdims=True))
           - jnp.exp(jnp.sum(lq2_ref[...] * lk2_ref[...], axis=-1, keepdims=True)) + lambda_init)
    od = (o_t[:, 0:tq] - lam * o_t[:, tq:2 * tq]).T
    o_ref[0] = (_rms(od, sub_ref[...]) * (1.0 - lambda_init)).astype(BF16)


def _diff_attention(dq, dk, dvt, lq1, lk1, lq2, lk2, subln, lambda_init, tq):
    bsz, s_len, _ = dq.shape
    qspec = pl.BlockSpec((1, tq, LANES), lambda b, h, i: (b, i, h))
    vec = lambda n: pl.BlockSpec((1, n), lambda b, h, i: (0, 0))
    return pl.pallas_call(
        functools.partial(_diff_attn_kernel, tq=tq, n_blocks=s_len // tq, lambda_init=lambda_init),
        grid=(bsz, DIFF_HEADS, s_len // tq),
        in_specs=[qspec, pl.BlockSpec((1, s_len, LANES), lambda b, h, i: (b, 0, h)),
                  pl.BlockSpec((1, LANES, s_len), lambda b, h, i: (b, h, 0)),
                  vec(DIFF_HD), vec(DIFF_HD), vec(DIFF_HD), vec(DIFF_HD), vec(DIFF_V)],
        out_specs=qspec,
        out_shape=jax.ShapeDtypeStruct((bsz, s_len, DIFF_HEADS * DIFF_V), BF16),
        scratch_shapes=[pltpu.VMEM((2 * tq, LANES), BF16)] + _attn_scratch(tq),
        compiler_params=pltpu.CompilerParams(dimension_semantics=("parallel", "parallel", "arbitrary"),
                                             vmem_limit_bytes=VMEM_LIMIT),
        name="diff_attention",
    )(dq, dk, dvt, lq1.reshape(1, -1), lk1.reshape(1, -1), lq2.reshape(1, -1), lk2.reshape(1, -1),
      subln.reshape(1, -1))


def _outproj_ffn_kernel(x_ref, ma_ref, mb_ref, woa_ref, wob_ref, nf_ref, wg_ref, wu_ref, wd_ref, fin_ref,
                        o_ref, *, final_norm):
    x1 = x_ref[...] + _dot(ma_ref[...], woa_ref[...]) + _dot(mb_ref[...], wob_ref[...])
    h = _rms(x1, nf_ref[...]).astype(BF16)
    g = _dot(h, wg_ref[...])
    u = _dot(h, wu_ref[...])
    a = (_silu(g) * u).astype(BF16)
    x2 = x1 + _dot(a, wd_ref[...])
    if final_norm:
        x2 = _rms(x2, fin_ref[...])
    o_ref[...] = x2


def _outproj_ffn(x2d, mix_a, mix_b, w_out, norm_ffn, w_gate, w_up, w_down, norm_final, final_norm, tm):
    n_tok = x2d.shape[0]
    half = mix_a.shape[1]
    d_ff = w_gate.shape[1]
    woa = w_out[:half].astype(BF16)
    wob = w_out[half:].astype(BF16)
    row = lambda i: (i, 0)
    return pl.pallas_call(
        functools.partial(_outproj_ffn_kernel, final_norm=final_norm),
        grid=(n_tok // tm,),
        in_specs=[pl.BlockSpec((tm, D_MODEL), row), pl.BlockSpec((tm, half), row), pl.BlockSpec((tm, half), row),
                  _const_spec(woa.shape), _const_spec(wob.shape), _const_spec((1, D_MODEL)),
                  _const_spec((D_MODEL, d_ff)), _const_spec((D_MODEL, d_ff)), _const_spec((d_ff, D_MODEL)),
                  _const_spec((1, D_MODEL))],
        out_specs=pl.BlockSpec((tm, D_MODEL), row),
        out_shape=jax.ShapeDtypeStruct((n_tok, D_MODEL), F32),
        compiler_params=pltpu.CompilerParams(dimension_semantics=("parallel",), vmem_limit_bytes=VMEM_LIMIT),
        name="outproj_ffn",
    )(x2d, mix_a, mix_b, woa, wob, norm_ffn.reshape(1, -1), w_gate.astype(BF16), w_up.astype(BF16),
      w_down.astype(BF16), norm_final.reshape(1, -1))


S_Z = 0
S_XBC = S_Z + SSM_INNER
S_HQ = S_XBC + SSM_CONV_DIM
S_HF = S_HQ + HG_KDIM_TOTAL
S_HI = S_HF + HG_KDIM_TOTAL
S_HG = S_HI + HG_WIDTH
S_DT = S_HG + HG_WIDTH
S_IN_PACKED = S_DT + LANES


def _rec_inproj_kernel(x_ref, nw_ref, win_ref, z_ref, xbc_ref, hq_ref, hf_ref, hi_ref, hg_ref, dt_ref):
    hn = _rms(x_ref[...], nw_ref[...]).astype(BF16)
    proj = _dot(hn, win_ref[...])
    z_ref[...] = proj[:, S_Z:S_XBC].astype(BF16)
    xbc_ref[...] = proj[:, S_XBC:S_HQ]
    hq_ref[...] = proj[:, S_HQ:S_HF].astype(BF16)
    hf_ref[...] = proj[:, S_HF:S_HI]
    hi_ref[...] = proj[:, S_HI:S_HG].astype(BF16)
    hg_ref[...] = proj[:, S_HG:S_DT].astype(BF16)
    dt_ref[...] = proj[:, S_DT:S_IN_PACKED]


def _rec_inproj(x2d, norm_w, w_in, tm):
    n_tok = x2d.shape[0]
    z, xbc, dt, hq, hf, hi, hg = jnp.split(
        w_in, [int(v) for v in np.cumsum([SSM_INNER, SSM_CONV_DIM, SSM_HEADS, 512, 512, 512])], axis=-1)
    dt_pad = jnp.pad(dt, ((0, 0), (0, LANES - SSM_HEADS)))
    win_p = jnp.concatenate([z, xbc, hq, hf, hi, hg, dt_pad], axis=-1).astype(BF16)
    row = lambda i: (i, 0)
    widths = [(SSM_INNER, BF16), (SSM_CONV_DIM, F32), (512, BF16), (512, F32), (512, BF16), (512, BF16),
              (LANES, F32)]
    return pl.pallas_call(
        _rec_inproj_kernel,
        grid=(n_tok // tm,),
        in_specs=[pl.BlockSpec((tm, D_MODEL), row), _const_spec((1, D_MODEL)), _const_spec(win_p.shape)],
        out_specs=[pl.BlockSpec((tm, w), row) for w, _ in widths],
        out_shape=[jax.ShapeDtypeStruct((n_tok, w), dt_) for w, dt_ in widths],
        compiler_params=pltpu.CompilerParams(dimension_semantics=("parallel",), vmem_limit_bytes=VMEM_LIMIT),
        name="rec_inproj",
    )(x2d, norm_w.reshape(1, -1), win_p)


CONV_PAD = 8


def _ssd_kernel(xbc_ref, dt_ref, z_ref, cw_ref, cb_ref, dtb_ref, ah_ref, dsk_ref, nw_ref, tri_ref, exp_ref,
                y_ref, xpad_sc, st_sc):
    L = SSM_CHUNK
    heads_per_group = SSM_HEADS // SSM_GROUPS
    gw = heads_per_group * SSM_HEADDIM

    @pl.when(pl.program_id(1) == 0)
    def _():
        xpad_sc[0:CONV_PAD] = jnp.zeros((CONV_PAD, SSM_CONV_DIM), F32)
        st_sc[...] = jnp.zeros(st_sc.shape, F32)

    xt = xbc_ref[0]
    xpad_sc[CONV_PAD:CONV_PAD + L] = xt
    conv = cb_ref[...] + cw_ref[SSM_CONV - 1:SSM_CONV] * xt
    for d in range(1, SSM_CONV):
        conv = conv + cw_ref[SSM_CONV - 1 - d:SSM_CONV - d] * xpad_sc[CONV_PAD - d:CONV_PAD - d + L]
    xpad_sc[0:CONV_PAD] = xt[L - CONV_PAD:L]
    xc = _silu(conv)
    xs = xc[:, 0:SSM_INNER]
    b_in = xc[:, SSM_INNER:SSM_INNER + SSM_GROUPS * SSM_STATE].astype(BF16)
    c_in = xc[:, SSM_INNER + SSM_GROUPS * SSM_STATE:].astype(BF16)

    dt = jax.nn.softplus(dt_ref[0] + dtb_ref[...])
    a = dt * ah_ref[...]
    a_cs = _dot01_left(tri_ref[...], a)
    a_cs_t = a_cs.T
    dt_e = _dot01_right(dt, exp_ref[...])
    acs_e = _dot01_right(a_cs, exp_ref[...])
    alast_e = acs_e[L - 1:L, :]
    xdt = xs * dt_e
    xdec = (xdt * jnp.exp(alast_e - acs_e)).astype(BF16)
    xdt_b = xdt.astype(BF16)
    eacs = jnp.exp(acs_e)
    r = lax.broadcasted_iota(jnp.int32, (L, L), 0)
    c = lax.broadcasted_iota(jnp.int32, (L, L), 1)
    causal = c <= r
    lane = lax.broadcasted_iota(jnp.int32, (L, LANES), 1)

    ys = []
    for g in range(SSM_GROUPS):
        bg = b_in[:, g * SSM_STATE:(g + 1) * SSM_STATE]
        cg = c_in[:, g * SSM_STATE:(g + 1) * SSM_STATE]
        cb = _dot_nt(cg, bg)
        st_prev = st_sc[g]
        y_off = _dot(cg, st_prev.astype(BF16)) * eacs[:, g * gw:(g + 1) * gw]
        st_sc[g] = st_prev * jnp.exp(alast_e[:, g * gw:(g + 1) * gw]) + _dot_tn(bg, xdec[:, g * gw:(g + 1) * gw])
        for pr in range(heads_per_group // 2):
            xpair = xdt_b[:, g * gw + pr * LANES:g * gw + (pr + 1) * LANES]
            res = []
            for hh in range(2):
                h = g * heads_per_group + 2 * pr + hh
                seg = jnp.exp(jnp.minimum(a_cs[:, h:h + 1] - a_cs_t[h:h + 1, :], 0.0))
                m = jnp.where(causal, cb * seg, 0.0).astype(BF16)
                res.append(_dot(m, xpair))
            ys.append(jnp.where(lane < SSM_HEADDIM, res[0], res[1]) + y_off[:, pr * LANES:(pr + 1) * LANES])
    y = jnp.concatenate(ys, axis=1) + dsk_ref[...] * xs
    y = y * _silu(z_ref[0].astype(F32))
    nw = nw_ref[...]
    for g in range(SSM_GROUPS):
        sl = slice(g * gw, (g + 1) * gw)
        y_ref[0, :, sl] = _rms(y[:, sl], nw[:, sl]).astype(BF16)


def _ssd(xbc, dt, z, conv_w, conv_b, dt_bias, a_log, d_skip, ssm_norm):
    bsz, s_len, _ = xbc.shape
    L = SSM_CHUNK
    pad = lambda v: jnp.pad(v.astype(F32), (0, LANES - SSM_HEADS)).reshape(1, LANES)
    a_head = -jnp.exp(a_log.astype(F32))
    tri = jnp.asarray(np.tril(np.ones((L, L), np.float32)), BF16)
    expand = np.zeros((LANES, SSM_INNER), np.float32)
    for h in range(SSM_HEADS):
        expand[h, h * SSM_HEADDIM:(h + 1) * SSM_HEADDIM] = 1.0
    expand = jnp.asarray(expand, BF16)
    dsk = jnp.repeat(d_skip.astype(F32), SSM_HEADDIM).reshape(1, SSM_INNER)
    chunk = lambda w: pl.BlockSpec((1, L, w), lambda b, c: (b, c, 0))
    const = lambda shape: pl.BlockSpec(shape, lambda b, c: (0,) * len(shape))
    return pl.pallas_call(
        _ssd_kernel,
        grid=(bsz, s_len // L),
        in_specs=[chunk(SSM_CONV_DIM), chunk(LANES), chunk(SSM_INNER), const((SSM_CONV, SSM_CONV_DIM)),
                  const((1, SSM_CONV_DIM)), const((1, LANES)), const((1, LANES)), const((1, SSM_INNER)),
                  const((1, SSM_INNER)), const((L, L)), const((LANES, SSM_INNER))],
        out_specs=chunk(SSM_INNER),
        out_shape=jax.ShapeDtypeStruct((bsz, s_len, SSM_INNER), BF16),
        scratch_shapes=[pltpu.VMEM((CONV_PAD + L, SSM_CONV_DIM), F32),
                        pltpu.VMEM((SSM_GROUPS, SSM_STATE, SSM_INNER // SSM_GROUPS), F32)],
        compiler_params=pltpu.CompilerParams(dimension_semantics=("parallel", "arbitrary"),
                                             vmem_limit_bytes=VMEM_LIMIT),
        name="ssd_scan",
    )(xbc, dt, z, conv_w, conv_b.reshape(1, -1), pad(dt_bias), pad(a_head), dsk, ssm_norm.reshape(1, -1),
      tri, expand)


HG_LEVELS = int(math.log2(HG_CHUNK))


def _hgrn_tables():
    C = HG_CHUNK
    idx = np.arange(C)
    mats = [np.tril(np.ones((C, C), np.float32))]
    masks = []
    for lev in range(HG_LEVELS):
        h = C >> (lev + 1)
        mid = (idx // (2 * h)) * (2 * h) + h - 1
        upper = (idx % (2 * h)) >= h
        j = idx[None, :]
        dq = (upper[:, None] & (j > mid[:, None]) & (j <= idx[:, None])).astype(np.float32)
        ek = ((~upper)[:, None] & (j > idx[:, None]) & (j <= mid[:, None])).astype(np.float32)
        mats += [dq, ek]
        same = (idx[:, None] // (2 * h)) == (idx[None, :] // (2 * h))
        masks.append((same & upper[:, None] & (~upper)[None, :]).astype(np.float32))
    masks.append(np.eye(C, dtype=np.float32))
    return np.concatenate(mats, axis=0), np.stack(masks, axis=0)


def _hgrn_kernel(hq_ref, hf_ref, hi_ref, hg_ref, lb_ref, gn_ref, wall_ref, masks_ref, o_ref, st_sc):
    C = HG_CHUNK

    @pl.when(pl.program_id(1) == 0)
    def _():
        st_sc[...] = jnp.zeros(st_sc.shape, F32)

    lb = lb_ref[...]
    xf = hf_ref[0]
    g = jnp.log(lb + (1.0 - lb) * jax.nn.sigmoid(xf))
    kin = (1.0 - lb) * jax.nn.sigmoid(-xf)
    q = _silu(hq_ref[0].astype(F32))
    sums = _dot01_left(wall_ref[...], g)
    gcum = sums[0:C]
    glast = gcum[C - 1:C]
    q_in = (q * jnp.exp(gcum)).astype(BF16)
    k_out = (kin * jnp.exp(glast - gcum)).astype(BF16)
    v = hi_ref[0]
    gate = _silu(hg_ref[0].astype(F32))
    for h in range(HG_HEADS):
        sl = slice(h * HG_EXPAND, (h + 1) * HG_EXPAND)
        qh, kh = q[:, sl], kin[:, sl]
        scores = masks_ref[HG_LEVELS] * _dot_nt(qh.astype(BF16), kh.astype(BF16))
        for lev in range(HG_LEVELS):
            dq = sums[C * (1 + 2 * lev):C * (2 + 2 * lev), sl]
            ek = sums[C * (2 + 2 * lev):C * (3 + 2 * lev), sl]
            qt = (qh * jnp.exp(dq)).astype(BF16)
            kt = (kh * jnp.exp(ek)).astype(BF16)
            scores = scores + masks_ref[lev] * _dot_nt(qt, kt)
        st = st_sc[h]
        vh = v[:, sl]
        o = _dot(scores.astype(BF16), vh) + _dot_nt(q_in[:, sl], st.astype(BF16))
        st_sc[h] = st * jnp.exp(glast[:, sl]) + _dot_tn(vh, k_out[:, sl])
        o_ref[0, :, sl] = (_rms(o, gn_ref[...]) * gate[:, sl]).astype(BF16)


def _hgrn2(hq, hf, hi, hg, lb, g_norm):
    bsz, s_len, _ = hq.shape
    C = HG_CHUNK
    wall, masks = _hgrn_tables()
    wall = jnp.asarray(wall, BF16)
    masks = jnp.asarray(masks, F32)
    chunk = pl.BlockSpec((1, C, HG_WIDTH), lambda b, c: (b, c, 0))
    const = lambda shape: pl.BlockSpec(shape, lambda b, c: (0,) * len(shape))
    return pl.pallas_call(
        _hgrn_kernel,
        grid=(bsz, s_len // C),
        in_specs=[chunk, chunk, chunk, chunk, const((1, HG_KDIM_TOTAL)), const((1, HG_VDIM)),
                  const(wall.shape), const(masks.shape)],
        out_specs=chunk,
        out_shape=jax.ShapeDtypeStruct((bsz, s_len, HG_WIDTH), BF16),
        scratch_shapes=[pltpu.VMEM((HG_HEADS, HG_VDIM, HG_EXPAND), F32)],
        compiler_params=pltpu.CompilerParams(dimension_semantics=("parallel", "arbitrary"),
                                             vmem_limit_bytes=VMEM_LIMIT),
        name="hgrn2_scan",
    )(hq, hf, hi, hg, lb.reshape(1, -1), g_norm.reshape(1, -1), wall, masks)


def kernel(x, norm_mix, norm_ffn, norm_final, a_w_in, a_q_norm, a_w_uq, a_kv_norm, a_w_ukv, a_lq1, a_lk1, a_lq2, a_lk2, a_subln, a_w_out, s_w_in, s_conv_w, s_conv_b, s_dt_bias, s_a_log, s_d, s_norm, h_g_norm, h_lower_bound, s_w_out, ffn_gate, ffn_up, ffn_down):
    bsz, s_len, _ = x.shape
    n_tok = bsz * s_len
    depth = norm_mix.shape[0]
    assert depth == 2 and s_len % 256 == 0
    p_lb = jax.nn.softmax(h_lower_bound.astype(F32), axis=0)
    lb_all = jnp.cumsum(p_lb, axis=0) - p_lb[0:1]

    lambda_init = 0.8 - 0.6 * math.exp(-0.3 * 0)
    q, k, vt, dq, dk, dvt = _attn_inproj(x, norm_mix[0], a_w_in[0], a_q_norm[0], a_w_uq[0], a_kv_norm[0],
                                       a_w_ukv[0], ts=256)
    o_mla = _mla_attention(q, k, vt, tq=256)
    o_diff = _diff_attention(dq, dk, dvt, a_lq1[0], a_lk1[0], a_lq2[0], a_lk2[0], a_subln[0], lambda_init, tq=256)
    x2d = _outproj_ffn(x.reshape(n_tok, D_MODEL), o_mla.reshape(n_tok, -1), o_diff.reshape(n_tok, -1),
                       a_w_out[0], norm_ffn[0], ffn_gate[0], ffn_up[0], ffn_down[0], norm_final,
                       final_norm=False, tm=256)

    z, xbc, hq, hf, hi, hg, dt = _rec_inproj(x2d, norm_mix[1], s_w_in[0], tm=256)
    seq = lambda t: t.reshape(bsz, s_len, t.shape[-1])
    y = _ssd(seq(xbc), seq(dt), seq(z), s_conv_w[0], s_conv_b[0], s_dt_bias[0], s_a_log[0], s_d[0], s_norm[0])
    o = _hgrn2(seq(hq), seq(hf), seq(hi), seq(hg), lb_all[1], h_g_norm[0])
    x2d = _outproj_ffn(x2d, y.reshape(n_tok, -1), o.reshape(n_tok, -1), s_w_out[0], norm_ffn[1], ffn_gate[1],
                       ffn_up[1], ffn_down[1], norm_final, final_norm=True, tm=256)
    return x2d.reshape(bsz, s_len, D_MODEL)
```

```python
import functools
import math

import numpy as np
import jax
import jax.numpy as jnp
from jax import lax
from jax.experimental import pallas as pl
from jax.experimental.pallas import tpu as pltpu

F32 = jnp.float32
BF16 = jnp.bfloat16

D_MODEL = 1024
EPS = 1e-6
ROPE_THETA = 10000.0

MLA_HEADS = 8
MLA_Q_LORA = 384
MLA_KV_LORA = 256
MLA_NOPE = 64
MLA_ROPE = 32
MLA_V = 64
DIFF_HEADS = 4
DIFF_HD = 64
DIFF_V = 2 * DIFF_HD

SSM_HEADS = 8
SSM_HEADDIM = 64
SSM_INNER = SSM_HEADS * SSM_HEADDIM
SSM_GROUPS = 2
SSM_STATE = 128
SSM_CONV = 4
SSM_CHUNK = 128
SSM_CONV_DIM = SSM_INNER + 2 * SSM_GROUPS * SSM_STATE
HG_HEADS = 4
HG_EXPAND = 128
HG_VDIM = 128
HG_KDIM_TOTAL = HG_HEADS * HG_EXPAND
HG_WIDTH = HG_HEADS * HG_VDIM
HG_CHUNK = 64

LANES = 128
VMEM_LIMIT = 52 * 1024 * 1024

NT_DIMS = (((1,), (1,)), ((), ()))
TN_DIMS = (((0,), (0,)), ((), ()))


def _dot(a, b):
    return jnp.dot(a, b, preferred_element_type=F32)


def _dot_nt(a, b):
    return lax.dot_general(a, b, NT_DIMS, preferred_element_type=F32)


def _dot_tn(a, b):
    return lax.dot_general(a, b, TN_DIMS, preferred_element_type=F32)


def _rms(x, w):
    return x * lax.rsqrt(jnp.mean(x * x, axis=-1, keepdims=True) + EPS) * w


def _silu(x):
    return x * jax.nn.sigmoid(x)


def _split3(a):
    hi = a.astype(BF16)
    r = a - hi.astype(F32)
    mid = r.astype(BF16)
    lo = (r - mid.astype(F32)).astype(BF16)
    return hi, mid, lo


def _dot01_left(m01, a):
    hi, mid, lo = _split3(a)
    return (_dot(m01, hi) + _dot(m01, mid)) + _dot(m01, lo)


def _dot01_right(a, m01):
    hi, mid, lo = _split3(a)
    return (_dot(hi, m01) + _dot(mid, m01)) + _dot(lo, m01)


def _rope(t, cos, sin_a, sin_b, half):
    return (t * cos + pltpu.roll(t, LANES - half, 1) * sin_a + pltpu.roll(t, half, 1) * sin_b)


A_CQ = 0
A_CKV = A_CQ + MLA_Q_LORA
A_DQ = A_CKV + MLA_KV_LORA
A_DK = A_DQ + DIFF_HEADS * 2 * DIFF_HD
A_KR = A_DK + DIFF_HEADS * 2 * DIFF_HD
A_IN_PACKED = A_KR + LANES


def _attn_inproj_kernel(x_ref, nw_ref, win_ref, wdvt_ref, qn_ref, wuq_ref, kvn_ref, wuk_ref, wuvt_ref,
                        cm_ref, sam_ref, sbm_ref, cd_ref, sad_ref, sbd_ref,
                        q_ref, k_ref, vt_ref, dq_ref, dk_ref, dvt_ref):
    hn = _rms(x_ref[0], nw_ref[...]).astype(BF16)
    proj = _dot(hn, win_ref[...])
    cq = _rms(proj[:, A_CQ:A_CKV], qn_ref[...]).astype(BF16)
    ckv = _rms(proj[:, A_CKV:A_DQ], kvn_ref[...]).astype(BF16)
    q = _dot(cq, wuq_ref[...])
    kn = _dot(ckv, wuk_ref[...])
    vt = _dot_nt(wuvt_ref[...], ckv).astype(BF16)
    dvt = _dot_nt(wdvt_ref[...], hn).astype(BF16)
    ones = jnp.ones((VT_ONES, vt.shape[1]), BF16)
    for g in range(4):
        vt_ref[0, g, 0:LANES] = vt[g * LANES:(g + 1) * LANES]
        vt_ref[0, g, LANES:VT_ROWS] = ones
        dvt_ref[0, g, 0:LANES] = dvt[g * LANES:(g + 1) * LANES]
        dvt_ref[0, g, LANES:VT_ROWS] = ones
    cm, sam, sbm = cm_ref[...], sam_ref[...], sbm_ref[...]
    kpe = _rope(proj[:, A_KR:A_KR + LANES], cm, sam, sbm, MLA_ROPE // 2)
    for h in range(MLA_HEADS):
        sl = slice(h * LANES, (h + 1) * LANES)
        q_ref[0, h] = _rope(q[:, sl], cm, sam, sbm, MLA_ROPE // 2).astype(BF16)
        k_ref[0, h] = (kn[:, sl] + kpe).astype(BF16)
    cd, sad, sbd = cd_ref[...], sad_ref[...], sbd_ref[...]
    for g in range(DIFF_HEADS):
        sl = slice(g * LANES, (g + 1) * LANES)
        dq = proj[:, A_DQ + g * LANES:A_DQ + (g + 1) * LANES]
        dk = proj[:, A_DK + g * LANES:A_DK + (g + 1) * LANES]
        dq_ref[0, :, sl] = (_rope(dq, cd, sad, sbd, DIFF_HD // 2) * (DIFF_HD ** -0.5 * LOG2E)).astype(BF16)
        dk_ref[0, :, sl] = _rope(dk, cd, sad, sbd, DIFF_HD // 2).astype(BF16)


def _rope_tables(seq_len, dim, lane_offsets):
    half = dim // 2
    inv_freq = 1.0 / (ROPE_THETA ** (jnp.arange(0, dim, 2, dtype=F32) / dim))
    ang = jnp.arange(seq_len, dtype=F32)[:, None] * inv_freq[None, :]
    cos, sin = jnp.cos(ang), jnp.sin(ang)
    zero = jnp.zeros_like(sin)
    c = jnp.ones((seq_len, LANES), F32)
    sa = jnp.zeros((seq_len, LANES), F32)
    sb = jnp.zeros((seq_len, LANES), F32)
    for off in lane_offsets:
        c = c.at[:, off:off + dim].set(jnp.concatenate([cos, cos], axis=-1))
        sa = sa.at[:, off:off + dim].set(jnp.concatenate([-sin, zero], axis=-1))
        sb = sb.at[:, off:off + dim].set(jnp.concatenate([zero, sin], axis=-1))
    return c, sa, sb


def _const_spec(shape):
    nd = len(shape)
    return pl.BlockSpec(shape, lambda *_: (0,) * nd, pipeline_mode=pl.Buffered(1))


def _attn_inproj(x, norm_w, w_in, q_norm, w_uq, kv_norm, w_ukv, ts):
    bsz, s_len, _ = x.shape
    cq, ckv, kr, dq, dk, dv = jnp.split(
        w_in, [int(v) for v in np.cumsum([MLA_Q_LORA, MLA_KV_LORA, MLA_ROPE, 512, 512])], axis=-1)
    kr_pad = jnp.pad(kr, ((0, 0), (MLA_NOPE, LANES - MLA_NOPE - MLA_ROPE)))
    win_p = jnp.concatenate([cq, ckv, dq, dk, kr_pad], axis=-1).astype(BF16)
    wdvt = dv.T.astype(BF16)
    scale = (MLA_NOPE + MLA_ROPE) ** -0.5 * LOG2E
    wuq_p = jnp.pad((w_uq * scale).reshape(MLA_Q_LORA, MLA_HEADS, MLA_NOPE + MLA_ROPE),
                    ((0, 0), (0, 0), (0, LANES - MLA_NOPE - MLA_ROPE)))
    wuq_p = wuq_p.reshape(MLA_Q_LORA, MLA_HEADS * LANES).astype(BF16)
    wkv = w_ukv.reshape(MLA_KV_LORA, MLA_HEADS, MLA_NOPE + MLA_V)
    wuk_p = jnp.pad(wkv[..., :MLA_NOPE], ((0, 0), (0, 0), (0, LANES - MLA_NOPE)))
    wuk_p = wuk_p.reshape(MLA_KV_LORA, MLA_HEADS * LANES).astype(BF16)
    wuvt = wkv[..., MLA_NOPE:].reshape(MLA_KV_LORA, MLA_HEADS * MLA_V).T.astype(BF16)
    tabs_m = _rope_tables(s_len, MLA_ROPE, (MLA_NOPE,))
    tabs_d = _rope_tables(s_len, DIFF_HD, (0, DIFF_HD))

    row = lambda b, i: (b, i, 0)
    tab = pl.BlockSpec((ts, LANES), lambda b, i: (i, 0))
    head_major = pl.BlockSpec((1, MLA_HEADS, ts, LANES), lambda b, i: (b, 0, i, 0))
    wide = pl.BlockSpec((1, ts, 512), row)
    transposed = pl.BlockSpec((1, 4, VT_ROWS, ts), lambda b, i: (b, 0, 0, i))
    return pl.pallas_call(
        _attn_inproj_kernel,
        grid=(bsz, s_len // ts),
        in_specs=[pl.BlockSpec((1, ts, D_MODEL), row), _const_spec((1, D_MODEL)),
                  _const_spec(win_p.shape), _const_spec(wdvt.shape), _const_spec((1, MLA_Q_LORA)),
                  _const_spec(wuq_p.shape), _const_spec((1, MLA_KV_LORA)), _const_spec(wuk_p.shape),
                  _const_spec(wuvt.shape), tab, tab, tab, tab, tab, tab],
        out_specs=[head_major, head_major, transposed, wide, wide, transposed],
        out_shape=[jax.ShapeDtypeStruct((bsz, MLA_HEADS, s_len, LANES), BF16),
                   jax.ShapeDtypeStruct((bsz, MLA_HEADS, s_len, LANES), BF16),
                   jax.ShapeDtypeStruct((bsz, 4, VT_ROWS, s_len), BF16),
                   jax.ShapeDtypeStruct((bsz, s_len, 512), BF16),
                   jax.ShapeDtypeStruct((bsz, s_len, 512), BF16),
                   jax.ShapeDtypeStruct((bsz, 4, VT_ROWS, s_len), BF16)],
        compiler_params=pltpu.CompilerParams(dimension_semantics=("parallel", "parallel"),
                                             vmem_limit_bytes=VMEM_LIMIT),
        name="attn_inproj",
    )(x, norm_w.reshape(1, -1), win_p, wdvt, q_norm.reshape(1, -1), wuq_p, kv_norm.reshape(1, -1), wuk_p, wuvt,
      *tabs_m, *tabs_d)


ATTN_TQ = 512
VT_ONES = 16
VT_ROWS = LANES + VT_ONES
LOG2E = math.log2(math.e)


def _attn_sweep(i, tq, fill_scores, vt_ref, sa_sc, sb_sc, m_sc, acc_sc):
    nq = sa_sc.shape[1]
    m_sc[...] = jnp.full(m_sc.shape, -jnp.inf, F32)
    acc_sc[...] = jnp.zeros(acc_sc.shape, F32)

    def start_of(blk):
        return pl.multiple_of(blk * tq, tq)

    def process(s, blk):
        m_prev = m_sc[...]
        m_new = jnp.maximum(m_prev, jnp.max(s, axis=0, keepdims=True))
        alpha = jnp.exp2(m_prev - m_new)
        p = jnp.exp2((s - m_new).astype(BF16))
        acc_sc[...] = alpha * acc_sc[...] + _dot(vt_ref[0, 0, :, pl.ds(start_of(blk), tq)], p)
        m_sc[...] = m_new

    def diagonal(s):
        key = lax.broadcasted_iota(jnp.int32, (tq, nq), 0)
        qry = lax.broadcasted_iota(jnp.int32, (tq, nq), 1) & (tq - 1)
        return jnp.where(key <= qry, s, -jnp.inf)

    fill_scores(sa_sc, start_of(0))

    def body(jj, carry):
        a = 2 * jj
        fill_scores(sb_sc, start_of(a + 1))
        process(sa_sc[...], a)
        fill_scores(sa_sc, start_of(a + 2))
        process(sb_sc[...], a + 1)
        return carry

    n_main = i // 2
    lax.fori_loop(0, n_main, body, 0)

    @pl.when(i % 2 == 0)
    def _():
        process(diagonal(sa_sc[...]), i)

    @pl.when(i % 2 == 1)
    def _():
        fill_scores(sb_sc, start_of(i))
        process(sa_sc[...], i - 1)
        process(diagonal(sb_sc[...]), i)

    return acc_sc[LANES:LANES + 1, :]


def _attn_scratch(tq):
    nq = 2 * tq
    return [pltpu.VMEM((tq, nq), F32), pltpu.VMEM((tq, nq), F32), pltpu.VMEM((1, nq), F32),
            pltpu.VMEM((VT_ROWS, nq), F32)]


def _mla_attn_kernel(q_ref, k_ref, vt_ref, o_ref, sa_sc, sb_sc, m_sc, acc_sc, *, tq):
    def fill_scores(buf, start):
        for h in range(2):
            buf[:, h * tq:(h + 1) * tq] = _dot_nt(k_ref[0, h, pl.ds(start, tq), :], q_ref[0, h])

    l = _attn_sweep(pl.program_id(2), tq, fill_scores, vt_ref, sa_sc, sb_sc, m_sc, acc_sc)
    o_t = jnp.concatenate([acc_sc[0:MLA_V, 0:tq] / l[:, 0:tq],
                           acc_sc[MLA_V:2 * MLA_V, tq:2 * tq] / l[:, tq:2 * tq]], axis=0)
    o_ref[0] = o_t.T.astype(BF16)


def _mla_attention(q, k, vt, tq):
    bsz, nh, s_len, _ = q.shape
    return pl.pallas_call(
        functools.partial(_mla_attn_kernel, tq=tq),
        grid=(bsz, nh // 2, s_len // tq),
        in_specs=[pl.BlockSpec((1, 2, tq, LANES), lambda b, p, i: (b, p, i, 0)),
                  pl.BlockSpec((1, 2, s_len, LANES), lambda b, p, i: (b, p, 0, 0)),
                  pl.BlockSpec((1, 1, VT_ROWS, s_len), lambda b, p, i: (b, p, 0, 0))],
        out_specs=pl.BlockSpec((1, tq, LANES), lambda b, p, i: (b, i, p)),
        out_shape=jax.ShapeDtypeStruct((bsz, s_len, nh * MLA_V), BF16),
        scratch_shapes=_attn_scratch(tq),
        compiler_params=pltpu.CompilerParams(dimension_semantics=("parallel", "parallel", "arbitrary"),
                                             vmem_limit_bytes=VMEM_LIMIT),
        name="mla_attention",
    )(q, k, vt)


def _diff_attn_kernel(q_ref, k_ref, vt_ref, lq1_ref, lk1_ref, lq2_ref, lk2_ref, sub_ref, o_ref,
                      qs_sc, sa_sc, sb_sc, m_sc, acc_sc, *, tq, lambda_init):
    q = q_ref[0]
    lane = lax.broadcasted_iota(jnp.int32, q.shape, 1)
    zero = jnp.zeros_like(q)
    qs_sc[0:tq] = jnp.where(lane < DIFF_HD, q, zero)
    qs_sc[tq:2 * tq] = jnp.where(lane < DIFF_HD, zero, q)

    def fill_scores(buf, start):
        buf[...] = _dot_nt(k_ref[0, pl.ds(start, tq), :], qs_sc[...])

    l = _attn_sweep(pl.program_id(2), tq, fill_scores, vt_ref, sa_sc, sb_sc, m_sc, acc_sc)
    o_t = acc_sc[0:LANES, :] / l
    lam = (jnp.exp(jnp.sum(lq1_ref[...] * lk1_ref[...], axis=-1, keepdims=True))
           - jnp.exp(jnp.sum(lq2_ref[...] * lk2_ref[...], axis=-1, keepdims=True)) + lambda_init)
    od = (o_t[:, 0:tq] - lam * o_t[:, tq:2 * tq]).T
    o_ref[0] = (_rms(od, sub_ref[...]) * (1.0 - lambda_init)).astype(BF16)


def _diff_attention(dq, dk, dvt, lq1, lk1, lq2, lk2, subln, lambda_init, tq):
    bsz, s_len, _ = dq.shape
    qspec = pl.BlockSpec((1, tq, LANES), lambda b, h, i: (b, i, h))
    vec = lambda n: pl.BlockSpec((1, n), lambda b, h, i: (0, 0))
    return pl.pallas_call(
        functools.partial(_diff_attn_kernel, tq=tq, lambda_init=lambda_init),
        grid=(bsz, DIFF_HEADS, s_len // tq),
        in_specs=[qspec, pl.BlockSpec((1, s_len, LANES), lambda b, h, i: (b, 0, h)),
                  pl.BlockSpec((1, 1, VT_ROWS, s_len), lambda b, h, i: (b, h, 0, 0)),
                  vec(DIFF_HD), vec(DIFF_HD), vec(DIFF_HD), vec(DIFF_HD), vec(DIFF_V)],
        out_specs=qspec,
        out_shape=jax.ShapeDtypeStruct((bsz, s_len, DIFF_HEADS * DIFF_V), BF16),
        scratch_shapes=[pltpu.VMEM((2 * tq, LANES), BF16)] + _attn_scratch(tq),
        compiler_params=pltpu.CompilerParams(dimension_semantics=("parallel", "parallel", "arbitrary"),
                                             vmem_limit_bytes=VMEM_LIMIT),
        name="diff_attention",
    )(dq, dk, dvt, lq1.reshape(1, -1), lk1.reshape(1, -1), lq2.reshape(1, -1), lk2.reshape(1, -1),
      subln.reshape(1, -1))


def _outproj_ffn_kernel(x_ref, ma_ref, mb_ref, woa_ref, wob_ref, nf_ref, wg_ref, wu_ref, wd_ref, fin_ref,
                        o_ref, *, final_norm):
    x1 = x_ref[...] + _dot(ma_ref[...], woa_ref[...]) + _dot(mb_ref[...], wob_ref[...])
    h = _rms(x1, nf_ref[...]).astype(BF16)
    g = _dot(h, wg_ref[...])
    u = _dot(h, wu_ref[...])
    a = (_silu(g) * u).astype(BF16)
    x2 = x1 + _dot(a, wd_ref[...])
    if final_norm:
        x2 = _rms(x2, fin_ref[...])
    o_ref[...] = x2


def _outproj_ffn(x2d, mix_a, mix_b, w_out, norm_ffn, w_gate, w_up, w_down, norm_final, final_norm, tm):
    n_tok = x2d.shape[0]
    half = mix_a.shape[1]
    d_ff = w_gate.shape[1]
    woa = w_out[:half].astype(BF16)
    wob = w_out[half:].astype(BF16)
    row = lambda i: (i, 0)
    return pl.pallas_call(
        functools.partial(_outproj_ffn_kernel, final_norm=final_norm),
        grid=(n_tok // tm,),
        in_specs=[pl.BlockSpec((tm, D_MODEL), row), pl.BlockSpec((tm, half), row), pl.BlockSpec((tm, half), row),
                  _const_spec(woa.shape), _const_spec(wob.shape), _const_spec((1, D_MODEL)),
                  _const_spec((D_MODEL, d_ff)), _const_spec((D_MODEL, d_ff)), _const_spec((d_ff, D_MODEL)),
                  _const_spec((1, D_MODEL))],
        out_specs=pl.BlockSpec((tm, D_MODEL), row),
        out_shape=jax.ShapeDtypeStruct((n_tok, D_MODEL), F32),
        compiler_params=pltpu.CompilerParams(dimension_semantics=("parallel",), vmem_limit_bytes=VMEM_LIMIT),
        name="outproj_ffn",
    )(x2d, mix_a, mix_b, woa, wob, norm_ffn.reshape(1, -1), w_gate.astype(BF16), w_up.astype(BF16),
      w_down.astype(BF16), norm_final.reshape(1, -1))


S_Z = 0
S_XBC = S_Z + SSM_INNER
S_HQ = S_XBC + SSM_CONV_DIM
S_HF = S_HQ + HG_KDIM_TOTAL
S_HI = S_HF + HG_KDIM_TOTAL
S_HG = S_HI + HG_WIDTH
S_DT = S_HG + HG_WIDTH
S_IN_PACKED = S_DT + LANES


def _rec_inproj_kernel(x_ref, nw_ref, win_ref, z_ref, xbc_ref, hq_ref, hf_ref, hi_ref, hg_ref, dt_ref):
    hn = _rms(x_ref[...], nw_ref[...]).astype(BF16)
    proj = _dot(hn, win_ref[...])
    z_ref[...] = proj[:, S_Z:S_XBC].astype(BF16)
    xbc_ref[...] = proj[:, S_XBC:S_HQ]
    hq_ref[...] = proj[:, S_HQ:S_HF].astype(BF16)
    hf_ref[...] = proj[:, S_HF:S_HI]
    hi_ref[...] = proj[:, S_HI:S_HG].astype(BF16)
    hg_ref[...] = proj[:, S_HG:S_DT].astype(BF16)
    dt_ref[...] = proj[:, S_DT:S_IN_PACKED]


def _rec_inproj(x2d, norm_w, w_in, tm):
    n_tok = x2d.shape[0]
    z, xbc, dt, hq, hf, hi, hg = jnp.split(
        w_in, [int(v) for v in np.cumsum([SSM_INNER, SSM_CONV_DIM, SSM_HEADS, 512, 512, 512])], axis=-1)
    dt_pad = jnp.pad(dt, ((0, 0), (0, LANES - SSM_HEADS)))
    win_p = jnp.concatenate([z, xbc, hq, hf, hi, hg, dt_pad], axis=-1).astype(BF16)
    row = lambda i: (i, 0)
    widths = [(SSM_INNER, BF16), (SSM_CONV_DIM, F32), (512, BF16), (512, F32), (512, BF16), (512, BF16),
              (LANES, F32)]
    return pl.pallas_call(
        _rec_inproj_kernel,
        grid=(n_tok // tm,),
        in_specs=[pl.BlockSpec((tm, D_MODEL), row), _const_spec((1, D_MODEL)), _const_spec(win_p.shape)],
        out_specs=[pl.BlockSpec((tm, w), row) for w, _ in widths],
        out_shape=[jax.ShapeDtypeStruct((n_tok, w), dt_) for w, dt_ in widths],
        compiler_params=pltpu.CompilerParams(dimension_semantics=("parallel",), vmem_limit_bytes=VMEM_LIMIT),
        name="rec_inproj",
    )(x2d, norm_w.reshape(1, -1), win_p)


CONV_PAD = 8


def _ssd_kernel(xbc_ref, dt_ref, z_ref, cw_ref, cb_ref, dtb_ref, ah_ref, dsk_ref, nw_ref, tri_ref, exp_ref,
                y_ref, xpad_sc, st_sc):
    L = SSM_CHUNK
    heads_per_group = SSM_HEADS // SSM_GROUPS
    gw = heads_per_group * SSM_HEADDIM

    @pl.when(pl.program_id(1) == 0)
    def _():
        xpad_sc[0:CONV_PAD] = jnp.zeros((CONV_PAD, SSM_CONV_DIM), F32)
        st_sc[...] = jnp.zeros(st_sc.shape, F32)

    xt = xbc_ref[0]
    xpad_sc[CONV_PAD:CONV_PAD + L] = xt
    conv = cb_ref[...] + cw_ref[SSM_CONV - 1:SSM_CONV] * xt
    for d in range(1, SSM_CONV):
        conv = conv + cw_ref[SSM_CONV - 1 - d:SSM_CONV - d] * xpad_sc[CONV_PAD - d:CONV_PAD - d + L]
    xpad_sc[0:CONV_PAD] = xt[L - CONV_PAD:L]
    xc = _silu(conv)
    xs = xc[:, 0:SSM_INNER]
    b_in = xc[:, SSM_INNER:SSM_INNER + SSM_GROUPS * SSM_STATE].astype(BF16)
    c_in = xc[:, SSM_INNER + SSM_GROUPS * SSM_STATE:].astype(BF16)

    dt = jax.nn.softplus(dt_ref[0] + dtb_ref[...])
    a = dt * ah_ref[...]
    a_cs = _dot01_left(tri_ref[...], a)
    a_cs_t = a_cs.T
    dt_e = _dot01_right(dt, exp_ref[...])
    acs_e = _dot01_right(a_cs, exp_ref[...])
    alast_e = acs_e[L - 1:L, :]
    xdt = xs * dt_e
    xdec = (xdt * jnp.exp(alast_e - acs_e)).astype(BF16)
    xdt_b = xdt.astype(BF16)
    eacs = jnp.exp(acs_e)
    r = lax.broadcasted_iota(jnp.int32, (L, L), 0)
    c = lax.broadcasted_iota(jnp.int32, (L, L), 1)
    causal = c <= r
    lane = lax.broadcasted_iota(jnp.int32, (L, LANES), 1)

    ys = []
    for g in range(SSM_GROUPS):
        bg = b_in[:, g * SSM_STATE:(g + 1) * SSM_STATE]
        cg = c_in[:, g * SSM_STATE:(g + 1) * SSM_STATE]
        cb = _dot_nt(cg, bg)
        st_prev = st_sc[g]
        y_off = _dot(cg, st_prev.astype(BF16)) * eacs[:, g * gw:(g + 1) * gw]
        st_sc[g] = st_prev * jnp.exp(alast_e[:, g * gw:(g + 1) * gw]) + _dot_tn(bg, xdec[:, g * gw:(g + 1) * gw])
        for pr in range(heads_per_group // 2):
            xpair = xdt_b[:, g * gw + pr * LANES:g * gw + (pr + 1) * LANES]
            res = []
            for hh in range(2):
                h = g * heads_per_group + 2 * pr + hh
                seg = jnp.exp(jnp.minimum(a_cs[:, h:h + 1] - a_cs_t[h:h + 1, :], 0.0))
                m = jnp.where(causal, cb * seg, 0.0).astype(BF16)
                res.append(_dot(m, xpair))
            ys.append(jnp.where(lane < SSM_HEADDIM, res[0], res[1]) + y_off[:, pr * LANES:(pr + 1) * LANES])
    y = jnp.concatenate(ys, axis=1) + dsk_ref[...] * xs
    y = y * _silu(z_ref[0].astype(F32))
    nw = nw_ref[...]
    for g in range(SSM_GROUPS):
        sl = slice(g * gw, (g + 1) * gw)
        y_ref[0, :, sl] = _rms(y[:, sl], nw[:, sl]).astype(BF16)


def _ssd(xbc, dt, z, conv_w, conv_b, dt_bias, a_log, d_skip, ssm_norm):
    bsz, s_len, _ = xbc.shape
    L = SSM_CHUNK
    pad = lambda v: jnp.pad(v.astype(F32), (0, LANES - SSM_HEADS)).reshape(1, LANES)
    a_head = -jnp.exp(a_log.astype(F32))
    tri = jnp.asarray(np.tril(np.ones((L, L), np.float32)), BF16)
    expand = np.zeros((LANES, SSM_INNER), np.float32)
    for h in range(SSM_HEADS):
        expand[h, h * SSM_HEADDIM:(h + 1) * SSM_HEADDIM] = 1.0
    expand = jnp.asarray(expand, BF16)
    dsk = jnp.repeat(d_skip.astype(F32), SSM_HEADDIM).reshape(1, SSM_INNER)
    chunk = lambda w: pl.BlockSpec((1, L, w), lambda b, c: (b, c, 0))
    const = lambda shape: pl.BlockSpec(shape, lambda b, c: (0,) * len(shape))
    return pl.pallas_call(
        _ssd_kernel,
        grid=(bsz, s_len // L),
        in_specs=[chunk(SSM_CONV_DIM), chunk(LANES), chunk(SSM_INNER), const((SSM_CONV, SSM_CONV_DIM)),
                  const((1, SSM_CONV_DIM)), const((1, LANES)), const((1, LANES)), const((1, SSM_INNER)),
                  const((1, SSM_INNER)), const((L, L)), const((LANES, SSM_INNER))],
        out_specs=chunk(SSM_INNER),
        out_shape=jax.ShapeDtypeStruct((bsz, s_len, SSM_INNER), BF16),
        scratch_shapes=[pltpu.VMEM((CONV_PAD + L, SSM_CONV_DIM), F32),
                        pltpu.VMEM((SSM_GROUPS, SSM_STATE, SSM_INNER // SSM_GROUPS), F32)],
        compiler_params=pltpu.CompilerParams(dimension_semantics=("parallel", "arbitrary"),
                                             vmem_limit_bytes=VMEM_LIMIT),
        name="ssd_scan",
    )(xbc, dt, z, conv_w, conv_b.reshape(1, -1), pad(dt_bias), pad(a_head), dsk, ssm_norm.reshape(1, -1),
      tri, expand)


HG_LEVELS = int(math.log2(HG_CHUNK))


def _hgrn_tables():
    C = HG_CHUNK
    idx = np.arange(C)
    mats = [np.tril(np.ones((C, C), np.float32))]
    masks = []
    for lev in range(HG_LEVELS):
        h = C >> (lev + 1)
        mid = (idx // (2 * h)) * (2 * h) + h - 1
        upper = (idx % (2 * h)) >= h
        j = idx[None, :]
        dq = (upper[:, None] & (j > mid[:, None]) & (j <= idx[:, None])).astype(np.float32)
        ek = ((~upper)[:, None] & (j > idx[:, None]) & (j <= mid[:, None])).astype(np.float32)
        mats += [dq, ek]
        same = (idx[:, None] // (2 * h)) == (idx[None, :] // (2 * h))
        masks.append((same & upper[:, None] & (~upper)[None, :]).astype(np.float32))
    masks.append(np.eye(C, dtype=np.float32))
    return np.concatenate(mats, axis=0), np.stack(masks, axis=0)


def _hgrn_kernel(hq_ref, hf_ref, hi_ref, hg_ref, lb_ref, gn_ref, wall_ref, masks_ref, o_ref, st_sc):
    C = HG_CHUNK

    @pl.when(pl.program_id(1) == 0)
    def _():
        st_sc[...] = jnp.zeros(st_sc.shape, F32)

    lb = lb_ref[...]
    xf = hf_ref[0]
    g = jnp.log(lb + (1.0 - lb) * jax.nn.sigmoid(xf))
    kin = (1.0 - lb) * jax.nn.sigmoid(-xf)
    q = _silu(hq_ref[0].astype(F32))
    sums = _dot01_left(wall_ref[...], g)
    gcum = sums[0:C]
    glast = gcum[C - 1:C]
    q_in = (q * jnp.exp(gcum)).astype(BF16)
    k_out = (kin * jnp.exp(glast - gcum)).astype(BF16)
    v = hi_ref[0]
    gate = _silu(hg_ref[0].astype(F32))
    for h in range(HG_HEADS):
        sl = slice(h * HG_EXPAND, (h + 1) * HG_EXPAND)
        qh, kh = q[:, sl], kin[:, sl]
        scores = masks_ref[HG_LEVELS] * _dot_nt(qh.astype(BF16), kh.astype(BF16))
        for lev in range(HG_LEVELS):
            dq = sums[C * (1 + 2 * lev):C * (2 + 2 * lev), sl]
            ek = sums[C * (2 + 2 * lev):C * (3 + 2 * lev), sl]
            qt = (qh * jnp.exp(dq)).astype(BF16)
            kt = (kh * jnp.exp(ek)).astype(BF16)
            scores = scores + masks_ref[lev] * _dot_nt(qt, kt)
        st = st_sc[h]
        vh = v[:, sl]
        o = _dot(scores.astype(BF16), vh) + _dot_nt(q_in[:, sl], st.astype(BF16))
        st_sc[h] = st * jnp.exp(glast[:, sl]) + _dot_tn(vh, k_out[:, sl])
        o_ref[0, :, sl] = (_rms(o, gn_ref[...]) * gate[:, sl]).astype(BF16)


def _hgrn2(hq, hf, hi, hg, lb, g_norm):
    bsz, s_len, _ = hq.shape
    C = HG_CHUNK
    wall, masks = _hgrn_tables()
    wall = jnp.asarray(wall, BF16)
    masks = jnp.asarray(masks, F32)
    chunk = pl.BlockSpec((1, C, HG_WIDTH), lambda b, c: (b, c, 0))
    const = lambda shape: pl.BlockSpec(shape, lambda b, c: (0,) * len(shape))
    return pl.pallas_call(
        _hgrn_kernel,
        grid=(bsz, s_len // C),
        in_specs=[chunk, chunk, chunk, chunk, const((1, HG_KDIM_TOTAL)), const((1, HG_VDIM)),
                  const(wall.shape), const(masks.shape)],
        out_specs=chunk,
        out_shape=jax.ShapeDtypeStruct((bsz, s_len, HG_WIDTH), BF16),
        scratch_shapes=[pltpu.VMEM((HG_HEADS, HG_VDIM, HG_EXPAND), F32)],
        compiler_params=pltpu.CompilerParams(dimension_semantics=("parallel", "arbitrary"),
                                             vmem_limit_bytes=VMEM_LIMIT),
        name="hgrn2_scan",
    )(hq, hf, hi, hg, lb.reshape(1, -1), g_norm.reshape(1, -1), wall, masks)


def kernel(x, norm_mix, norm_ffn, norm_final, a_w_in, a_q_norm, a_w_uq, a_kv_norm, a_w_ukv, a_lq1, a_lk1, a_lq2, a_lk2, a_subln, a_w_out, s_w_in, s_conv_w, s_conv_b, s_dt_bias, s_a_log, s_d, s_norm, h_g_norm, h_lower_bound, s_w_out, ffn_gate, ffn_up, ffn_down):
    bsz, s_len, _ = x.shape
    n_tok = bsz * s_len
    depth = norm_mix.shape[0]
    assert depth == 2 and s_len % ATTN_TQ == 0
    p_lb = jax.nn.softmax(h_lower_bound.astype(F32), axis=0)
    lb_all = jnp.cumsum(p_lb, axis=0) - p_lb[0:1]

    lambda_init = 0.8 - 0.6 * math.exp(-0.3 * 0)
    q, k, vt, dq, dk, dvt = _attn_inproj(x, norm_mix[0], a_w_in[0], a_q_norm[0], a_w_uq[0], a_kv_norm[0],
                                       a_w_ukv[0], ts=256)
    o_mla = _mla_attention(q, k, vt, tq=ATTN_TQ)
    o_diff = _diff_attention(dq, dk, dvt, a_lq1[0], a_lk1[0], a_lq2[0], a_lk2[0], a_subln[0], lambda_init,
                             tq=ATTN_TQ)
    x2d = _outproj_ffn(x.reshape(n_tok, D_MODEL), o_mla.reshape(n_tok, -1), o_diff.reshape(n_tok, -1),
                       a_w_out[0], norm_ffn[0], ffn_gate[0], ffn_up[0], ffn_down[0], norm_final,
                       final_norm=False, tm=256)

    z, xbc, hq, hf, hi, hg, dt = _rec_inproj(x2d, norm_mix[1], s_w_in[0], tm=256)
    seq = lambda t: t.reshape(bsz, s_len, t.shape[-1])
    y = _ssd(seq(xbc), seq(dt), seq(z), s_conv_w[0], s_conv_b[0], s_dt_bias[0], s_a_log[0], s_d[0], s_norm[0])
    o = _hgrn2(seq(hq), seq(hf), seq(hi), seq(hg), lb_all[1], h_g_norm[0])
    x2d = _outproj_ffn(x2d, y.reshape(n_tok, -1), o.reshape(n_tok, -1), s_w_out[0], norm_ffn[1], ffn_gate[1],
                       ffn_up[1], ffn_down[1], norm_final, final_norm=True, tm=256)
    return x2d.reshape(bsz, s_len, D_MODEL)
```

```python
import functools
import math

import numpy as np
import jax
import jax.numpy as jnp
from jax import lax
from jax.experimental import pallas as pl
from jax.experimental.pallas import tpu as pltpu

F32 = jnp.float32
BF16 = jnp.bfloat16

D_MODEL = 1024
EPS = 1e-6
ROPE_THETA = 10000.0

MLA_HEADS = 8
MLA_Q_LORA = 384
MLA_KV_LORA = 256
MLA_NOPE = 64
MLA_ROPE = 32
MLA_V = 64
DIFF_HEADS = 4
DIFF_HD = 64
DIFF_V = 2 * DIFF_HD

SSM_HEADS = 8
SSM_HEADDIM = 64
SSM_INNER = SSM_HEADS * SSM_HEADDIM
SSM_GROUPS = 2
SSM_STATE = 128
SSM_CONV = 4
SSM_CHUNK = 128
SSM_CONV_DIM = SSM_INNER + 2 * SSM_GROUPS * SSM_STATE
HG_HEADS = 4
HG_EXPAND = 128
HG_VDIM = 128
HG_KDIM_TOTAL = HG_HEADS * HG_EXPAND
HG_WIDTH = HG_HEADS * HG_VDIM
HG_CHUNK = 64

LANES = 128
VMEM_LIMIT = 52 * 1024 * 1024

NT_DIMS = (((1,), (1,)), ((), ()))
TN_DIMS = (((0,), (0,)), ((), ()))


def _dot(a, b):
    return jnp.dot(a, b, preferred_element_type=F32)


def _dot_nt(a, b):
    return lax.dot_general(a, b, NT_DIMS, preferred_element_type=F32)


def _dot_tn(a, b):
    return lax.dot_general(a, b, TN_DIMS, preferred_element_type=F32)


def _rms(x, w):
    return x * lax.rsqrt(jnp.mean(x * x, axis=-1, keepdims=True) + EPS) * w


def _silu(x):
    return x * jax.nn.sigmoid(x)


def _split3(a):
    hi = a.astype(BF16)
    r = a - hi.astype(F32)
    mid = r.astype(BF16)
    lo = (r - mid.astype(F32)).astype(BF16)
    return hi, mid, lo


def _dot01_left(m01, a):
    hi, mid, lo = _split3(a)
    return (_dot(m01, hi) + _dot(m01, mid)) + _dot(m01, lo)


def _dot01_right(a, m01):
    hi, mid, lo = _split3(a)
    return (_dot(hi, m01) + _dot(mid, m01)) + _dot(lo, m01)


def _rope(t, cos, sin_a, sin_b, half):
    return (t * cos + pltpu.roll(t, LANES - half, 1) * sin_a + pltpu.roll(t, half, 1) * sin_b)


A_CQ = 0
A_CKV = A_CQ + MLA_Q_LORA
A_DQ = A_CKV + MLA_KV_LORA
A_DK = A_DQ + DIFF_HEADS * 2 * DIFF_HD
A_KR = A_DK + DIFF_HEADS * 2 * DIFF_HD
A_IN_PACKED = A_KR + LANES


def _attn_inproj_kernel(x_ref, nw_ref, win_ref, wdvt_ref, qn_ref, wuq_ref, kvn_ref, wuk_ref, wuvt_ref,
                        cm_ref, sam_ref, sbm_ref, cd_ref, sad_ref, sbd_ref,
                        q_ref, k_ref, vt_ref, dq_ref, dk_ref, dvt_ref):
    hn = _rms(x_ref[0], nw_ref[...]).astype(BF16)
    proj = _dot(hn, win_ref[...])
    cq = _rms(proj[:, A_CQ:A_CKV], qn_ref[...]).astype(BF16)
    ckv = _rms(proj[:, A_CKV:A_DQ], kvn_ref[...]).astype(BF16)
    q = _dot(cq, wuq_ref[...])
    kn = _dot(ckv, wuk_ref[...])
    vt = _dot_nt(wuvt_ref[...], ckv).astype(BF16)
    dvt = _dot_nt(wdvt_ref[...], hn).astype(BF16)
    ones = jnp.ones((VT_ONES, vt.shape[1]), BF16)
    for g in range(4):
        vt_ref[0, g, 0:LANES] = vt[g * LANES:(g + 1) * LANES]
        vt_ref[0, g, LANES:VT_ROWS] = ones
        dvt_ref[0, g, 0:LANES] = dvt[g * LANES:(g + 1) * LANES]
        dvt_ref[0, g, LANES:VT_ROWS] = ones
    cm, sam, sbm = cm_ref[...], sam_ref[...], sbm_ref[...]
    kpe = _rope(proj[:, A_KR:A_KR + LANES], cm, sam, sbm, MLA_ROPE // 2)
    for h in range(MLA_HEADS):
        sl = slice(h * LANES, (h + 1) * LANES)
        q_ref[0, h] = _rope(q[:, sl], cm, sam, sbm, MLA_ROPE // 2).astype(BF16)
        k_ref[0, h] = (kn[:, sl] + kpe).astype(BF16)
    cd, sad, sbd = cd_ref[...], sad_ref[...], sbd_ref[...]
    for g in range(DIFF_HEADS):
        sl = slice(g * LANES, (g + 1) * LANES)
        dq = proj[:, A_DQ + g * LANES:A_DQ + (g + 1) * LANES]
        dk = proj[:, A_DK + g * LANES:A_DK + (g + 1) * LANES]
        dq_ref[0, :, sl] = (_rope(dq, cd, sad, sbd, DIFF_HD // 2) * (DIFF_HD ** -0.5 * LOG2E)).astype(BF16)
        dk_ref[0, :, sl] = _rope(dk, cd, sad, sbd, DIFF_HD // 2).astype(BF16)


def _rope_tables(seq_len, dim, lane_offsets):
    half = dim // 2
    inv_freq = 1.0 / (ROPE_THETA ** (jnp.arange(0, dim, 2, dtype=F32) / dim))
    ang = jnp.arange(seq_len, dtype=F32)[:, None] * inv_freq[None, :]
    cos, sin = jnp.cos(ang), jnp.sin(ang)
    zero = jnp.zeros_like(sin)
    c, sa, sb = [], [], []
    lane = 0
    for off in lane_offsets:
        gap = off - lane
        c += [jnp.ones((seq_len, gap), F32), cos, cos]
        sa += [jnp.zeros((seq_len, gap), F32), -sin, zero]
        sb += [jnp.zeros((seq_len, gap), F32), zero, sin]
        lane = off + dim
    c.append(jnp.ones((seq_len, LANES - lane), F32))
    sa.append(jnp.zeros((seq_len, LANES - lane), F32))
    sb.append(jnp.zeros((seq_len, LANES - lane), F32))
    return tuple(jnp.concatenate(t, axis=-1) for t in (c, sa, sb))


def _const_spec(shape):
    nd = len(shape)
    return pl.BlockSpec(shape, lambda *_: (0,) * nd, pipeline_mode=pl.Buffered(1))


def _attn_inproj(x, norm_w, w_in, q_norm, w_uq, kv_norm, w_ukv, ts):
    bsz, s_len, _ = x.shape
    cq, ckv, kr, dq, dk, dv = jnp.split(
        w_in, [int(v) for v in np.cumsum([MLA_Q_LORA, MLA_KV_LORA, MLA_ROPE, 512, 512])], axis=-1)
    kr_pad = jnp.pad(kr, ((0, 0), (MLA_NOPE, LANES - MLA_NOPE - MLA_ROPE)))
    win_p = jnp.concatenate([cq, ckv, dq, dk, kr_pad], axis=-1).astype(BF16)
    wdvt = dv.T.astype(BF16)
    scale = (MLA_NOPE + MLA_ROPE) ** -0.5 * LOG2E
    wuq_p = jnp.pad((w_uq * scale).reshape(MLA_Q_LORA, MLA_HEADS, MLA_NOPE + MLA_ROPE),
                    ((0, 0), (0, 0), (0, LANES - MLA_NOPE - MLA_ROPE)))
    wuq_p = wuq_p.reshape(MLA_Q_LORA, MLA_HEADS * LANES).astype(BF16)
    wkv = w_ukv.reshape(MLA_KV_LORA, MLA_HEADS, MLA_NOPE + MLA_V)
    wuk_p = jnp.pad(wkv[..., :MLA_NOPE], ((0, 0), (0, 0), (0, LANES - MLA_NOPE)))
    wuk_p = wuk_p.reshape(MLA_KV_LORA, MLA_HEADS * LANES).astype(BF16)
    wuvt = wkv[..., MLA_NOPE:].reshape(MLA_KV_LORA, MLA_HEADS * MLA_V).T.astype(BF16)
    tabs_m = _rope_tables(s_len, MLA_ROPE, (MLA_NOPE,))
    tabs_d = _rope_tables(s_len, DIFF_HD, (0, DIFF_HD))

    row = lambda b, i: (b, i, 0)
    tab = pl.BlockSpec((ts, LANES), lambda b, i: (i, 0))
    head_major = pl.BlockSpec((1, MLA_HEADS, ts, LANES), lambda b, i: (b, 0, i, 0))
    wide = pl.BlockSpec((1, ts, 512), row)
    transposed = pl.BlockSpec((1, 4, VT_ROWS, ts), lambda b, i: (b, 0, 0, i))
    return pl.pallas_call(
        _attn_inproj_kernel,
        grid=(bsz, s_len // ts),
        in_specs=[pl.BlockSpec((1, ts, D_MODEL), row), _const_spec((1, D_MODEL)),
                  _const_spec(win_p.shape), _const_spec(wdvt.shape), _const_spec((1, MLA_Q_LORA)),
                  _const_spec(wuq_p.shape), _const_spec((1, MLA_KV_LORA)), _const_spec(wuk_p.shape),
                  _const_spec(wuvt.shape), tab, tab, tab, tab, tab, tab],
        out_specs=[head_major, head_major, transposed, wide, wide, transposed],
        out_shape=[jax.ShapeDtypeStruct((bsz, MLA_HEADS, s_len, LANES), BF16),
                   jax.ShapeDtypeStruct((bsz, MLA_HEADS, s_len, LANES), BF16),
                   jax.ShapeDtypeStruct((bsz, 4, VT_ROWS, s_len), BF16),
                   jax.ShapeDtypeStruct((bsz, s_len, 512), BF16),
                   jax.ShapeDtypeStruct((bsz, s_len, 512), BF16),
                   jax.ShapeDtypeStruct((bsz, 4, VT_ROWS, s_len), BF16)],
        compiler_params=pltpu.CompilerParams(dimension_semantics=("parallel", "parallel"),
                                             vmem_limit_bytes=VMEM_LIMIT),
        name="attn_inproj",
    )(x, norm_w.reshape(1, -1), win_p, wdvt, q_norm.reshape(1, -1), wuq_p, kv_norm.reshape(1, -1), wuk_p, wuvt,
      *tabs_m, *tabs_d)


ATTN_TQ = 512
ATTN_GROUP = 256
VT_ONES = 16
VT_ROWS = LANES + VT_ONES
LOG2E = math.log2(math.e)


def _attn_sweep(i, tq, fill_scores, vt_ref, sa_sc, sb_sc, m_sc, acc_sc):
    nq = sa_sc.shape[1]
    groups = [slice(c, c + ATTN_GROUP) for c in range(0, nq, ATTN_GROUP)]
    m_sc[...] = jnp.full(m_sc.shape, -jnp.inf, F32)
    acc_sc[...] = jnp.zeros(acc_sc.shape, F32)

    def start_of(blk):
        return pl.multiple_of(blk * tq, tq)

    def process(s_ref, blk, cols, diagonal=False):
        s = s_ref[:, cols]
        if diagonal:
            key = lax.broadcasted_iota(jnp.int32, s.shape, 0)
            qry = (lax.broadcasted_iota(jnp.int32, s.shape, 1) + cols.start) & (tq - 1)
            s = jnp.where(key <= qry, s, -jnp.inf)
        m_prev = m_sc[:, cols]
        m_new = jnp.maximum(m_prev, jnp.max(s, axis=0, keepdims=True))
        alpha = jnp.exp2(m_prev - m_new)
        p = jnp.exp2((s - m_new).astype(BF16))
        acc_sc[:, cols] = alpha * acc_sc[:, cols] + _dot(vt_ref[0, 0, :, pl.ds(start_of(blk), tq)], p)
        m_sc[:, cols] = m_new

    for cols in groups:
        fill_scores(sa_sc, start_of(0), cols)

    def body(jj, carry):
        a = 2 * jj
        for cols in groups:
            fill_scores(sb_sc, start_of(a + 1), cols)
            process(sa_sc, a, cols)
        for cols in groups:
            fill_scores(sa_sc, start_of(a + 2), cols)
            process(sb_sc, a + 1, cols)
        return carry

    n_main = i // 2
    lax.fori_loop(0, n_main, body, 0)

    @pl.when(i % 2 == 0)
    def _():
        for cols in groups:
            process(sa_sc, i, cols, diagonal=True)

    @pl.when(i % 2 == 1)
    def _():
        for cols in groups:
            fill_scores(sb_sc, start_of(i), cols)
            process(sa_sc, i - 1, cols)
        for cols in groups:
            process(sb_sc, i, cols, diagonal=True)

    return acc_sc[LANES:LANES + 1, :]


def _attn_scratch(tq):
    nq = 2 * tq
    return [pltpu.VMEM((tq, nq), F32), pltpu.VMEM((tq, nq), F32), pltpu.VMEM((1, nq), F32),
            pltpu.VMEM((VT_ROWS, nq), F32)]


def _mla_attn_kernel(q_ref, k_ref, vt_ref, o_ref, sa_sc, sb_sc, m_sc, acc_sc, *, tq):
    def fill_scores(buf, start, cols):
        h, r = divmod(cols.start, tq)
        buf[:, cols] = _dot_nt(k_ref[0, h, pl.ds(start, tq), :], q_ref[0, h, r:r + ATTN_GROUP])

    l = _attn_sweep(pl.program_id(2), tq, fill_scores, vt_ref, sa_sc, sb_sc, m_sc, acc_sc)
    o_t = jnp.concatenate([acc_sc[0:MLA_V, 0:tq] / l[:, 0:tq],
                           acc_sc[MLA_V:2 * MLA_V, tq:2 * tq] / l[:, tq:2 * tq]], axis=0)
    o_ref[0] = o_t.T.astype(BF16)


def _mla_attention(q, k, vt, tq):
    bsz, nh, s_len, _ = q.shape
    return pl.pallas_call(
        functools.partial(_mla_attn_kernel, tq=tq),
        grid=(bsz, nh // 2, s_len // tq),
        in_specs=[pl.BlockSpec((1, 2, tq, LANES), lambda b, p, i: (b, p, i, 0)),
                  pl.BlockSpec((1, 2, s_len, LANES), lambda b, p, i: (b, p, 0, 0)),
                  pl.BlockSpec((1, 1, VT_ROWS, s_len), lambda b, p, i: (b, p, 0, 0))],
        out_specs=pl.BlockSpec((1, tq, LANES), lambda b, p, i: (b, i, p)),
        out_shape=jax.ShapeDtypeStruct((bsz, s_len, nh * MLA_V), BF16),
        scratch_shapes=_attn_scratch(tq),
        compiler_params=pltpu.CompilerParams(dimension_semantics=("parallel", "parallel", "arbitrary"),
                                             vmem_limit_bytes=VMEM_LIMIT),
        name="mla_attention",
    )(q, k, vt)


def _diff_attn_kernel(q_ref, k_ref, vt_ref, lq1_ref, lk1_ref, lq2_ref, lk2_ref, sub_ref, o_ref,
                      qs_sc, sa_sc, sb_sc, m_sc, acc_sc, *, tq, lambda_init):
    q = q_ref[0]
    lane = lax.broadcasted_iota(jnp.int32, q.shape, 1)
    zero = jnp.zeros_like(q)
    qs_sc[0:tq] = jnp.where(lane < DIFF_HD, q, zero)
    qs_sc[tq:2 * tq] = jnp.where(lane < DIFF_HD, zero, q)

    def fill_scores(buf, start, cols):
        buf[:, cols] = _dot_nt(k_ref[0, pl.ds(start, tq), :], qs_sc[cols])

    l = _attn_sweep(pl.program_id(2), tq, fill_scores, vt_ref, sa_sc, sb_sc, m_sc, acc_sc)
    o_t = acc_sc[0:LANES, :] / l
    lam = (jnp.exp(jnp.sum(lq1_ref[...] * lk1_ref[...], axis=-1, keepdims=True))
           - jnp.exp(jnp.sum(lq2_ref[...] * lk2_ref[...], axis=-1, keepdims=True)) + lambda_init)
    od = (o_t[:, 0:tq] - lam * o_t[:, tq:2 * tq]).T
    o_ref[0] = (_rms(od, sub_ref[...]) * (1.0 - lambda_init)).astype(BF16)


def _diff_attention(dq, dk, dvt, lq1, lk1, lq2, lk2, subln, lambda_init, tq):
    bsz, s_len, _ = dq.shape
    qspec = pl.BlockSpec((1, tq, LANES), lambda b, h, i: (b, i, h))
    vec = lambda n: pl.BlockSpec((1, n), lambda b, h, i: (0, 0))
    return pl.pallas_call(
        functools.partial(_diff_attn_kernel, tq=tq, lambda_init=lambda_init),
        grid=(bsz, DIFF_HEADS, s_len // tq),
        in_specs=[qspec, pl.BlockSpec((1, s_len, LANES), lambda b, h, i: (b, 0, h)),
                  pl.BlockSpec((1, 1, VT_ROWS, s_len), lambda b, h, i: (b, h, 0, 0)),
                  vec(DIFF_HD), vec(DIFF_HD), vec(DIFF_HD), vec(DIFF_HD), vec(DIFF_V)],
        out_specs=qspec,
        out_shape=jax.ShapeDtypeStruct((bsz, s_len, DIFF_HEADS * DIFF_V), BF16),
        scratch_shapes=[pltpu.VMEM((2 * tq, LANES), BF16)] + _attn_scratch(tq),
        compiler_params=pltpu.CompilerParams(dimension_semantics=("parallel", "parallel", "arbitrary"),
                                             vmem_limit_bytes=VMEM_LIMIT),
        name="diff_attention",
    )(dq, dk, dvt, lq1.reshape(1, -1), lk1.reshape(1, -1), lq2.reshape(1, -1), lk2.reshape(1, -1),
      subln.reshape(1, -1))


def _outproj_ffn_kernel(x_ref, ma_ref, mb_ref, woa_ref, wob_ref, nf_ref, wg_ref, wu_ref, wd_ref, fin_ref,
                        o_ref, *, final_norm):
    x1 = x_ref[...] + _dot(ma_ref[...], woa_ref[...]) + _dot(mb_ref[...], wob_ref[...])
    h = _rms(x1, nf_ref[...]).astype(BF16)
    g = _dot(h, wg_ref[...])
    u = _dot(h, wu_ref[...])
    a = (_silu(g) * u).astype(BF16)
    x2 = x1 + _dot(a, wd_ref[...])
    if final_norm:
        x2 = _rms(x2, fin_ref[...])
    o_ref[...] = x2


def _outproj_ffn(x2d, mix_a, mix_b, w_out, norm_ffn, w_gate, w_up, w_down, norm_final, final_norm, tm):
    n_tok = x2d.shape[0]
    half = mix_a.shape[1]
    d_ff = w_gate.shape[1]
    woa = w_out[:half].astype(BF16)
    wob = w_out[half:].astype(BF16)
    row = lambda i: (i, 0)
    return pl.pallas_call(
        functools.partial(_outproj_ffn_kernel, final_norm=final_norm),
        grid=(n_tok // tm,),
        in_specs=[pl.BlockSpec((tm, D_MODEL), row), pl.BlockSpec((tm, half), row), pl.BlockSpec((tm, half), row),
                  _const_spec(woa.shape), _const_spec(wob.shape), _const_spec((1, D_MODEL)),
                  _const_spec((D_MODEL, d_ff)), _const_spec((D_MODEL, d_ff)), _const_spec((d_ff, D_MODEL)),
                  _const_spec((1, D_MODEL))],
        out_specs=pl.BlockSpec((tm, D_MODEL), row),
        out_shape=jax.ShapeDtypeStruct((n_tok, D_MODEL), F32),
        compiler_params=pltpu.CompilerParams(dimension_semantics=("parallel",), vmem_limit_bytes=VMEM_LIMIT),
        name="outproj_ffn",
    )(x2d, mix_a, mix_b, woa, wob, norm_ffn.reshape(1, -1), w_gate.astype(BF16), w_up.astype(BF16),
      w_down.astype(BF16), norm_final.reshape(1, -1))


S_Z = 0
S_XBC = S_Z + SSM_INNER
S_HQ = S_XBC + SSM_CONV_DIM
S_HF = S_HQ + HG_KDIM_TOTAL
S_HI = S_HF + HG_KDIM_TOTAL
S_HG = S_HI + HG_WIDTH
S_DT = S_HG + HG_WIDTH
S_IN_PACKED = S_DT + LANES


def _rec_inproj_kernel(x_ref, nw_ref, win_ref, z_ref, xbc_ref, hq_ref, hf_ref, hi_ref, hg_ref, dt_ref):
    hn = _rms(x_ref[...], nw_ref[...]).astype(BF16)
    proj = _dot(hn, win_ref[...])
    z_ref[...] = proj[:, S_Z:S_XBC].astype(BF16)
    xbc_ref[...] = proj[:, S_XBC:S_HQ]
    hq_ref[...] = proj[:, S_HQ:S_HF].astype(BF16)
    hf_ref[...] = proj[:, S_HF:S_HI]
    hi_ref[...] = proj[:, S_HI:S_HG].astype(BF16)
    hg_ref[...] = proj[:, S_HG:S_DT].astype(BF16)
    dt_ref[...] = proj[:, S_DT:S_IN_PACKED]


def _rec_inproj(x2d, norm_w, w_in, tm):
    n_tok = x2d.shape[0]
    z, xbc, dt, hq, hf, hi, hg = jnp.split(
        w_in, [int(v) for v in np.cumsum([SSM_INNER, SSM_CONV_DIM, SSM_HEADS, 512, 512, 512])], axis=-1)
    dt_pad = jnp.pad(dt, ((0, 0), (0, LANES - SSM_HEADS)))
    win_p = jnp.concatenate([z, xbc, hq, hf, hi, hg, dt_pad], axis=-1).astype(BF16)
    row = lambda i: (i, 0)
    widths = [(SSM_INNER, BF16), (SSM_CONV_DIM, F32), (512, BF16), (512, F32), (512, BF16), (512, BF16),
              (LANES, F32)]
    return pl.pallas_call(
        _rec_inproj_kernel,
        grid=(n_tok // tm,),
        in_specs=[pl.BlockSpec((tm, D_MODEL), row), _const_spec((1, D_MODEL)), _const_spec(win_p.shape)],
        out_specs=[pl.BlockSpec((tm, w), row) for w, _ in widths],
        out_shape=[jax.ShapeDtypeStruct((n_tok, w), dt_) for w, dt_ in widths],
        compiler_params=pltpu.CompilerParams(dimension_semantics=("parallel",), vmem_limit_bytes=VMEM_LIMIT),
        name="rec_inproj",
    )(x2d, norm_w.reshape(1, -1), win_p)


CONV_PAD = 8


def _ssd_kernel(xbc_ref, dt_ref, z_ref, cw_ref, cb_ref, dtb_ref, ah_ref, dsk_ref, nw_ref, tri_ref, exp_ref,
                y_ref, xpad_sc, st_sc):
    L = SSM_CHUNK
    heads_per_group = SSM_HEADS // SSM_GROUPS
    gw = heads_per_group * SSM_HEADDIM

    @pl.when(pl.program_id(0) == 0)
    def _():
        xpad_sc[:, 0:CONV_PAD] = jnp.zeros((xpad_sc.shape[0], CONV_PAD, SSM_CONV_DIM), F32)
        st_sc[...] = jnp.zeros(st_sc.shape, F32)

    r = lax.broadcasted_iota(jnp.int32, (L, L), 0)
    c = lax.broadcasted_iota(jnp.int32, (L, L), 1)
    causal = c <= r
    lane = lax.broadcasted_iota(jnp.int32, (L, LANES), 1)
    nw = nw_ref[...]

    for b in range(xbc_ref.shape[0]):
        xt = xbc_ref[b]
        xpad_sc[b, CONV_PAD:CONV_PAD + L] = xt
        conv = cb_ref[...] + cw_ref[SSM_CONV - 1:SSM_CONV] * xt
        for d in range(1, SSM_CONV):
            conv = conv + cw_ref[SSM_CONV - 1 - d:SSM_CONV - d] * xpad_sc[b, CONV_PAD - d:CONV_PAD - d + L]
        xpad_sc[b, 0:CONV_PAD] = xt[L - CONV_PAD:L]
        xc = _silu(conv)
        xs = xc[:, 0:SSM_INNER]
        b_in = xc[:, SSM_INNER:SSM_INNER + SSM_GROUPS * SSM_STATE].astype(BF16)
        c_in = xc[:, SSM_INNER + SSM_GROUPS * SSM_STATE:].astype(BF16)

        dt = jax.nn.softplus(dt_ref[b] + dtb_ref[...])
        a = dt * ah_ref[...]
        a_cs = _dot01_left(tri_ref[...], a)
        a_cs_t = a_cs.T
        dt_e = _dot01_right(dt, exp_ref[...])
        acs_e = _dot01_right(a_cs, exp_ref[...])
        alast_e = acs_e[L - 1:L, :]
        xdt = xs * dt_e
        xdec = (xdt * jnp.exp(alast_e - acs_e)).astype(BF16)
        xdt_b = xdt.astype(BF16)
        eacs = jnp.exp(acs_e)

        ys = []
        for g in range(SSM_GROUPS):
            gs = slice(g * gw, (g + 1) * gw)
            bg = b_in[:, g * SSM_STATE:(g + 1) * SSM_STATE]
            cg = c_in[:, g * SSM_STATE:(g + 1) * SSM_STATE]
            cb = _dot_nt(cg, bg)
            st_prev = st_sc[b * SSM_GROUPS + g]
            y_off = _dot(cg, st_prev.astype(BF16)) * eacs[:, gs]
            st_sc[b * SSM_GROUPS + g] = st_prev * jnp.exp(alast_e[:, gs]) + _dot_tn(bg, xdec[:, gs])
            for pr in range(heads_per_group // 2):
                xpair = xdt_b[:, g * gw + pr * LANES:g * gw + (pr + 1) * LANES]
                res = []
                for hh in range(2):
                    h = g * heads_per_group + 2 * pr + hh
                    seg = jnp.exp(jnp.minimum(a_cs[:, h:h + 1] - a_cs_t[h:h + 1, :], 0.0))
                    m = jnp.where(causal, cb * seg, 0.0).astype(BF16)
                    res.append(_dot(m, xpair))
                ys.append(jnp.where(lane < SSM_HEADDIM, res[0], res[1]) + y_off[:, pr * LANES:(pr + 1) * LANES])
        y = jnp.concatenate(ys, axis=1) + dsk_ref[...] * xs
        y = y * _silu(z_ref[b].astype(F32))
        for g in range(SSM_GROUPS):
            sl = slice(g * gw, (g + 1) * gw)
            y_ref[b, :, sl] = _rms(y[:, sl], nw[:, sl]).astype(BF16)


def _ssd(xbc, dt, z, conv_w, conv_b, dt_bias, a_log, d_skip, ssm_norm):
    bsz, s_len, _ = xbc.shape
    L = SSM_CHUNK
    pad = lambda v: jnp.pad(v.astype(F32), (0, LANES - SSM_HEADS)).reshape(1, LANES)
    a_head = -jnp.exp(a_log.astype(F32))
    tri = jnp.asarray(np.tril(np.ones((L, L), np.float32)), BF16)
    expand = np.zeros((LANES, SSM_INNER), np.float32)
    for h in range(SSM_HEADS):
        expand[h, h * SSM_HEADDIM:(h + 1) * SSM_HEADDIM] = 1.0
    expand = jnp.asarray(expand, BF16)
    dsk = jnp.repeat(d_skip.astype(F32), SSM_HEADDIM).reshape(1, SSM_INNER)
    chunk = lambda w: pl.BlockSpec((bsz, L, w), lambda c: (0, c, 0))
    const = lambda shape: pl.BlockSpec(shape, lambda c: (0,) * len(shape))
    return pl.pallas_call(
        _ssd_kernel,
        grid=(s_len // L,),
        in_specs=[chunk(SSM_CONV_DIM), chunk(LANES), chunk(SSM_INNER), const((SSM_CONV, SSM_CONV_DIM)),
                  const((1, SSM_CONV_DIM)), const((1, LANES)), const((1, LANES)), const((1, SSM_INNER)),
                  const((1, SSM_INNER)), const((L, L)), const((LANES, SSM_INNER))],
        out_specs=chunk(SSM_INNER),
        out_shape=jax.ShapeDtypeStruct((bsz, s_len, SSM_INNER), BF16),
        scratch_shapes=[pltpu.VMEM((bsz, CONV_PAD + L, SSM_CONV_DIM), F32),
                        pltpu.VMEM((bsz * SSM_GROUPS, SSM_STATE, SSM_INNER // SSM_GROUPS), F32)],
        compiler_params=pltpu.CompilerParams(dimension_semantics=("arbitrary",), vmem_limit_bytes=VMEM_LIMIT),
        name="ssd_scan",
    )(xbc, dt, z, conv_w, conv_b.reshape(1, -1), pad(dt_bias), pad(a_head), dsk, ssm_norm.reshape(1, -1),
      tri, expand)


HG_LEVELS = int(math.log2(HG_CHUNK))


def _hgrn_tables():
    C = HG_CHUNK
    idx = np.arange(C)
    tri = np.tril(np.ones((C, C), np.float32))
    mats = [tri]
    masks = []
    for lev in range(HG_LEVELS):
        h = C >> (lev + 1)
        mid = (idx // (2 * h)) * (2 * h) + h - 1
        upper = (idx % (2 * h)) >= h
        mats.append(tri[mid])
        same = (idx[:, None] // (2 * h)) == (idx[None, :] // (2 * h))
        masks.append((same & upper[:, None] & (~upper)[None, :]).astype(np.float32))
    masks.append(np.eye(C, dtype=np.float32))
    w = np.concatenate(mats, axis=0)
    return np.concatenate([w, w, w], axis=1), np.stack(masks, axis=0)


def _hgrn_kernel(hq_ref, hf_ref, hi_ref, hg_ref, lb_ref, gn_ref, w3_ref, masks_ref, o_ref, st_sc):
    C = HG_CHUNK

    @pl.when(pl.program_id(0) == 0)
    def _():
        st_sc[...] = jnp.zeros(st_sc.shape, F32)

    lb = lb_ref[...]
    for b in range(hq_ref.shape[0]):
        xf = hf_ref[b]
        g = jnp.log(lb + (1.0 - lb) * jax.nn.sigmoid(xf))
        kin = (1.0 - lb) * jax.nn.sigmoid(-xf)
        q = _silu(hq_ref[b].astype(F32))
        sums = _dot(w3_ref[...], jnp.concatenate(_split3(g), axis=0))
        gcum = sums[0:C]
        glast = gcum[C - 1:C]
        q_in = (q * jnp.exp(gcum)).astype(BF16)
        k_out = (kin * jnp.exp(glast - gcum)).astype(BF16)
        qts, kts = [], []
        for lev in range(HG_LEVELS):
            x = gcum - sums[C * (1 + lev):C * (2 + lev)]
            qts.append((q * jnp.exp(jnp.minimum(x, 0.0))).astype(BF16))
            kts.append((kin * jnp.exp(-jnp.maximum(x, 0.0))).astype(BF16))
        qts.append(q.astype(BF16))
        kts.append(kin.astype(BF16))
        v = hi_ref[b]
        gate = _silu(hg_ref[b].astype(F32))
        for h in range(HG_HEADS):
            sl = slice(h * HG_EXPAND, (h + 1) * HG_EXPAND)
            scores = masks_ref[0] * _dot_nt(qts[0][:, sl], kts[0][:, sl])
            for lev in range(1, HG_LEVELS + 1):
                scores = scores + masks_ref[lev] * _dot_nt(qts[lev][:, sl], kts[lev][:, sl])
            st = st_sc[b * HG_HEADS + h]
            vh = v[:, sl]
            o = _dot(scores.astype(BF16), vh) + _dot_nt(q_in[:, sl], st.astype(BF16))
            st_sc[b * HG_HEADS + h] = st * jnp.exp(glast[:, sl]) + _dot_tn(vh, k_out[:, sl])
            o_ref[b, :, sl] = (_rms(o, gn_ref[...]) * gate[:, sl]).astype(BF16)


def _hgrn2(hq, hf, hi, hg, lb, g_norm):
    bsz, s_len, _ = hq.shape
    C = HG_CHUNK
    w3, masks = _hgrn_tables()
    w3 = jnp.asarray(w3, BF16)
    masks = jnp.asarray(masks, F32)
    chunk = pl.BlockSpec((bsz, C, HG_WIDTH), lambda c: (0, c, 0))
    const = lambda shape: pl.BlockSpec(shape, lambda c: (0,) * len(shape))
    return pl.pallas_call(
        _hgrn_kernel,
        grid=(s_len // C,),
        in_specs=[chunk, chunk, chunk, chunk, const((1, HG_KDIM_TOTAL)), const((1, HG_VDIM)),
                  const(w3.shape), const(masks.shape)],
        out_specs=chunk,
        out_shape=jax.ShapeDtypeStruct((bsz, s_len, HG_WIDTH), BF16),
        scratch_shapes=[pltpu.VMEM((bsz * HG_HEADS, HG_VDIM, HG_EXPAND), F32)],
        compiler_params=pltpu.CompilerParams(dimension_semantics=("arbitrary",), vmem_limit_bytes=VMEM_LIMIT),
        name="hgrn2_scan",
    )(hq, hf, hi, hg, lb.reshape(1, -1), g_norm.reshape(1, -1), w3, masks)


def kernel(x, norm_mix, norm_ffn, norm_final, a_w_in, a_q_norm, a_w_uq, a_kv_norm, a_w_ukv, a_lq1, a_lk1, a_lq2, a_lk2, a_subln, a_w_out, s_w_in, s_conv_w, s_conv_b, s_dt_bias, s_a_log, s_d, s_norm, h_g_norm, h_lower_bound, s_w_out, ffn_gate, ffn_up, ffn_down):
    bsz, s_len, _ = x.shape
    n_tok = bsz * s_len
    depth = norm_mix.shape[0]
    assert depth == 2 and s_len % ATTN_TQ == 0
    p_lb = jax.nn.softmax(h_lower_bound.astype(F32), axis=0)
    lb_all = jnp.cumsum(p_lb, axis=0) - p_lb[0:1]

    lambda_init = 0.8 - 0.6 * math.exp(-0.3 * 0)
    q, k, vt, dq, dk, dvt = _attn_inproj(x, norm_mix[0], a_w_in[0], a_q_norm[0], a_w_uq[0], a_kv_norm[0],
                                       a_w_ukv[0], ts=256)
    o_mla = _mla_attention(q, k, vt, tq=ATTN_TQ)
    o_diff = _diff_attention(dq, dk, dvt, a_lq1[0], a_lk1[0], a_lq2[0], a_lk2[0], a_subln[0], lambda_init,
                             tq=ATTN_TQ)
    x2d = _outproj_ffn(x.reshape(n_tok, D_MODEL), o_mla.reshape(n_tok, -1), o_diff.reshape(n_tok, -1),
                       a_w_out[0], norm_ffn[0], ffn_gate[0], ffn_up[0], ffn_down[0], norm_final,
                       final_norm=False, tm=256)

    z, xbc, hq, hf, hi, hg, dt = _rec_inproj(x2d, norm_mix[1], s_w_in[0], tm=256)
    seq = lambda t: t.reshape(bsz, s_len, t.shape[-1])
    y = _ssd(seq(xbc), seq(dt), seq(z), s_conv_w[0], s_conv_b[0], s_dt_bias[0], s_a_log[0], s_d[0], s_norm[0])
    o = _hgrn2(seq(hq), seq(hf), seq(hi), seq(hg), lb_all[1], h_g_norm[0])
    x2d = _outproj_ffn(x2d, y.reshape(n_tok, -1), o.reshape(n_tok, -1), s_w_out[0], norm_ffn[1], ffn_gate[1],
                       ffn_up[1], ffn_down[1], norm_final, final_norm=True, tm=256)
    return x2d.reshape(bsz, s_len, D_MODEL)
```

```python
import functools
import math

import numpy as np
import jax
import jax.numpy as jnp
from jax import lax
from jax.experimental import pallas as pl
from jax.experimental.pallas import tpu as pltpu

F32 = jnp.float32
BF16 = jnp.bfloat16

D_MODEL = 1024
EPS = 1e-6
ROPE_THETA = 10000.0

MLA_HEADS = 8
MLA_Q_LORA = 384
MLA_KV_LORA = 256
MLA_NOPE = 64
MLA_ROPE = 32
MLA_V = 64
DIFF_HEADS = 4
DIFF_HD = 64
DIFF_V = 2 * DIFF_HD

SSM_HEADS = 8
SSM_HEADDIM = 64
SSM_INNER = SSM_HEADS * SSM_HEADDIM
SSM_GROUPS = 2
SSM_STATE = 128
SSM_CONV = 4
SSM_CHUNK = 128
SSM_CONV_DIM = SSM_INNER + 2 * SSM_GROUPS * SSM_STATE
HG_HEADS = 4
HG_EXPAND = 128
HG_VDIM = 128
HG_KDIM_TOTAL = HG_HEADS * HG_EXPAND
HG_WIDTH = HG_HEADS * HG_VDIM
HG_CHUNK = 64

LANES = 128
VMEM_LIMIT = 52 * 1024 * 1024

NT_DIMS = (((1,), (1,)), ((), ()))
TN_DIMS = (((0,), (0,)), ((), ()))


def _dot(a, b):
    return jnp.dot(a, b, preferred_element_type=F32)


def _dot_nt(a, b):
    return lax.dot_general(a, b, NT_DIMS, preferred_element_type=F32)


def _dot_tn(a, b):
    return lax.dot_general(a, b, TN_DIMS, preferred_element_type=F32)


def _rms(x, w):
    return x * lax.rsqrt(jnp.mean(x * x, axis=-1, keepdims=True) + EPS) * w


def _silu(x):
    return x * jax.nn.sigmoid(x)


def _split3(a):
    hi = a.astype(BF16)
    r = a - hi.astype(F32)
    mid = r.astype(BF16)
    lo = (r - mid.astype(F32)).astype(BF16)
    return hi, mid, lo


def _dot01_left(m01, a):
    hi, mid, lo = _split3(a)
    return (_dot(m01, hi) + _dot(m01, mid)) + _dot(m01, lo)


def _dot01_right(a, m01):
    hi, mid, lo = _split3(a)
    return (_dot(hi, m01) + _dot(mid, m01)) + _dot(lo, m01)


def _rope(t, cos, sin_a, sin_b, half):
    return (t * cos + pltpu.roll(t, LANES - half, 1) * sin_a + pltpu.roll(t, half, 1) * sin_b)


A_CQ = 0
A_CKV = A_CQ + MLA_Q_LORA
A_DQ = A_CKV + MLA_KV_LORA
A_DK = A_DQ + DIFF_HEADS * 2 * DIFF_HD
A_KR = A_DK + DIFF_HEADS * 2 * DIFF_HD
A_IN_PACKED = A_KR + LANES


def _attn_inproj_kernel(x_ref, nw_ref, win_ref, wdvt_ref, qn_ref, wuq_ref, kvn_ref, wuk_ref, wuvt_ref,
                        cm_ref, sam_ref, sbm_ref, cd_ref, sad_ref, sbd_ref,
                        q_ref, k_ref, vt_ref, dq_ref, dk_ref, dvt_ref):
    hn = _rms(x_ref[0], nw_ref[...]).astype(BF16)
    proj = _dot(hn, win_ref[...])
    cq = _rms(proj[:, A_CQ:A_CKV], qn_ref[...]).astype(BF16)
    ckv = _rms(proj[:, A_CKV:A_DQ], kvn_ref[...]).astype(BF16)
    q = _dot(cq, wuq_ref[...])
    kn = _dot(ckv, wuk_ref[...])
    vt = _dot_nt(wuvt_ref[...], ckv).astype(BF16)
    dvt = _dot_nt(wdvt_ref[...], hn).astype(BF16)
    ones = jnp.ones((VT_ONES, vt.shape[1]), BF16)
    for g in range(4):
        vt_ref[0, g, 0:LANES] = vt[g * LANES:(g + 1) * LANES]
        vt_ref[0, g, LANES:VT_ROWS] = ones
        dvt_ref[0, g, 0:LANES] = dvt[g * LANES:(g + 1) * LANES]
        dvt_ref[0, g, LANES:VT_ROWS] = ones
    cm, sam, sbm = cm_ref[...], sam_ref[...], sbm_ref[...]
    kpe = _rope(proj[:, A_KR:A_KR + LANES], cm, sam, sbm, MLA_ROPE // 2)
    for h in range(MLA_HEADS):
        sl = slice(h * LANES, (h + 1) * LANES)
        q_ref[0, h] = _rope(q[:, sl], cm, sam, sbm, MLA_ROPE // 2).astype(BF16)
        k_ref[0, h] = (kn[:, sl] + kpe).astype(BF16)
    cd, sad, sbd = cd_ref[...], sad_ref[...], sbd_ref[...]
    for g in range(DIFF_HEADS):
        sl = slice(g * LANES, (g + 1) * LANES)
        dq = proj[:, A_DQ + g * LANES:A_DQ + (g + 1) * LANES]
        dk = proj[:, A_DK + g * LANES:A_DK + (g + 1) * LANES]
        dq_ref[0, :, sl] = (_rope(dq, cd, sad, sbd, DIFF_HD // 2) * (DIFF_HD ** -0.5 * LOG2E)).astype(BF16)
        dk_ref[0, :, sl] = _rope(dk, cd, sad, sbd, DIFF_HD // 2).astype(BF16)


def _rope_tables(seq_len, dim, lane_offsets):
    half = dim // 2
    inv_freq = 1.0 / (ROPE_THETA ** (jnp.arange(0, dim, 2, dtype=F32) / dim))
    ang = jnp.arange(seq_len, dtype=F32)[:, None] * inv_freq[None, :]
    cos, sin = jnp.cos(ang), jnp.sin(ang)
    zero = jnp.zeros_like(sin)
    c, sa, sb = [], [], []
    lane = 0
    for off in lane_offsets:
        gap = off - lane
        c += [jnp.ones((seq_len, gap), F32), cos, cos]
        sa += [jnp.zeros((seq_len, gap), F32), -sin, zero]
        sb += [jnp.zeros((seq_len, gap), F32), zero, sin]
        lane = off + dim
    c.append(jnp.ones((seq_len, LANES - lane), F32))
    sa.append(jnp.zeros((seq_len, LANES - lane), F32))
    sb.append(jnp.zeros((seq_len, LANES - lane), F32))
    return tuple(jnp.concatenate(t, axis=-1) for t in (c, sa, sb))


def _const_spec(shape):
    nd = len(shape)
    return pl.BlockSpec(shape, lambda *_: (0,) * nd, pipeline_mode=pl.Buffered(1))


def _attn_inproj(x, norm_w, w_in, q_norm, w_uq, kv_norm, w_ukv, ts):
    bsz, s_len, _ = x.shape
    cq, ckv, kr, dq, dk, dv = jnp.split(
        w_in, [int(v) for v in np.cumsum([MLA_Q_LORA, MLA_KV_LORA, MLA_ROPE, 512, 512])], axis=-1)
    kr_pad = jnp.pad(kr, ((0, 0), (MLA_NOPE, LANES - MLA_NOPE - MLA_ROPE)))
    win_p = jnp.concatenate([cq, ckv, dq, dk, kr_pad], axis=-1).astype(BF16)
    wdvt = dv.T.astype(BF16)
    scale = (MLA_NOPE + MLA_ROPE) ** -0.5 * LOG2E
    wuq_p = jnp.pad((w_uq * scale).reshape(MLA_Q_LORA, MLA_HEADS, MLA_NOPE + MLA_ROPE),
                    ((0, 0), (0, 0), (0, LANES - MLA_NOPE - MLA_ROPE)))
    wuq_p = wuq_p.reshape(MLA_Q_LORA, MLA_HEADS * LANES).astype(BF16)
    wkv = w_ukv.reshape(MLA_KV_LORA, MLA_HEADS, MLA_NOPE + MLA_V)
    wuk_p = jnp.pad(wkv[..., :MLA_NOPE], ((0, 0), (0, 0), (0, LANES - MLA_NOPE)))
    wuk_p = wuk_p.reshape(MLA_KV_LORA, MLA_HEADS * LANES).astype(BF16)
    wuvt = wkv[..., MLA_NOPE:].reshape(MLA_KV_LORA, MLA_HEADS * MLA_V).T.astype(BF16)
    tabs_m = _rope_tables(s_len, MLA_ROPE, (MLA_NOPE,))
    tabs_d = _rope_tables(s_len, DIFF_HD, (0, DIFF_HD))

    row = lambda b, i: (b, i, 0)
    tab = pl.BlockSpec((ts, LANES), lambda b, i: (i, 0))
    head_major = pl.BlockSpec((1, MLA_HEADS, ts, LANES), lambda b, i: (b, 0, i, 0))
    wide = pl.BlockSpec((1, ts, 512), row)
    transposed = pl.BlockSpec((1, 4, VT_ROWS, ts), lambda b, i: (b, 0, 0, i))
    return pl.pallas_call(
        _attn_inproj_kernel,
        grid=(bsz, s_len // ts),
        in_specs=[pl.BlockSpec((1, ts, D_MODEL), row), _const_spec((1, D_MODEL)),
                  _const_spec(win_p.shape), _const_spec(wdvt.shape), _const_spec((1, MLA_Q_LORA)),
                  _const_spec(wuq_p.shape), _const_spec((1, MLA_KV_LORA)), _const_spec(wuk_p.shape),
                  _const_spec(wuvt.shape), tab, tab, tab, tab, tab, tab],
        out_specs=[head_major, head_major, transposed, wide, wide, transposed],
        out_shape=[jax.ShapeDtypeStruct((bsz, MLA_HEADS, s_len, LANES), BF16),
                   jax.ShapeDtypeStruct((bsz, MLA_HEADS, s_len, LANES), BF16),
                   jax.ShapeDtypeStruct((bsz, 4, VT_ROWS, s_len), BF16),
                   jax.ShapeDtypeStruct((bsz, s_len, 512), BF16),
                   jax.ShapeDtypeStruct((bsz, s_len, 512), BF16),
                   jax.ShapeDtypeStruct((bsz, 4, VT_ROWS, s_len), BF16)],
        compiler_params=pltpu.CompilerParams(dimension_semantics=("parallel", "parallel"),
                                             vmem_limit_bytes=VMEM_LIMIT),
        name="attn_inproj",
    )(x, norm_w.reshape(1, -1), win_p, wdvt, q_norm.reshape(1, -1), wuq_p, kv_norm.reshape(1, -1), wuk_p, wuvt,
      *tabs_m, *tabs_d)


ATTN_TQ = 512
ATTN_GROUP = 256
VT_ONES = 16
VT_ROWS = LANES + VT_ONES
LOG2E = math.log2(math.e)


def _attn_sweep(i, n_q, tq, load_queries, k_rows, vt_ref, qs_sc, sa_sc, sb_sc, m_sc, acc_sc):
    nq = sa_sc.shape[1]
    groups = [slice(c, c + ATTN_GROUP) for c in range(0, nq, ATTN_GROUP)]

    def start_of(blk):
        return pl.multiple_of(blk * tq, tq)

    def fill_scores(buf, start, cols):
        buf[:, cols] = _dot_nt(k_rows(start, cols), qs_sc[cols])

    def process(s_ref, blk, cols, diagonal=False):
        rows = (cols.start % tq) + ATTN_GROUP if diagonal else tq
        s = s_ref[0:rows, cols]
        if diagonal:
            key = lax.broadcasted_iota(jnp.int32, s.shape, 0)
            qry = (lax.broadcasted_iota(jnp.int32, s.shape, 1) + cols.start) & (tq - 1)
            s = jnp.where(key <= qry, s, -jnp.inf)
        m_prev = m_sc[:, cols]
        m_new = jnp.maximum(m_prev, jnp.max(s, axis=0, keepdims=True))
        alpha = jnp.exp2(m_prev - m_new)
        p = jnp.exp2((s - m_new).astype(BF16))
        acc_sc[:, cols] = alpha * acc_sc[:, cols] + _dot(vt_ref[0, 0, :, pl.ds(start_of(blk), rows)], p)
        m_sc[:, cols] = m_new

    @pl.when(i == 0)
    def _():
        load_queries(start_of(0))
        for cols in groups:
            fill_scores(sa_sc, start_of(0), cols)

    m_sc[...] = jnp.full(m_sc.shape, -jnp.inf, F32)
    acc_sc[...] = jnp.zeros(acc_sc.shape, F32)

    def body(jj, carry):
        a = 2 * jj
        for cols in groups:
            fill_scores(sb_sc, start_of(a + 1), cols)
            process(sa_sc, a, cols)
        for cols in groups:
            fill_scores(sa_sc, start_of(a + 2), cols)
            process(sb_sc, a + 1, cols)
        return carry

    n_main = i // 2
    lax.fori_loop(0, n_main, body, 0)

    @pl.when(i % 2 == 0)
    def _():
        for cols in groups:
            process(sa_sc, i, cols, diagonal=True)

    @pl.when(i % 2 == 1)
    def _():
        for cols in groups:
            fill_scores(sb_sc, start_of(i), cols)
            process(sa_sc, i - 1, cols)
        for cols in groups:
            process(sb_sc, i, cols, diagonal=True)

    load_queries(start_of(jnp.minimum(i + 1, n_q - 1)))
    for cols in groups:
        fill_scores(sa_sc, start_of(0), cols)
    return acc_sc[LANES:LANES + 1, :]


def _attn_scratch(tq):
    nq = 2 * tq
    return [pltpu.VMEM((nq, LANES), BF16), pltpu.VMEM((tq, nq), F32), pltpu.VMEM((tq, nq), F32),
            pltpu.VMEM((1, nq), F32), pltpu.VMEM((VT_ROWS, nq), F32)]


def _mla_attn_kernel(q_ref, k_ref, vt_ref, o_ref, qs_sc, sa_sc, sb_sc, m_sc, acc_sc, *, tq, n_q):
    def load_queries(start):
        for h in range(2):
            qs_sc[h * tq:(h + 1) * tq] = q_ref[0, h, pl.ds(start, tq), :]

    def k_rows(start, cols):
        return k_ref[0, cols.start // tq, pl.ds(start, tq), :]

    l = _attn_sweep(pl.program_id(2), n_q, tq, load_queries, k_rows, vt_ref, qs_sc, sa_sc, sb_sc, m_sc, acc_sc)
    o_t = jnp.concatenate([acc_sc[0:MLA_V, 0:tq] / l[:, 0:tq],
                           acc_sc[MLA_V:2 * MLA_V, tq:2 * tq] / l[:, tq:2 * tq]], axis=0)
    o_ref[0] = o_t.T.astype(BF16)


def _mla_attention(q, k, vt, tq):
    bsz, nh, s_len, _ = q.shape
    return pl.pallas_call(
        functools.partial(_mla_attn_kernel, tq=tq, n_q=s_len // tq),
        grid=(bsz, nh // 2, s_len // tq),
        in_specs=[pl.BlockSpec((1, 2, s_len, LANES), lambda b, p, i: (b, p, 0, 0)),
                  pl.BlockSpec((1, 2, s_len, LANES), lambda b, p, i: (b, p, 0, 0)),
                  pl.BlockSpec((1, 1, VT_ROWS, s_len), lambda b, p, i: (b, p, 0, 0))],
        out_specs=pl.BlockSpec((1, tq, LANES), lambda b, p, i: (b, i, p)),
        out_shape=jax.ShapeDtypeStruct((bsz, s_len, nh * MLA_V), BF16),
        scratch_shapes=_attn_scratch(tq),
        compiler_params=pltpu.CompilerParams(dimension_semantics=("parallel", "parallel", "arbitrary"),
                                             vmem_limit_bytes=VMEM_LIMIT),
        name="mla_attention",
    )(q, k, vt)


def _diff_attn_kernel(q_ref, k_ref, vt_ref, lq1_ref, lk1_ref, lq2_ref, lk2_ref, sub_ref, o_ref,
                      qs_sc, sa_sc, sb_sc, m_sc, acc_sc, *, tq, n_q, lambda_init):
    def load_queries(start):
        q = q_ref[0, pl.ds(start, tq), :]
        lane = lax.broadcasted_iota(jnp.int32, q.shape, 1)
        zero = jnp.zeros_like(q)
        qs_sc[0:tq] = jnp.where(lane < DIFF_HD, q, zero)
        qs_sc[tq:2 * tq] = jnp.where(lane < DIFF_HD, zero, q)

    def k_rows(start, cols):
        return k_ref[0, pl.ds(start, tq), :]

    l = _attn_sweep(pl.program_id(2), n_q, tq, load_queries, k_rows, vt_ref, qs_sc, sa_sc, sb_sc, m_sc, acc_sc)
    o_t = acc_sc[0:LANES, :] / l
    lam = (jnp.exp(jnp.sum(lq1_ref[...] * lk1_ref[...], axis=-1, keepdims=True))
           - jnp.exp(jnp.sum(lq2_ref[...] * lk2_ref[...], axis=-1, keepdims=True)) + lambda_init)
    od = (o_t[:, 0:tq] - lam * o_t[:, tq:2 * tq]).T
    o_ref[0] = (_rms(od, sub_ref[...]) * (1.0 - lambda_init)).astype(BF16)


def _diff_attention(dq, dk, dvt, lq1, lk1, lq2, lk2, subln, lambda_init, tq):
    bsz, s_len, _ = dq.shape
    qspec = pl.BlockSpec((1, tq, LANES), lambda b, h, i: (b, i, h))
    seq = pl.BlockSpec((1, s_len, LANES), lambda b, h, i: (b, 0, h))
    vec = lambda n: pl.BlockSpec((1, n), lambda b, h, i: (0, 0))
    return pl.pallas_call(
        functools.partial(_diff_attn_kernel, tq=tq, n_q=s_len // tq, lambda_init=lambda_init),
        grid=(bsz, DIFF_HEADS, s_len // tq),
        in_specs=[seq, seq,
                  pl.BlockSpec((1, 1, VT_ROWS, s_len), lambda b, h, i: (b, h, 0, 0)),
                  vec(DIFF_HD), vec(DIFF_HD), vec(DIFF_HD), vec(DIFF_HD), vec(DIFF_V)],
        out_specs=qspec,
        out_shape=jax.ShapeDtypeStruct((bsz, s_len, DIFF_HEADS * DIFF_V), BF16),
        scratch_shapes=_attn_scratch(tq),
        compiler_params=pltpu.CompilerParams(dimension_semantics=("parallel", "parallel", "arbitrary"),
                                             vmem_limit_bytes=VMEM_LIMIT),
        name="diff_attention",
    )(dq, dk, dvt, lq1.reshape(1, -1), lk1.reshape(1, -1), lq2.reshape(1, -1), lk2.reshape(1, -1),
      subln.reshape(1, -1))


def _outproj_ffn_kernel(x_ref, ma_ref, mb_ref, woa_ref, wob_ref, nf_ref, wg_ref, wu_ref, wd_ref, fin_ref,
                        o_ref, *, final_norm):
    x1 = x_ref[...] + _dot(ma_ref[...], woa_ref[...]) + _dot(mb_ref[...], wob_ref[...])
    h = _rms(x1, nf_ref[...]).astype(BF16)
    g = _dot(h, wg_ref[...])
    u = _dot(h, wu_ref[...])
    a = (_silu(g) * u).astype(BF16)
    x2 = x1 + _dot(a, wd_ref[...])
    if final_norm:
        x2 = _rms(x2, fin_ref[...])
    o_ref[...] = x2


def _outproj_ffn(x2d, mix_a, mix_b, w_out, norm_ffn, layer, w_gate, w_up, w_down, norm_final, final_norm, tm):
    n_tok = x2d.shape[0]
    half = mix_a.shape[1]
    d_ff = w_gate.shape[2]
    woa = w_out[:half].astype(BF16)
    wob = w_out[half:].astype(BF16)
    row = lambda i: (i, 0)
    layer_spec = lambda r, c: pl.BlockSpec((None, r, c), lambda i: (layer, 0, 0), pipeline_mode=pl.Buffered(1))
    return pl.pallas_call(
        functools.partial(_outproj_ffn_kernel, final_norm=final_norm),
        grid=(n_tok // tm,),
        in_specs=[pl.BlockSpec((tm, D_MODEL), row), pl.BlockSpec((tm, half), row), pl.BlockSpec((tm, half), row),
                  _const_spec(woa.shape), _const_spec(wob.shape), _const_spec((1, D_MODEL)),
                  layer_spec(D_MODEL, d_ff), layer_spec(D_MODEL, d_ff), layer_spec(d_ff, D_MODEL),
                  _const_spec((1, D_MODEL))],
        out_specs=pl.BlockSpec((tm, D_MODEL), row),
        out_shape=jax.ShapeDtypeStruct((n_tok, D_MODEL), F32),
        compiler_params=pltpu.CompilerParams(dimension_semantics=("parallel",), vmem_limit_bytes=VMEM_LIMIT),
        name="outproj_ffn",
    )(x2d, mix_a, mix_b, woa, wob, norm_ffn.reshape(1, -1), w_gate, w_up, w_down, norm_final.reshape(1, -1))


S_Z = 0
S_XBC = S_Z + SSM_INNER
S_HQ = S_XBC + SSM_CONV_DIM
S_HF = S_HQ + HG_KDIM_TOTAL
S_HI = S_HF + HG_KDIM_TOTAL
S_HG = S_HI + HG_WIDTH
S_DT = S_HG + HG_WIDTH
S_IN_PACKED = S_DT + LANES


def _rec_inproj_kernel(x_ref, nw_ref, win_ref, z_ref, xbc_ref, hq_ref, hf_ref, hi_ref, hg_ref, dt_ref):
    hn = _rms(x_ref[...], nw_ref[...]).astype(BF16)
    proj = _dot(hn, win_ref[...])
    z_ref[...] = proj[:, S_Z:S_XBC].astype(BF16)
    xbc_ref[...] = proj[:, S_XBC:S_HQ]
    hq_ref[...] = proj[:, S_HQ:S_HF].astype(BF16)
    hf_ref[...] = proj[:, S_HF:S_HI]
    hi_ref[...] = proj[:, S_HI:S_HG].astype(BF16)
    hg_ref[...] = proj[:, S_HG:S_DT].astype(BF16)
    dt_ref[...] = proj[:, S_DT:S_IN_PACKED]


def _rec_inproj(x2d, norm_w, w_in, tm):
    n_tok = x2d.shape[0]
    z, xbc, dt, hq, hf, hi, hg = jnp.split(
        w_in, [int(v) for v in np.cumsum([SSM_INNER, SSM_CONV_DIM, SSM_HEADS, 512, 512, 512])], axis=-1)
    dt_pad = jnp.pad(dt, ((0, 0), (0, LANES - SSM_HEADS)))
    win_p = jnp.concatenate([z, xbc, hq, hf, hi, hg, dt_pad], axis=-1).astype(BF16)
    row = lambda i: (i, 0)
    widths = [(SSM_INNER, BF16), (SSM_CONV_DIM, F32), (512, BF16), (512, F32), (512, BF16), (512, BF16),
              (LANES, F32)]
    return pl.pallas_call(
        _rec_inproj_kernel,
        grid=(n_tok // tm,),
        in_specs=[pl.BlockSpec((tm, D_MODEL), row), _const_spec((1, D_MODEL)), _const_spec(win_p.shape)],
        out_specs=[pl.BlockSpec((tm, w), row) for w, _ in widths],
        out_shape=[jax.ShapeDtypeStruct((n_tok, w), dt_) for w, dt_ in widths],
        compiler_params=pltpu.CompilerParams(dimension_semantics=("parallel",), vmem_limit_bytes=VMEM_LIMIT),
        name="rec_inproj",
    )(x2d, norm_w.reshape(1, -1), win_p)


CONV_PAD = 8


def _ssd_kernel(xbc_ref, dt_ref, z_ref, cw_ref, cb_ref, dtb_ref, ah_ref, dsk_ref, nw_ref, tri_ref, exp_ref,
                y_ref, xpad_sc, st_sc):
    L = SSM_CHUNK
    heads_per_group = SSM_HEADS // SSM_GROUPS
    gw = heads_per_group * SSM_HEADDIM

    @pl.when(pl.program_id(0) == 0)
    def _():
        xpad_sc[:, 0:CONV_PAD] = jnp.zeros((xpad_sc.shape[0], CONV_PAD, SSM_CONV_DIM), F32)
        st_sc[...] = jnp.zeros(st_sc.shape, F32)

    r = lax.broadcasted_iota(jnp.int32, (L, L), 0)
    c = lax.broadcasted_iota(jnp.int32, (L, L), 1)
    causal = c <= r
    lane = lax.broadcasted_iota(jnp.int32, (L, LANES), 1)
    nw = nw_ref[...]

    for b in range(xbc_ref.shape[0]):
        xt = xbc_ref[b]
        xpad_sc[b, CONV_PAD:CONV_PAD + L] = xt
        conv = cb_ref[...] + cw_ref[SSM_CONV - 1:SSM_CONV] * xt
        for d in range(1, SSM_CONV):
            conv = conv + cw_ref[SSM_CONV - 1 - d:SSM_CONV - d] * xpad_sc[b, CONV_PAD - d:CONV_PAD - d + L]
        xpad_sc[b, 0:CONV_PAD] = xt[L - CONV_PAD:L]
        xc = _silu(conv)
        xs = xc[:, 0:SSM_INNER]
        b_in = xc[:, SSM_INNER:SSM_INNER + SSM_GROUPS * SSM_STATE].astype(BF16)
        c_in = xc[:, SSM_INNER + SSM_GROUPS * SSM_STATE:].astype(BF16)

        dt = jax.nn.softplus(dt_ref[b] + dtb_ref[...])
        a = dt * ah_ref[...]
        a_cs = _dot01_left(tri_ref[...], a)
        a_cs_t = a_cs.T
        dt_e = _dot01_right(dt, exp_ref[...])
        acs_e = _dot01_right(a_cs, exp_ref[...])
        alast_e = acs_e[L - 1:L, :]
        xdt = xs * dt_e
        xdec = (xdt * jnp.exp(alast_e - acs_e)).astype(BF16)
        xdt_b = xdt.astype(BF16)
        eacs = jnp.exp(acs_e)

        ys = []
        for g in range(SSM_GROUPS):
            gs = slice(g * gw, (g + 1) * gw)
            bg = b_in[:, g * SSM_STATE:(g + 1) * SSM_STATE]
            cg = c_in[:, g * SSM_STATE:(g + 1) * SSM_STATE]
            cb = _dot_nt(cg, bg)
            st_prev = st_sc[b * SSM_GROUPS + g]
            y_off = _dot(cg, st_prev.astype(BF16)) * eacs[:, gs]
            st_sc[b * SSM_GROUPS + g] = st_prev * jnp.exp(alast_e[:, gs]) + _dot_tn(bg, xdec[:, gs])
            for pr in range(heads_per_group // 2):
                xpair = xdt_b[:, g * gw + pr * LANES:g * gw + (pr + 1) * LANES]
                res = []
                for hh in range(2):
                    h = g * heads_per_group + 2 * pr + hh
                    seg = jnp.exp(jnp.minimum(a_cs[:, h:h + 1] - a_cs_t[h:h + 1, :], 0.0))
                    m = jnp.where(causal, cb * seg, 0.0).astype(BF16)
                    res.append(_dot(m, xpair))
                ys.append(jnp.where(lane < SSM_HEADDIM, res[0], res[1]) + y_off[:, pr * LANES:(pr + 1) * LANES])
        y = jnp.concatenate(ys, axis=1) + dsk_ref[...] * xs
        y = y * _silu(z_ref[b].astype(F32))
        for g in range(SSM_GROUPS):
            sl = slice(g * gw, (g + 1) * gw)
            y_ref[b, :, sl] = _rms(y[:, sl], nw[:, sl]).astype(BF16)


def _ssd(xbc, dt, z, conv_w, conv_b, dt_bias, a_log, d_skip, ssm_norm):
    bsz, s_len, _ = xbc.shape
    L = SSM_CHUNK
    pad = lambda v: jnp.pad(v.astype(F32), (0, LANES - SSM_HEADS)).reshape(1, LANES)
    a_head = -jnp.exp(a_log.astype(F32))
    tri = jnp.asarray(np.tril(np.ones((L, L), np.float32)), BF16)
    expand = np.zeros((LANES, SSM_INNER), np.float32)
    for h in range(SSM_HEADS):
        expand[h, h * SSM_HEADDIM:(h + 1) * SSM_HEADDIM] = 1.0
    expand = jnp.asarray(expand, BF16)
    dsk = jnp.repeat(d_skip.astype(F32), SSM_HEADDIM).reshape(1, SSM_INNER)
    chunk = lambda w: pl.BlockSpec((bsz, L, w), lambda c: (0, c, 0))
    const = lambda shape: pl.BlockSpec(shape, lambda c: (0,) * len(shape))
    return pl.pallas_call(
        _ssd_kernel,
        grid=(s_len // L,),
        in_specs=[chunk(SSM_CONV_DIM), chunk(LANES), chunk(SSM_INNER), const((SSM_CONV, SSM_CONV_DIM)),
                  const((1, SSM_CONV_DIM)), const((1, LANES)), const((1, LANES)), const((1, SSM_INNER)),
                  const((1, SSM_INNER)), const((L, L)), const((LANES, SSM_INNER))],
        out_specs=chunk(SSM_INNER),
        out_shape=jax.ShapeDtypeStruct((bsz, s_len, SSM_INNER), BF16),
        scratch_shapes=[pltpu.VMEM((bsz, CONV_PAD + L, SSM_CONV_DIM), F32),
                        pltpu.VMEM((bsz * SSM_GROUPS, SSM_STATE, SSM_INNER // SSM_GROUPS), F32)],
        compiler_params=pltpu.CompilerParams(dimension_semantics=("arbitrary",), vmem_limit_bytes=VMEM_LIMIT),
        name="ssd_scan",
    )(xbc, dt, z, conv_w, conv_b.reshape(1, -1), pad(dt_bias), pad(a_head), dsk, ssm_norm.reshape(1, -1),
      tri, expand)


HG_LEVELS = int(math.log2(HG_CHUNK))


def _hgrn_tables():
    C = HG_CHUNK
    idx = np.arange(C)
    tri = np.tril(np.ones((C, C), np.float32))
    mats = [tri]
    masks = []
    for lev in range(HG_LEVELS):
        h = C >> (lev + 1)
        mid = (idx // (2 * h)) * (2 * h) + h - 1
        upper = (idx % (2 * h)) >= h
        mats.append(tri[mid])
        same = (idx[:, None] // (2 * h)) == (idx[None, :] // (2 * h))
        masks.append((same & upper[:, None] & (~upper)[None, :]).astype(np.float32))
    masks.append(np.eye(C, dtype=np.float32))
    w = np.concatenate(mats, axis=0)
    return np.concatenate([w, w, w], axis=1), np.stack(masks, axis=0)


def _hgrn_kernel(hq_ref, hf_ref, hi_ref, hg_ref, lb_ref, gn_ref, w3_ref, masks_ref, o_ref, st_sc):
    C = HG_CHUNK

    @pl.when(pl.program_id(0) == 0)
    def _():
        st_sc[...] = jnp.zeros(st_sc.shape, F32)

    lb = lb_ref[...]
    row = lax.broadcasted_iota(jnp.int32, (C, HG_KDIM_TOTAL), 0)
    for b in range(hq_ref.shape[0]):
        xf = hf_ref[b]
        g = jnp.log(lb + (1.0 - lb) * jax.nn.sigmoid(xf))
        kin = (1.0 - lb) * jax.nn.sigmoid(-xf)
        q = _silu(hq_ref[b].astype(F32))
        sums = _dot(w3_ref[...], jnp.concatenate(_split3(g), axis=0))
        gcum = sums[0:C]
        glast = gcum[C - 1:C]
        q_in = (q * jnp.exp(gcum)).astype(BF16)
        k_out = (kin * jnp.exp(glast - gcum)).astype(BF16)
        zs = []
        for lev in range(HG_LEVELS):
            x = gcum - sums[C * (1 + lev):C * (2 + lev)]
            after_mid = (row & (C >> (lev + 1))) != 0
            zs.append((jnp.where(after_mid, q, kin) * jnp.exp(-jnp.abs(x))).astype(BF16))
        qb, kb = q.astype(BF16), kin.astype(BF16)
        v = hi_ref[b]
        gate = _silu(hg_ref[b].astype(F32))
        for h in range(HG_HEADS):
            sl = slice(h * HG_EXPAND, (h + 1) * HG_EXPAND)
            scores = masks_ref[HG_LEVELS] * _dot_nt(qb[:, sl], kb[:, sl])
            for lev in range(HG_LEVELS):
                scores = scores + masks_ref[lev] * _dot_nt(zs[lev][:, sl], zs[lev][:, sl])
            st = st_sc[b * HG_HEADS + h]
            vh = v[:, sl]
            o = _dot(scores.astype(BF16), vh) + _dot_nt(q_in[:, sl], st.astype(BF16))
            st_sc[b * HG_HEADS + h] = st * jnp.exp(glast[:, sl]) + _dot_tn(vh, k_out[:, sl])
            o_ref[b, :, sl] = (_rms(o, gn_ref[...]) * gate[:, sl]).astype(BF16)


def _hgrn2(hq, hf, hi, hg, lb, g_norm):
    bsz, s_len, _ = hq.shape
    C = HG_CHUNK
    w3, masks = _hgrn_tables()
    w3 = jnp.asarray(w3, BF16)
    masks = jnp.asarray(masks, F32)
    chunk = pl.BlockSpec((bsz, C, HG_WIDTH), lambda c: (0, c, 0))
    const = lambda shape: pl.BlockSpec(shape, lambda c: (0,) * len(shape))
    return pl.pallas_call(
        _hgrn_kernel,
        grid=(s_len // C,),
        in_specs=[chunk, chunk, chunk, chunk, const((1, HG_KDIM_TOTAL)), const((1, HG_VDIM)),
                  const(w3.shape), const(masks.shape)],
        out_specs=chunk,
        out_shape=jax.ShapeDtypeStruct((bsz, s_len, HG_WIDTH), BF16),
        scratch_shapes=[pltpu.VMEM((bsz * HG_HEADS, HG_VDIM, HG_EXPAND), F32)],
        compiler_params=pltpu.CompilerParams(dimension_semantics=("arbitrary",), vmem_limit_bytes=VMEM_LIMIT),
        name="hgrn2_scan",
    )(hq, hf, hi, hg, lb.reshape(1, -1), g_norm.reshape(1, -1), w3, masks)


def kernel(x, norm_mix, norm_ffn, norm_final, a_w_in, a_q_norm, a_w_uq, a_kv_norm, a_w_ukv, a_lq1, a_lk1, a_lq2, a_lk2, a_subln, a_w_out, s_w_in, s_conv_w, s_conv_b, s_dt_bias, s_a_log, s_d, s_norm, h_g_norm, h_lower_bound, s_w_out, ffn_gate, ffn_up, ffn_down):
    bsz, s_len, _ = x.shape
    n_tok = bsz * s_len
    depth = norm_mix.shape[0]
    assert depth == 2 and s_len % ATTN_TQ == 0
    p_lb = jax.nn.softmax(h_lower_bound.astype(F32), axis=0)
    lb_all = jnp.cumsum(p_lb, axis=0) - p_lb[0:1]
    w_gate, w_up, w_down = ffn_gate.astype(BF16), ffn_up.astype(BF16), ffn_down.astype(BF16)

    lambda_init = 0.8 - 0.6 * math.exp(-0.3 * 0)
    q, k, vt, dq, dk, dvt = _attn_inproj(x, norm_mix[0], a_w_in[0], a_q_norm[0], a_w_uq[0], a_kv_norm[0],
                                       a_w_ukv[0], ts=256)
    o_mla = _mla_attention(q, k, vt, tq=ATTN_TQ)
    o_diff = _diff_attention(dq, dk, dvt, a_lq1[0], a_lk1[0], a_lq2[0], a_lk2[0], a_subln[0], lambda_init,
                             tq=ATTN_TQ)
    x2d = _outproj_ffn(x.reshape(n_tok, D_MODEL), o_mla.reshape(n_tok, -1), o_diff.reshape(n_tok, -1),
                       a_w_out[0], norm_ffn[0], 0, w_gate, w_up, w_down, norm_final, final_norm=False, tm=256)

    z, xbc, hq, hf, hi, hg, dt = _rec_inproj(x2d, norm_mix[1], s_w_in[0], tm=256)
    seq = lambda t: t.reshape(bsz, s_len, t.shape[-1])
    y = _ssd(seq(xbc), seq(dt), seq(z), s_conv_w[0], s_conv_b[0], s_dt_bias[0], s_a_log[0], s_d[0], s_norm[0])
    o = _hgrn2(seq(hq), seq(hf), seq(hi), seq(hg), lb_all[1], h_g_norm[0])
    x2d = _outproj_ffn(x2d, y.reshape(n_tok, -1), o.reshape(n_tok, -1), s_w_out[0], norm_ffn[1], 1, w_gate, w_up,
                       w_down, norm_final, final_norm=True, tm=256)
    return x2d.reshape(bsz, s_len, D_MODEL)
```

```python
import functools
import math

import numpy as np
import jax
import jax.numpy as jnp
from jax import lax
from jax.experimental import pallas as pl
from jax.experimental.pallas import tpu as pltpu

F32 = jnp.float32
BF16 = jnp.bfloat16

D_MODEL = 1024
EPS = 1e-6
ROPE_THETA = 10000.0

MLA_HEADS = 8
MLA_Q_LORA = 384
MLA_KV_LORA = 256
MLA_NOPE = 64
MLA_ROPE = 32
MLA_V = 64
DIFF_HEADS = 4
DIFF_HD = 64
DIFF_V = 2 * DIFF_HD

SSM_HEADS = 8
SSM_HEADDIM = 64
SSM_INNER = SSM_HEADS * SSM_HEADDIM
SSM_GROUPS = 2
SSM_STATE = 128
SSM_CONV = 4
SSM_CHUNK = 128
SSM_CONV_DIM = SSM_INNER + 2 * SSM_GROUPS * SSM_STATE
HG_HEADS = 4
HG_EXPAND = 128
HG_VDIM = 128
HG_KDIM_TOTAL = HG_HEADS * HG_EXPAND
HG_WIDTH = HG_HEADS * HG_VDIM
HG_CHUNK = 64

LANES = 128
VMEM_LIMIT = 52 * 1024 * 1024

NT_DIMS = (((1,), (1,)), ((), ()))
TN_DIMS = (((0,), (0,)), ((), ()))


def _dot(a, b):
    return jnp.dot(a, b, preferred_element_type=F32)


def _dot_nt(a, b):
    return lax.dot_general(a, b, NT_DIMS, preferred_element_type=F32)


def _dot_tn(a, b):
    return lax.dot_general(a, b, TN_DIMS, preferred_element_type=F32)


def _rms(x, w):
    return x * lax.rsqrt(jnp.mean(x * x, axis=-1, keepdims=True) + EPS) * w


def _silu(x):
    return x * jax.nn.sigmoid(x)


def _split3(a):
    hi = a.astype(BF16)
    r = a - hi.astype(F32)
    mid = r.astype(BF16)
    lo = (r - mid.astype(F32)).astype(BF16)
    return hi, mid, lo


def _dot01_left(m01, a):
    hi, mid, lo = _split3(a)
    return (_dot(m01, hi) + _dot(m01, mid)) + _dot(m01, lo)


def _dot01_right(a, m01):
    hi, mid, lo = _split3(a)
    return (_dot(hi, m01) + _dot(mid, m01)) + _dot(lo, m01)


def _rope(t, cos, sin_a, sin_b, half):
    return (t * cos + pltpu.roll(t, LANES - half, 1) * sin_a + pltpu.roll(t, half, 1) * sin_b)


A_CQ = 0
A_CKV = A_CQ + MLA_Q_LORA
A_DQ = A_CKV + MLA_KV_LORA
A_DK = A_DQ + DIFF_HEADS * 2 * DIFF_HD
A_KR = A_DK + DIFF_HEADS * 2 * DIFF_HD
A_IN_PACKED = A_KR + LANES
INPROJ_SUB = 256


def _attn_inproj_kernel(x_ref, nw_ref, win_ref, wdvt_ref, qn_ref, wuq_ref, kvn_ref, wuk_ref, wuvt_ref,
                        cm_ref, sam_ref, sbm_ref, cd_ref, sad_ref, sbd_ref,
                        q_ref, k_ref, vt_ref, dq_ref, dk_ref, dvt_ref):
    for r0 in range(0, x_ref.shape[1], INPROJ_SUB):
        rs = slice(r0, r0 + INPROJ_SUB)
        hn = _rms(x_ref[0, rs], nw_ref[...]).astype(BF16)
        proj = _dot(hn, win_ref[...])
        cq = _rms(proj[:, A_CQ:A_CKV], qn_ref[...]).astype(BF16)
        ckv = _rms(proj[:, A_CKV:A_DQ], kvn_ref[...]).astype(BF16)
        q = _dot(cq, wuq_ref[...])
        kn = _dot(ckv, wuk_ref[...])
        vt = _dot_nt(wuvt_ref[...], ckv).astype(BF16)
        dvt = _dot_nt(wdvt_ref[...], hn).astype(BF16)
        ones = jnp.ones((VT_ONES, INPROJ_SUB), BF16)
        for g in range(4):
            vt_ref[0, g, 0:LANES, rs] = vt[g * LANES:(g + 1) * LANES]
            vt_ref[0, g, LANES:VT_ROWS, rs] = ones
            dvt_ref[0, g, 0:LANES, rs] = dvt[g * LANES:(g + 1) * LANES]
            dvt_ref[0, g, LANES:VT_ROWS, rs] = ones
        cm, sam, sbm = cm_ref[rs], sam_ref[rs], sbm_ref[rs]
        kpe = _rope(proj[:, A_KR:A_KR + LANES], cm, sam, sbm, MLA_ROPE // 2)
        for h in range(MLA_HEADS):
            sl = slice(h * LANES, (h + 1) * LANES)
            q_ref[0, h, rs] = _rope(q[:, sl], cm, sam, sbm, MLA_ROPE // 2).astype(BF16)
            k_ref[0, h, rs] = (kn[:, sl] + kpe).astype(BF16)
        cd, sad, sbd = cd_ref[rs], sad_ref[rs], sbd_ref[rs]
        for g in range(DIFF_HEADS):
            sl = slice(g * LANES, (g + 1) * LANES)
            dq = proj[:, A_DQ + g * LANES:A_DQ + (g + 1) * LANES]
            dk = proj[:, A_DK + g * LANES:A_DK + (g + 1) * LANES]
            dq_ref[0, rs, sl] = (_rope(dq, cd, sad, sbd, DIFF_HD // 2) * (DIFF_HD ** -0.5 * LOG2E)).astype(BF16)
            dk_ref[0, rs, sl] = _rope(dk, cd, sad, sbd, DIFF_HD // 2).astype(BF16)


def _rope_tables(seq_len, dim, lane_offsets):
    half = dim // 2
    inv_freq = 1.0 / (ROPE_THETA ** (jnp.arange(0, dim, 2, dtype=F32) / dim))
    ang = jnp.arange(seq_len, dtype=F32)[:, None] * inv_freq[None, :]
    cos, sin = jnp.cos(ang), jnp.sin(ang)
    zero = jnp.zeros_like(sin)
    c, sa, sb = [], [], []
    lane = 0
    for off in lane_offsets:
        gap = off - lane
        c += [jnp.ones((seq_len, gap), F32), cos, cos]
        sa += [jnp.zeros((seq_len, gap), F32), -sin, zero]
        sb += [jnp.zeros((seq_len, gap), F32), zero, sin]
        lane = off + dim
    c.append(jnp.ones((seq_len, LANES - lane), F32))
    sa.append(jnp.zeros((seq_len, LANES - lane), F32))
    sb.append(jnp.zeros((seq_len, LANES - lane), F32))
    return tuple(jnp.concatenate(t, axis=-1) for t in (c, sa, sb))


def _const_spec(shape):
    nd = len(shape)
    return pl.BlockSpec(shape, lambda *_: (0,) * nd, pipeline_mode=pl.Buffered(1))


def _attn_inproj(x, norm_w, w_in, q_norm, w_uq, kv_norm, w_ukv, ts):
    bsz, s_len, _ = x.shape
    cq, ckv, kr, dq, dk, dv = jnp.split(
        w_in, [int(v) for v in np.cumsum([MLA_Q_LORA, MLA_KV_LORA, MLA_ROPE, 512, 512])], axis=-1)
    kr_pad = jnp.pad(kr, ((0, 0), (MLA_NOPE, LANES - MLA_NOPE - MLA_ROPE)))
    win_p = jnp.concatenate([cq, ckv, dq, dk, kr_pad], axis=-1).astype(BF16)
    wdvt = dv.T.astype(BF16)
    scale = (MLA_NOPE + MLA_ROPE) ** -0.5 * LOG2E
    wuq_p = jnp.pad((w_uq * scale).reshape(MLA_Q_LORA, MLA_HEADS, MLA_NOPE + MLA_ROPE),
                    ((0, 0), (0, 0), (0, LANES - MLA_NOPE - MLA_ROPE)))
    wuq_p = wuq_p.reshape(MLA_Q_LORA, MLA_HEADS * LANES).astype(BF16)
    wkv = w_ukv.reshape(MLA_KV_LORA, MLA_HEADS, MLA_NOPE + MLA_V)
    wuk_p = jnp.pad(wkv[..., :MLA_NOPE], ((0, 0), (0, 0), (0, LANES - MLA_NOPE)))
    wuk_p = wuk_p.reshape(MLA_KV_LORA, MLA_HEADS * LANES).astype(BF16)
    wuvt = wkv[..., MLA_NOPE:].reshape(MLA_KV_LORA, MLA_HEADS * MLA_V).T.astype(BF16)
    tabs_m = _rope_tables(s_len, MLA_ROPE, (MLA_NOPE,))
    tabs_d = _rope_tables(s_len, DIFF_HD, (0, DIFF_HD))

    row = lambda b, i: (b, i, 0)
    tab = pl.BlockSpec((ts, LANES), lambda b, i: (i, 0))
    head_major = pl.BlockSpec((1, MLA_HEADS, ts, LANES), lambda b, i: (b, 0, i, 0))
    wide = pl.BlockSpec((1, ts, 512), row)
    transposed = pl.BlockSpec((1, 4, VT_ROWS, ts), lambda b, i: (b, 0, 0, i))
    return pl.pallas_call(
        _attn_inproj_kernel,
        grid=(bsz, s_len // ts),
        in_specs=[pl.BlockSpec((1, ts, D_MODEL), row), _const_spec((1, D_MODEL)),
                  _const_spec(win_p.shape), _const_spec(wdvt.shape), _const_spec((1, MLA_Q_LORA)),
                  _const_spec(wuq_p.shape), _const_spec((1, MLA_KV_LORA)), _const_spec(wuk_p.shape),
                  _const_spec(wuvt.shape), tab, tab, tab, tab, tab, tab],
        out_specs=[head_major, head_major, transposed, wide, wide, transposed],
        out_shape=[jax.ShapeDtypeStruct((bsz, MLA_HEADS, s_len, LANES), BF16),
                   jax.ShapeDtypeStruct((bsz, MLA_HEADS, s_len, LANES), BF16),
                   jax.ShapeDtypeStruct((bsz, 4, VT_ROWS, s_len), BF16),
                   jax.ShapeDtypeStruct((bsz, s_len, 512), BF16),
                   jax.ShapeDtypeStruct((bsz, s_len, 512), BF16),
                   jax.ShapeDtypeStruct((bsz, 4, VT_ROWS, s_len), BF16)],
        compiler_params=pltpu.CompilerParams(dimension_semantics=("parallel", "parallel"),
                                             vmem_limit_bytes=VMEM_LIMIT),
        name="attn_inproj",
    )(x, norm_w.reshape(1, -1), win_p, wdvt, q_norm.reshape(1, -1), wuq_p, kv_norm.reshape(1, -1), wuk_p, wuvt,
      *tabs_m, *tabs_d)


ATTN_TQ = 512
ATTN_GROUP = 256
ATTN_UNROLL = 4
VT_ONES = 16
VT_ROWS = LANES + VT_ONES
LOG2E = math.log2(math.e)


def _attn_sweep(i, n_q, tq, load_queries, k_rows, vt_ref, qs_sc, sa_sc, sb_sc, m_sc, acc_sc):
    nq = sa_sc.shape[1]
    groups = [slice(c, c + ATTN_GROUP) for c in range(0, nq, ATTN_GROUP)]

    def start_of(blk):
        return pl.multiple_of(blk * tq, tq)

    def fill_scores(buf, start, cols):
        buf[:, cols] = _dot_nt(k_rows(start, cols), qs_sc[cols])

    def process(s_ref, blk, cols, diagonal=False):
        rows = (cols.start % tq) + ATTN_GROUP if diagonal else tq
        s = s_ref[0:rows, cols]
        if diagonal:
            key = lax.broadcasted_iota(jnp.int32, s.shape, 0)
            qry = (lax.broadcasted_iota(jnp.int32, s.shape, 1) + cols.start) & (tq - 1)
            s = jnp.where(key <= qry, s, -jnp.inf)
        m_prev = m_sc[:, cols]
        m_new = jnp.maximum(m_prev, jnp.max(s, axis=0, keepdims=True))
        alpha = jnp.exp2(m_prev - m_new)
        p = jnp.exp2((s - m_new).astype(BF16))
        acc_sc[:, cols] = alpha * acc_sc[:, cols] + _dot(vt_ref[0, 0, :, pl.ds(start_of(blk), rows)], p)
        m_sc[:, cols] = m_new

    @pl.when(i == 0)
    def _():
        load_queries(start_of(0))
        for cols in groups:
            fill_scores(sa_sc, start_of(0), cols)

    m_sc[...] = jnp.full(m_sc.shape, -jnp.inf, F32)
    acc_sc[...] = jnp.zeros(acc_sc.shape, F32)

    bufs = (sa_sc, sb_sc)

    def run(first, n_full, then_diagonal):
        for t in range(n_full):
            for cols in groups:
                fill_scores(bufs[(t + 1) % 2], start_of(first + t + 1), cols)
                process(bufs[t % 2], first + t, cols)
        if then_diagonal:
            for cols in groups:
                process(bufs[n_full % 2], first + n_full, cols, diagonal=True)

    def body(jj, carry):
        run(ATTN_UNROLL * jj, ATTN_UNROLL, False)
        return carry

    n_main = i // ATTN_UNROLL
    lax.fori_loop(0, n_main, body, 0)
    for rest in range(ATTN_UNROLL):
        @pl.when(i % ATTN_UNROLL == rest)
        def _(rest=rest):
            run(i - rest, rest, True)

    load_queries(start_of(jnp.minimum(i + 1, n_q - 1)))
    for cols in groups:
        fill_scores(sa_sc, start_of(0), cols)
    return acc_sc[LANES:LANES + 1, :]


def _attn_scratch(tq):
    nq = 2 * tq
    return [pltpu.VMEM((nq, LANES), BF16), pltpu.VMEM((tq, nq), F32), pltpu.VMEM((tq, nq), F32),
            pltpu.VMEM((1, nq), F32), pltpu.VMEM((VT_ROWS, nq), F32)]


def _mla_attn_kernel(q_ref, k_ref, vt_ref, o_ref, qs_sc, sa_sc, sb_sc, m_sc, acc_sc, *, tq, n_q):
    def load_queries(start):
        for h in range(2):
            qs_sc[h * tq:(h + 1) * tq] = q_ref[0, h, pl.ds(start, tq), :]

    def k_rows(start, cols):
        return k_ref[0, cols.start // tq, pl.ds(start, tq), :]

    l = _attn_sweep(pl.program_id(2), n_q, tq, load_queries, k_rows, vt_ref, qs_sc, sa_sc, sb_sc, m_sc, acc_sc)
    o_t = jnp.concatenate([acc_sc[0:MLA_V, 0:tq] / l[:, 0:tq],
                           acc_sc[MLA_V:2 * MLA_V, tq:2 * tq] / l[:, tq:2 * tq]], axis=0)
    o_ref[0] = o_t.T.astype(BF16)


def _mla_attention(q, k, vt, tq):
    bsz, nh, s_len, _ = q.shape
    return pl.pallas_call(
        functools.partial(_mla_attn_kernel, tq=tq, n_q=s_len // tq),
        grid=(bsz, nh // 2, s_len // tq),
        in_specs=[pl.BlockSpec((1, 2, s_len, LANES), lambda b, p, i: (b, p, 0, 0)),
                  pl.BlockSpec((1, 2, s_len, LANES), lambda b, p, i: (b, p, 0, 0)),
                  pl.BlockSpec((1, 1, VT_ROWS, s_len), lambda b, p, i: (b, p, 0, 0))],
        out_specs=pl.BlockSpec((1, tq, LANES), lambda b, p, i: (b, i, p)),
        out_shape=jax.ShapeDtypeStruct((bsz, s_len, nh * MLA_V), BF16),
        scratch_shapes=_attn_scratch(tq),
        compiler_params=pltpu.CompilerParams(dimension_semantics=("parallel", "parallel", "arbitrary"),
                                             vmem_limit_bytes=VMEM_LIMIT),
        name="mla_attention",
    )(q, k, vt)


def _diff_attn_kernel(q_ref, k_ref, vt_ref, lq1_ref, lk1_ref, lq2_ref, lk2_ref, sub_ref, o_ref,
                      qs_sc, sa_sc, sb_sc, m_sc, acc_sc, *, tq, n_q, lambda_init):
    def load_queries(start):
        q = q_ref[0, pl.ds(start, tq), :]
        lane = lax.broadcasted_iota(jnp.int32, q.shape, 1)
        zero = jnp.zeros_like(q)
        qs_sc[0:tq] = jnp.where(lane < DIFF_HD, q, zero)
        qs_sc[tq:2 * tq] = jnp.where(lane < DIFF_HD, zero, q)

    def k_rows(start, cols):
        return k_ref[0, pl.ds(start, tq), :]

    l = _attn_sweep(pl.program_id(2), n_q, tq, load_queries, k_rows, vt_ref, qs_sc, sa_sc, sb_sc, m_sc, acc_sc)
    o_t = acc_sc[0:LANES, :] / l
    lam = (jnp.exp(jnp.sum(lq1_ref[...] * lk1_ref[...], axis=-1, keepdims=True))
           - jnp.exp(jnp.sum(lq2_ref[...] * lk2_ref[...], axis=-1, keepdims=True)) + lambda_init)
    od = (o_t[:, 0:tq] - lam * o_t[:, tq:2 * tq]).T
    o_ref[0] = (_rms(od, sub_ref[...]) * (1.0 - lambda_init)).astype(BF16)


def _diff_attention(dq, dk, dvt, lq1, lk1, lq2, lk2, subln, lambda_init, tq):
    bsz, s_len, _ = dq.shape
    qspec = pl.BlockSpec((1, tq, LANES), lambda b, h, i: (b, i, h))
    seq = pl.BlockSpec((1, s_len, LANES), lambda b, h, i: (b, 0, h))
    vec = lambda n: pl.BlockSpec((1, n), lambda b, h, i: (0, 0))
    return pl.pallas_call(
        functools.partial(_diff_attn_kernel, tq=tq, n_q=s_len // tq, lambda_init=lambda_init),
        grid=(bsz, DIFF_HEADS, s_len // tq),
        in_specs=[seq, seq,
                  pl.BlockSpec((1, 1, VT_ROWS, s_len), lambda b, h, i: (b, h, 0, 0)),
                  vec(DIFF_HD), vec(DIFF_HD), vec(DIFF_HD), vec(DIFF_HD), vec(DIFF_V)],
        out_specs=qspec,
        out_shape=jax.ShapeDtypeStruct((bsz, s_len, DIFF_HEADS * DIFF_V), BF16),
        scratch_shapes=_attn_scratch(tq),
        compiler_params=pltpu.CompilerParams(dimension_semantics=("parallel", "parallel", "arbitrary"),
                                             vmem_limit_bytes=VMEM_LIMIT),
        name="diff_attention",
    )(dq, dk, dvt, lq1.reshape(1, -1), lk1.reshape(1, -1), lq2.reshape(1, -1), lk2.reshape(1, -1),
      subln.reshape(1, -1))


def _outproj_ffn_kernel(x_ref, ma_ref, mb_ref, woa_ref, wob_ref, nf_ref, wg_ref, wu_ref, wd_ref, fin_ref,
                        o_ref, *, final_norm):
    x1 = x_ref[...] + _dot(ma_ref[...], woa_ref[...]) + _dot(mb_ref[...], wob_ref[...])
    h = _rms(x1, nf_ref[...]).astype(BF16)
    g = _dot(h, wg_ref[...])
    u = _dot(h, wu_ref[...])
    a = (_silu(g) * u).astype(BF16)
    x2 = x1 + _dot(a, wd_ref[...])
    if final_norm:
        x2 = _rms(x2, fin_ref[...])
    o_ref[...] = x2


def _outproj_ffn(x2d, mix_a, mix_b, w_out, norm_ffn, layer, w_gate, w_up, w_down, norm_final, final_norm, tm):
    n_tok = x2d.shape[0]
    half = mix_a.shape[1]
    d_ff = w_gate.shape[2]
    woa = w_out[:half].astype(BF16)
    wob = w_out[half:].astype(BF16)
    row = lambda i: (i, 0)
    layer_spec = lambda r, c: pl.BlockSpec((None, r, c), lambda i: (layer, 0, 0), pipeline_mode=pl.Buffered(1))
    return pl.pallas_call(
        functools.partial(_outproj_ffn_kernel, final_norm=final_norm),
        grid=(n_tok // tm,),
        in_specs=[pl.BlockSpec((tm, D_MODEL), row), pl.BlockSpec((tm, half), row), pl.BlockSpec((tm, half), row),
                  _const_spec(woa.shape), _const_spec(wob.shape), _const_spec((1, D_MODEL)),
                  layer_spec(D_MODEL, d_ff), layer_spec(D_MODEL, d_ff), layer_spec(d_ff, D_MODEL),
                  _const_spec((1, D_MODEL))],
        out_specs=pl.BlockSpec((tm, D_MODEL), row),
        out_shape=jax.ShapeDtypeStruct((n_tok, D_MODEL), F32),
        compiler_params=pltpu.CompilerParams(dimension_semantics=("parallel",), vmem_limit_bytes=VMEM_LIMIT),
        name="outproj_ffn",
    )(x2d, mix_a, mix_b, woa, wob, norm_ffn.reshape(1, -1), w_gate, w_up, w_down, norm_final.reshape(1, -1))


S_Z = 0
S_XBC = S_Z + SSM_INNER
S_HQ = S_XBC + SSM_CONV_DIM
S_HF = S_HQ + HG_KDIM_TOTAL
S_HI = S_HF + HG_KDIM_TOTAL
S_HG = S_HI + HG_WIDTH
S_DT = S_HG + HG_WIDTH
S_IN_PACKED = S_DT + LANES


def _rec_inproj_kernel(x_ref, nw_ref, win_ref, z_ref, xbc_ref, hq_ref, hf_ref, hi_ref, hg_ref, dt_ref):
    hn = _rms(x_ref[...], nw_ref[...]).astype(BF16)
    proj = _dot(hn, win_ref[...])
    z_ref[...] = proj[:, S_Z:S_XBC].astype(BF16)
    xbc_ref[...] = proj[:, S_XBC:S_HQ]
    hq_ref[...] = proj[:, S_HQ:S_HF].astype(BF16)
    hf_ref[...] = proj[:, S_HF:S_HI]
    hi_ref[...] = proj[:, S_HI:S_HG].astype(BF16)
    hg_ref[...] = proj[:, S_HG:S_DT].astype(BF16)
    dt_ref[...] = proj[:, S_DT:S_IN_PACKED]


def _rec_inproj(x2d, norm_w, w_in, tm):
    n_tok = x2d.shape[0]
    z, xbc, dt, hq, hf, hi, hg = jnp.split(
        w_in, [int(v) for v in np.cumsum([SSM_INNER, SSM_CONV_DIM, SSM_HEADS, 512, 512, 512])], axis=-1)
    dt_pad = jnp.pad(dt, ((0, 0), (0, LANES - SSM_HEADS)))
    win_p = jnp.concatenate([z, xbc, hq, hf, hi, hg, dt_pad], axis=-1).astype(BF16)
    row = lambda i: (i, 0)
    widths = [(SSM_INNER, BF16), (SSM_CONV_DIM, F32), (512, BF16), (512, F32), (512, BF16), (512, BF16),
              (LANES, F32)]
    return pl.pallas_call(
        _rec_inproj_kernel,
        grid=(n_tok // tm,),
        in_specs=[pl.BlockSpec((tm, D_MODEL), row), _const_spec((1, D_MODEL)), _const_spec(win_p.shape)],
        out_specs=[pl.BlockSpec((tm, w), row) for w, _ in widths],
        out_shape=[jax.ShapeDtypeStruct((n_tok, w), dt_) for w, dt_ in widths],
        compiler_params=pltpu.CompilerParams(dimension_semantics=("parallel",), vmem_limit_bytes=VMEM_LIMIT),
        name="rec_inproj",
    )(x2d, norm_w.reshape(1, -1), win_p)


CONV_PAD = 8


def _ssd_kernel(xbc_ref, dt_ref, z_ref, cw_ref, cb_ref, dtb_ref, ah_ref, dsk_ref, nw_ref, tri_ref, exp_ref,
                y_ref, xpad_sc, st_sc):
    L = SSM_CHUNK
    heads_per_group = SSM_HEADS // SSM_GROUPS
    gw = heads_per_group * SSM_HEADDIM

    @pl.when(pl.program_id(0) == 0)
    def _():
        xpad_sc[:, 0:CONV_PAD] = jnp.zeros((xpad_sc.shape[0], CONV_PAD, SSM_CONV_DIM), F32)
        st_sc[...] = jnp.zeros(st_sc.shape, F32)

    r = lax.broadcasted_iota(jnp.int32, (L, L), 0)
    c = lax.broadcasted_iota(jnp.int32, (L, L), 1)
    causal = c <= r
    lane = lax.broadcasted_iota(jnp.int32, (L, LANES), 1)
    nw = nw_ref[...]

    for b in range(xbc_ref.shape[0]):
        xt = xbc_ref[b]
        xpad_sc[b, CONV_PAD:CONV_PAD + L] = xt
        conv = cb_ref[...] + cw_ref[SSM_CONV - 1:SSM_CONV] * xt
        for d in range(1, SSM_CONV):
            conv = conv + cw_ref[SSM_CONV - 1 - d:SSM_CONV - d] * xpad_sc[b, CONV_PAD - d:CONV_PAD - d + L]
        xpad_sc[b, 0:CONV_PAD] = xt[L - CONV_PAD:L]
        xc = _silu(conv)
        xs = xc[:, 0:SSM_INNER]
        b_in = xc[:, SSM_INNER:SSM_INNER + SSM_GROUPS * SSM_STATE].astype(BF16)
        c_in = xc[:, SSM_INNER + SSM_GROUPS * SSM_STATE:].astype(BF16)

        dt = jax.nn.softplus(dt_ref[b] + dtb_ref[...])
        a = dt * ah_ref[...]
        a_cs = _dot01_left(tri_ref[...], a)
        a_cs_t = a_cs.T
        dt_e = _dot01_right(dt, exp_ref[...])
        acs_e = _dot01_right(a_cs, exp_ref[...])
        alast_e = acs_e[L - 1:L, :]
        xdt = xs * dt_e
        xdec = (xdt * jnp.exp(alast_e - acs_e)).astype(BF16)
        xdt_b = xdt.astype(BF16)
        eacs = jnp.exp(acs_e)

        ys = []
        for g in range(SSM_GROUPS):
            gs = slice(g * gw, (g + 1) * gw)
            bg = b_in[:, g * SSM_STATE:(g + 1) * SSM_STATE]
            cg = c_in[:, g * SSM_STATE:(g + 1) * SSM_STATE]
            cb = _dot_nt(cg, bg)
            st_prev = st_sc[b * SSM_GROUPS + g]
            y_off = _dot(cg, st_prev.astype(BF16)) * eacs[:, gs]
            st_sc[b * SSM_GROUPS + g] = st_prev * jnp.exp(alast_e[:, gs]) + _dot_tn(bg, xdec[:, gs])
            for pr in range(heads_per_group // 2):
                xpair = xdt_b[:, g * gw + pr * LANES:g * gw + (pr + 1) * LANES]
                res = []
                for hh in range(2):
                    h = g * heads_per_group + 2 * pr + hh
                    seg = jnp.exp(jnp.minimum(a_cs[:, h:h + 1] - a_cs_t[h:h + 1, :], 0.0))
                    m = jnp.where(causal, cb * seg, 0.0).astype(BF16)
                    res.append(_dot(m, xpair))
                ys.append(jnp.where(lane < SSM_HEADDIM, res[0], res[1]) + y_off[:, pr * LANES:(pr + 1) * LANES])
        y = jnp.concatenate(ys, axis=1) + dsk_ref[...] * xs
        y = y * _silu(z_ref[b].astype(F32))
        for g in range(SSM_GROUPS):
            sl = slice(g * gw, (g + 1) * gw)
            y_ref[b, :, sl] = _rms(y[:, sl], nw[:, sl]).astype(BF16)


def _ssd(xbc, dt, z, conv_w, conv_b, dt_bias, a_log, d_skip, ssm_norm):
    bsz, s_len, _ = xbc.shape
    L = SSM_CHUNK
    pad = lambda v: jnp.pad(v.astype(F32), (0, LANES - SSM_HEADS)).reshape(1, LANES)
    a_head = -jnp.exp(a_log.astype(F32))
    tri = jnp.asarray(np.tril(np.ones((L, L), np.float32)), BF16)
    expand = np.zeros((LANES, SSM_INNER), np.float32)
    for h in range(SSM_HEADS):
        expand[h, h * SSM_HEADDIM:(h + 1) * SSM_HEADDIM] = 1.0
    expand = jnp.asarray(expand, BF16)
    dsk = jnp.repeat(d_skip.astype(F32), SSM_HEADDIM).reshape(1, SSM_INNER)
    chunk = lambda w: pl.BlockSpec((bsz, L, w), lambda c: (0, c, 0))
    const = lambda shape: pl.BlockSpec(shape, lambda c: (0,) * len(shape))
    return pl.pallas_call(
        _ssd_kernel,
        grid=(s_len // L,),
        in_specs=[chunk(SSM_CONV_DIM), chunk(LANES), chunk(SSM_INNER), const((SSM_CONV, SSM_CONV_DIM)),
                  const((1, SSM_CONV_DIM)), const((1, LANES)), const((1, LANES)), const((1, SSM_INNER)),
                  const((1, SSM_INNER)), const((L, L)), const((LANES, SSM_INNER))],
        out_specs=chunk(SSM_INNER),
        out_shape=jax.ShapeDtypeStruct((bsz, s_len, SSM_INNER), BF16),
        scratch_shapes=[pltpu.VMEM((bsz, CONV_PAD + L, SSM_CONV_DIM), F32),
                        pltpu.VMEM((bsz * SSM_GROUPS, SSM_STATE, SSM_INNER // SSM_GROUPS), F32)],
        compiler_params=pltpu.CompilerParams(dimension_semantics=("arbitrary",), vmem_limit_bytes=VMEM_LIMIT),
        name="ssd_scan",
    )(xbc, dt, z, conv_w, conv_b.reshape(1, -1), pad(dt_bias), pad(a_head), dsk, ssm_norm.reshape(1, -1),
      tri, expand)


HG_LEVELS = int(math.log2(HG_CHUNK))


def _hgrn_tables():
    C = HG_CHUNK
    idx = np.arange(C)
    tri = np.tril(np.ones((C, C), np.float32))
    mats = [tri]
    masks = []
    for lev in range(HG_LEVELS):
        h = C >> (lev + 1)
        mid = (idx // (2 * h)) * (2 * h) + h - 1
        upper = (idx % (2 * h)) >= h
        mats.append(tri[mid])
        same = (idx[:, None] // (2 * h)) == (idx[None, :] // (2 * h))
        masks.append((same & upper[:, None] & (~upper)[None, :]).astype(np.float32))
    masks.append(np.eye(C, dtype=np.float32))
    w = np.concatenate(mats, axis=0)
    return np.concatenate([w, w, w], axis=1), np.stack(masks, axis=0)


def _hgrn_kernel(hq_ref, hf_ref, hi_ref, hg_ref, lb_ref, gn_ref, w3_ref, masks_ref, o_ref, st_sc):
    C = HG_CHUNK

    @pl.when(pl.program_id(0) == 0)
    def _():
        st_sc[...] = jnp.zeros(st_sc.shape, F32)

    lb = lb_ref[...]
    row = lax.broadcasted_iota(jnp.int32, (C, HG_KDIM_TOTAL), 0)
    for b in range(hq_ref.shape[0]):
        xf = hf_ref[b]
        g = jnp.log(lb + (1.0 - lb) * jax.nn.sigmoid(xf))
        kin = (1.0 - lb) * jax.nn.sigmoid(-xf)
        q = _silu(hq_ref[b].astype(F32))
        sums = _dot(w3_ref[...], jnp.concatenate(_split3(g), axis=0))
        gcum = sums[0:C]
        glast = gcum[C - 1:C]
        q_in = (q * jnp.exp(gcum)).astype(BF16)
        k_out = (kin * jnp.exp(glast - gcum)).astype(BF16)
        zs = []
        for lev in range(HG_LEVELS):
            x = gcum - sums[C * (1 + lev):C * (2 + lev)]
            after_mid = (row & (C >> (lev + 1))) != 0
            zs.append((jnp.where(after_mid, q, kin) * jnp.exp(-jnp.abs(x))).astype(BF16))
        qb, kb = q.astype(BF16), kin.astype(BF16)
        v = hi_ref[b]
        gate = _silu(hg_ref[b].astype(F32))
        for h in range(HG_HEADS):
            sl = slice(h * HG_EXPAND, (h + 1) * HG_EXPAND)
            scores = masks_ref[HG_LEVELS] * _dot_nt(qb[:, sl], kb[:, sl])
            for lev in range(HG_LEVELS):
                scores = scores + masks_ref[lev] * _dot_nt(zs[lev][:, sl], zs[lev][:, sl])
            st = st_sc[b * HG_HEADS + h]
            vh = v[:, sl]
            o = _dot(scores.astype(BF16), vh) + _dot_nt(q_in[:, sl], st.astype(BF16))
            st_sc[b * HG_HEADS + h] = st * jnp.exp(glast[:, sl]) + _dot_tn(vh, k_out[:, sl])
            o_ref[b, :, sl] = (_rms(o, gn_ref[...]) * gate[:, sl]).astype(BF16)


def _hgrn2(hq, hf, hi, hg, lb, g_norm):
    bsz, s_len, _ = hq.shape
    C = HG_CHUNK
    w3, masks = _hgrn_tables()
    w3 = jnp.asarray(w3, BF16)
    masks = jnp.asarray(masks, F32)
    chunk = pl.BlockSpec((bsz, C, HG_WIDTH), lambda c: (0, c, 0))
    const = lambda shape: pl.BlockSpec(shape, lambda c: (0,) * len(shape))
    return pl.pallas_call(
        _hgrn_kernel,
        grid=(s_len // C,),
        in_specs=[chunk, chunk, chunk, chunk, const((1, HG_KDIM_TOTAL)), const((1, HG_VDIM)),
                  const(w3.shape), const(masks.shape)],
        out_specs=chunk,
        out_shape=jax.ShapeDtypeStruct((bsz, s_len, HG_WIDTH), BF16),
        scratch_shapes=[pltpu.VMEM((bsz * HG_HEADS, HG_VDIM, HG_EXPAND), F32)],
        compiler_params=pltpu.CompilerParams(dimension_semantics=("arbitrary",), vmem_limit_bytes=VMEM_LIMIT),
        name="hgrn2_scan",
    )(hq, hf, hi, hg, lb.reshape(1, -1), g_norm.reshape(1, -1), w3, masks)


def kernel(x, norm_mix, norm_ffn, norm_final, a_w_in, a_q_norm, a_w_uq, a_kv_norm, a_w_ukv, a_lq1, a_lk1, a_lq2, a_lk2, a_subln, a_w_out, s_w_in, s_conv_w, s_conv_b, s_dt_bias, s_a_log, s_d, s_norm, h_g_norm, h_lower_bound, s_w_out, ffn_gate, ffn_up, ffn_down):
    bsz, s_len, _ = x.shape
    n_tok = bsz * s_len
    depth = norm_mix.shape[0]
    assert depth == 2 and s_len % ATTN_TQ == 0
    p_lb = jax.nn.softmax(h_lower_bound.astype(F32), axis=0)
    lb_all = jnp.cumsum(p_lb, axis=0) - p_lb[0:1]
    w_gate, w_up, w_down = ffn_gate.astype(BF16), ffn_up.astype(BF16), ffn_down.astype(BF16)

    lambda_init = 0.8 - 0.6 * math.exp(-0.3 * 0)
    q, k, vt, dq, dk, dvt = _attn_inproj(x, norm_mix[0], a_w_in[0], a_q_norm[0], a_w_uq[0], a_kv_norm[0],
                                       a_w_ukv[0], ts=2 * INPROJ_SUB)
    o_mla = _mla_attention(q, k, vt, tq=ATTN_TQ)
    o_diff = _diff_attention(dq, dk, dvt, a_lq1[0], a_lk1[0], a_lq2[0], a_lk2[0], a_subln[0], lambda_init,
                             tq=ATTN_TQ)
    x2d = _outproj_ffn(x.reshape(n_tok, D_MODEL), o_mla.reshape(n_tok, -1), o_diff.reshape(n_tok, -1),
                       a_w_out[0], norm_ffn[0], 0, w_gate, w_up, w_down, norm_final, final_norm=False, tm=256)

    z, xbc, hq, hf, hi, hg, dt = _rec_inproj(x2d, norm_mix[1], s_w_in[0], tm=256)
    seq = lambda t: t.reshape(bsz, s_len, t.shape[-1])
    y = _ssd(seq(xbc), seq(dt), seq(z), s_conv_w[0], s_conv_b[0], s_dt_bias[0], s_a_log[0], s_d[0], s_norm[0])
    o = _hgrn2(seq(hq), seq(hf), seq(hi), seq(hg), lb_all[1], h_g_norm[0])
    x2d = _outproj_ffn(x2d, y.reshape(n_tok, -1), o.reshape(n_tok, -1), s_w_out[0], norm_ffn[1], 1, w_gate, w_up,
                       w_down, norm_final, final_norm=True, tm=256)
    return x2d.reshape(bsz, s_len, D_MODEL)
```

```python
import functools
import math

import numpy as np
import jax
import jax.numpy as jnp
from jax import lax
from jax.experimental import pallas as pl
from jax.experimental.pallas import tpu as pltpu

F32 = jnp.float32
BF16 = jnp.bfloat16

D_MODEL = 1024
EPS = 1e-6
ROPE_THETA = 10000.0

MLA_HEADS = 8
MLA_Q_LORA = 384
MLA_KV_LORA = 256
MLA_NOPE = 64
MLA_ROPE = 32
MLA_V = 64
DIFF_HEADS = 4
DIFF_HD = 64
DIFF_V = 2 * DIFF_HD

SSM_HEADS = 8
SSM_HEADDIM = 64
SSM_INNER = SSM_HEADS * SSM_HEADDIM
SSM_GROUPS = 2
SSM_STATE = 128
SSM_CONV = 4
SSM_CHUNK = 128
SSM_CONV_DIM = SSM_INNER + 2 * SSM_GROUPS * SSM_STATE
HG_HEADS = 4
HG_EXPAND = 128
HG_VDIM = 128
HG_KDIM_TOTAL = HG_HEADS * HG_EXPAND
HG_WIDTH = HG_HEADS * HG_VDIM
HG_CHUNK = 64

LANES = 128
VMEM_LIMIT = 52 * 1024 * 1024

NT_DIMS = (((1,), (1,)), ((), ()))
TN_DIMS = (((0,), (0,)), ((), ()))


def _dot(a, b):
    return jnp.dot(a, b, preferred_element_type=F32)


def _dot_nt(a, b):
    return lax.dot_general(a, b, NT_DIMS, preferred_element_type=F32)


def _dot_tn(a, b):
    return lax.dot_general(a, b, TN_DIMS, preferred_element_type=F32)


def _rms(x, w):
    return x * lax.rsqrt(jnp.mean(x * x, axis=-1, keepdims=True) + EPS) * w


def _silu(x):
    return x * jax.nn.sigmoid(x)


def _split3(a):
    hi = a.astype(BF16)
    r = a - hi.astype(F32)
    mid = r.astype(BF16)
    lo = (r - mid.astype(F32)).astype(BF16)
    return hi, mid, lo


def _dot01_left(m01, a):
    hi, mid, lo = _split3(a)
    return (_dot(m01, hi) + _dot(m01, mid)) + _dot(m01, lo)


def _dot01_right(a, m01):
    hi, mid, lo = _split3(a)
    return (_dot(hi, m01) + _dot(mid, m01)) + _dot(lo, m01)


def _rope(t, cos, sin_a, sin_b, half):
    return (t * cos + pltpu.roll(t, LANES - half, 1) * sin_a + pltpu.roll(t, half, 1) * sin_b)


A_CQ = 0
A_CKV = A_CQ + MLA_Q_LORA
A_DQ = A_CKV + MLA_KV_LORA
A_DK = A_DQ + DIFF_HEADS * 2 * DIFF_HD
A_KR = A_DK + DIFF_HEADS * 2 * DIFF_HD
A_IN_PACKED = A_KR + LANES
INPROJ_SUB = 256


def _attn_inproj_kernel(x_ref, nw_ref, win_ref, wdvt_ref, qn_ref, wuq_ref, kvn_ref, wuk_ref, wuvt_ref,
                        cm_ref, sam_ref, sbm_ref, cd_ref, sad_ref, sbd_ref,
                        q_ref, k_ref, vt_ref, dq_ref, dk_ref, dvt_ref):
    for r0 in range(0, x_ref.shape[1], INPROJ_SUB):
        rs = slice(r0, r0 + INPROJ_SUB)
        hn = _rms(x_ref[0, rs], nw_ref[...]).astype(BF16)
        proj = _dot(hn, win_ref[...])
        cq = _rms(proj[:, A_CQ:A_CKV], qn_ref[...]).astype(BF16)
        ckv = _rms(proj[:, A_CKV:A_DQ], kvn_ref[...]).astype(BF16)
        q = _dot(cq, wuq_ref[...])
        kn = _dot(ckv, wuk_ref[...])
        vt = _dot_nt(wuvt_ref[...], ckv).astype(BF16)
        dvt = _dot_nt(wdvt_ref[...], hn).astype(BF16)
        ones = jnp.ones((VT_ONES, INPROJ_SUB), BF16)
        for g in range(4):
            vt_ref[0, g, 0:LANES, rs] = vt[g * LANES:(g + 1) * LANES]
            vt_ref[0, g, LANES:VT_ROWS, rs] = ones
            dvt_ref[0, g, 0:LANES, rs] = dvt[g * LANES:(g + 1) * LANES]
            dvt_ref[0, g, LANES:VT_ROWS, rs] = ones
        cm, sam, sbm = cm_ref[rs], sam_ref[rs], sbm_ref[rs]
        kpe = _rope(proj[:, A_KR:A_KR + LANES], cm, sam, sbm, MLA_ROPE // 2)
        for h in range(MLA_HEADS):
            sl = slice(h * LANES, (h + 1) * LANES)
            q_ref[0, h, rs] = _rope(q[:, sl], cm, sam, sbm, MLA_ROPE // 2).astype(BF16)
            k_ref[0, h, rs] = (kn[:, sl] + kpe).astype(BF16)
        cd, sad, sbd = cd_ref[rs], sad_ref[rs], sbd_ref[rs]
        for g in range(DIFF_HEADS):
            sl = slice(g * LANES, (g + 1) * LANES)
            dq = proj[:, A_DQ + g * LANES:A_DQ + (g + 1) * LANES]
            dk = proj[:, A_DK + g * LANES:A_DK + (g + 1) * LANES]
            dq_ref[0, rs, sl] = (_rope(dq, cd, sad, sbd, DIFF_HD // 2) * (DIFF_HD ** -0.5 * LOG2E)).astype(BF16)
            dk_ref[0, rs, sl] = _rope(dk, cd, sad, sbd, DIFF_HD // 2).astype(BF16)


def _rope_tables(seq_len, dim, lane_offsets):
    half = dim // 2
    inv_freq = 1.0 / (ROPE_THETA ** (jnp.arange(0, dim, 2, dtype=F32) / dim))
    ang = jnp.arange(seq_len, dtype=F32)[:, None] * inv_freq[None, :]
    cos, sin = jnp.cos(ang), jnp.sin(ang)
    zero = jnp.zeros_like(sin)
    c, sa, sb = [], [], []
    lane = 0
    for off in lane_offsets:
        gap = off - lane
        c += [jnp.ones((seq_len, gap), F32), cos, cos]
        sa += [jnp.zeros((seq_len, gap), F32), -sin, zero]
        sb += [jnp.zeros((seq_len, gap), F32), zero, sin]
        lane = off + dim
    c.append(jnp.ones((seq_len, LANES - lane), F32))
    sa.append(jnp.zeros((seq_len, LANES - lane), F32))
    sb.append(jnp.zeros((seq_len, LANES - lane), F32))
    return tuple(jnp.concatenate(t, axis=-1) for t in (c, sa, sb))


def _const_spec(shape):
    nd = len(shape)
    return pl.BlockSpec(shape, lambda *_: (0,) * nd, pipeline_mode=pl.Buffered(1))


def _attn_inproj(x, norm_w, w_in, q_norm, w_uq, kv_norm, w_ukv, ts):
    bsz, s_len, _ = x.shape
    cq, ckv, kr, dq, dk, dv = jnp.split(
        w_in, [int(v) for v in np.cumsum([MLA_Q_LORA, MLA_KV_LORA, MLA_ROPE, 512, 512])], axis=-1)
    kr_pad = jnp.pad(kr, ((0, 0), (MLA_NOPE, LANES - MLA_NOPE - MLA_ROPE)))
    win_p = jnp.concatenate([cq, ckv, dq, dk, kr_pad], axis=-1).astype(BF16)
    wdvt = dv.T.astype(BF16)
    scale = (MLA_NOPE + MLA_ROPE) ** -0.5 * LOG2E
    wuq_p = jnp.pad((w_uq * scale).reshape(MLA_Q_LORA, MLA_HEADS, MLA_NOPE + MLA_ROPE),
                    ((0, 0), (0, 0), (0, LANES - MLA_NOPE - MLA_ROPE)))
    wuq_p = wuq_p.reshape(MLA_Q_LORA, MLA_HEADS * LANES).astype(BF16)
    wkv = w_ukv.reshape(MLA_KV_LORA, MLA_HEADS, MLA_NOPE + MLA_V)
    wuk_p = jnp.pad(wkv[..., :MLA_NOPE], ((0, 0), (0, 0), (0, LANES - MLA_NOPE)))
    wuk_p = wuk_p.reshape(MLA_KV_LORA, MLA_HEADS * LANES).astype(BF16)
    wuvt = wkv[..., MLA_NOPE:].reshape(MLA_KV_LORA, MLA_HEADS * MLA_V).T.astype(BF16)
    tabs_m = _rope_tables(s_len, MLA_ROPE, (MLA_NOPE,))
    tabs_d = _rope_tables(s_len, DIFF_HD, (0, DIFF_HD))

    row = lambda b, i: (b, i, 0)
    tab = pl.BlockSpec((ts, LANES), lambda b, i: (i, 0))
    head_major = pl.BlockSpec((1, MLA_HEADS, ts, LANES), lambda b, i: (b, 0, i, 0))
    wide = pl.BlockSpec((1, ts, 512), row)
    transposed = pl.BlockSpec((1, 4, VT_ROWS, ts), lambda b, i: (b, 0, 0, i))
    return pl.pallas_call(
        _attn_inproj_kernel,
        grid=(bsz, s_len // ts),
        in_specs=[pl.BlockSpec((1, ts, D_MODEL), row), _const_spec((1, D_MODEL)),
                  _const_spec(win_p.shape), _const_spec(wdvt.shape), _const_spec((1, MLA_Q_LORA)),
                  _const_spec(wuq_p.shape), _const_spec((1, MLA_KV_LORA)), _const_spec(wuk_p.shape),
                  _const_spec(wuvt.shape), tab, tab, tab, tab, tab, tab],
        out_specs=[head_major, head_major, transposed, wide, wide, transposed],
        out_shape=[jax.ShapeDtypeStruct((bsz, MLA_HEADS, s_len, LANES), BF16),
                   jax.ShapeDtypeStruct((bsz, MLA_HEADS, s_len, LANES), BF16),
                   jax.ShapeDtypeStruct((bsz, 4, VT_ROWS, s_len), BF16),
                   jax.ShapeDtypeStruct((bsz, s_len, 512), BF16),
                   jax.ShapeDtypeStruct((bsz, s_len, 512), BF16),
                   jax.ShapeDtypeStruct((bsz, 4, VT_ROWS, s_len), BF16)],
        compiler_params=pltpu.CompilerParams(dimension_semantics=("parallel", "parallel"),
                                             vmem_limit_bytes=VMEM_LIMIT),
        name="attn_inproj",
    )(x, norm_w.reshape(1, -1), win_p, wdvt, q_norm.reshape(1, -1), wuq_p, kv_norm.reshape(1, -1), wuk_p, wuvt,
      *tabs_m, *tabs_d)


ATTN_TQ = 512
ATTN_GROUP = 256
ATTN_UNROLL = 4
VT_ONES = 16
VT_ROWS = LANES + VT_ONES
LOG2E = math.log2(math.e)


def _attn_sweep(i, n_q, tq, load_queries, k_rows, vt_ref, qs_sc, sa_sc, sb_sc, m_sc, acc_sc):
    nq = sa_sc.shape[1]
    groups = [slice(c, c + ATTN_GROUP) for c in range(0, nq, ATTN_GROUP)]

    def start_of(blk):
        return pl.multiple_of(blk * tq, tq)

    def fill_scores(buf, start, cols):
        buf[:, cols] = _dot_nt(k_rows(start, cols), qs_sc[cols])

    def process(s_ref, blk, cols, diagonal=False):
        rows = (cols.start % tq) + ATTN_GROUP if diagonal else tq
        s = s_ref[0:rows, cols]
        if diagonal:
            key = lax.broadcasted_iota(jnp.int32, s.shape, 0)
            qry = (lax.broadcasted_iota(jnp.int32, s.shape, 1) + cols.start) & (tq - 1)
            s = jnp.where(key <= qry, s, -jnp.inf)
        m_prev = m_sc[:, cols]
        m_new = jnp.maximum(m_prev, jnp.max(s, axis=0, keepdims=True))
        alpha = jnp.exp2(m_prev - m_new)
        p = jnp.exp2((s - m_new).astype(BF16))
        acc_sc[:, cols] = alpha * acc_sc[:, cols] + _dot(vt_ref[0, 0, :, pl.ds(start_of(blk), rows)], p)
        m_sc[:, cols] = m_new

    @pl.when(i == 0)
    def _():
        load_queries(start_of(0))
        for cols in groups:
            fill_scores(sa_sc, start_of(0), cols)

    m_sc[...] = jnp.full(m_sc.shape, -jnp.inf, F32)
    acc_sc[...] = jnp.zeros(acc_sc.shape, F32)

    bufs = (sa_sc, sb_sc)

    def run(first, n_full, then_diagonal):
        for t in range(n_full):
            for cols in groups:
                fill_scores(bufs[(t + 1) % 2], start_of(first + t + 1), cols)
                process(bufs[t % 2], first + t, cols)
        if then_diagonal:
            for cols in groups:
                process(bufs[n_full % 2], first + n_full, cols, diagonal=True)

    def body(jj, carry):
        run(ATTN_UNROLL * jj, ATTN_UNROLL, False)
        return carry

    n_main = i // ATTN_UNROLL
    lax.fori_loop(0, n_main, body, 0)
    for rest in range(ATTN_UNROLL):
        @pl.when(i % ATTN_UNROLL == rest)
        def _(rest=rest):
            run(i - rest, rest, True)

    load_queries(start_of(jnp.minimum(i + 1, n_q - 1)))
    for cols in groups:
        fill_scores(sa_sc, start_of(0), cols)
    return acc_sc[LANES:LANES + 1, :]


def _attn_scratch(tq):
    nq = 2 * tq
    return [pltpu.VMEM((nq, LANES), BF16), pltpu.VMEM((tq, nq), F32), pltpu.VMEM((tq, nq), F32),
            pltpu.VMEM((1, nq), F32), pltpu.VMEM((VT_ROWS, nq), F32)]


def _mla_attn_kernel(q_ref, k_ref, vt_ref, o_ref, qs_sc, sa_sc, sb_sc, m_sc, acc_sc, *, tq, n_q):
    def load_queries(start):
        for h in range(2):
            qs_sc[h * tq:(h + 1) * tq] = q_ref[0, h, pl.ds(start, tq), :]

    def k_rows(start, cols):
        return k_ref[0, cols.start // tq, pl.ds(start, tq), :]

    l = _attn_sweep(pl.program_id(2), n_q, tq, load_queries, k_rows, vt_ref, qs_sc, sa_sc, sb_sc, m_sc, acc_sc)
    o_t = jnp.concatenate([acc_sc[0:MLA_V, 0:tq] / l[:, 0:tq],
                           acc_sc[MLA_V:2 * MLA_V, tq:2 * tq] / l[:, tq:2 * tq]], axis=0)
    o_ref[0] = o_t.T.astype(BF16)


def _mla_attention(q, k, vt, tq):
    bsz, nh, s_len, _ = q.shape
    return pl.pallas_call(
        functools.partial(_mla_attn_kernel, tq=tq, n_q=s_len // tq),
        grid=(bsz, nh // 2, s_len // tq),
        in_specs=[pl.BlockSpec((1, 2, s_len, LANES), lambda b, p, i: (b, p, 0, 0)),
                  pl.BlockSpec((1, 2, s_len, LANES), lambda b, p, i: (b, p, 0, 0)),
                  pl.BlockSpec((1, 1, VT_ROWS, s_len), lambda b, p, i: (b, p, 0, 0))],
        out_specs=pl.BlockSpec((1, tq, LANES), lambda b, p, i: (b, i, p)),
        out_shape=jax.ShapeDtypeStruct((bsz, s_len, nh * MLA_V), BF16),
        scratch_shapes=_attn_scratch(tq),
        compiler_params=pltpu.CompilerParams(dimension_semantics=("parallel", "parallel", "arbitrary"),
                                             vmem_limit_bytes=VMEM_LIMIT),
        name="mla_attention",
    )(q, k, vt)


def _diff_attn_kernel(q_ref, k_ref, vt_ref, lq1_ref, lk1_ref, lq2_ref, lk2_ref, sub_ref, o_ref,
                      qs_sc, sa_sc, sb_sc, m_sc, acc_sc, *, tq, n_q, lambda_init):
    def load_queries(start):
        q = q_ref[0, pl.ds(start, tq), :]
        lane = lax.broadcasted_iota(jnp.int32, q.shape, 1)
        zero = jnp.zeros_like(q)
        qs_sc[0:tq] = jnp.where(lane < DIFF_HD, q, zero)
        qs_sc[tq:2 * tq] = jnp.where(lane < DIFF_HD, zero, q)

    def k_rows(start, cols):
        return k_ref[0, pl.ds(start, tq), :]

    l = _attn_sweep(pl.program_id(2), n_q, tq, load_queries, k_rows, vt_ref, qs_sc, sa_sc, sb_sc, m_sc, acc_sc)
    o_t = acc_sc[0:LANES, :] / l
    lam = (jnp.exp(jnp.sum(lq1_ref[...] * lk1_ref[...], axis=-1, keepdims=True))
           - jnp.exp(jnp.sum(lq2_ref[...] * lk2_ref[...], axis=-1, keepdims=True)) + lambda_init)
    od = (o_t[:, 0:tq] - lam * o_t[:, tq:2 * tq]).T
    o_ref[0] = (_rms(od, sub_ref[...]) * (1.0 - lambda_init)).astype(BF16)


def _diff_attention(dq, dk, dvt, lq1, lk1, lq2, lk2, subln, lambda_init, tq):
    bsz, s_len, _ = dq.shape
    qspec = pl.BlockSpec((1, tq, LANES), lambda b, h, i: (b, i, h))
    seq = pl.BlockSpec((1, s_len, LANES), lambda b, h, i: (b, 0, h))
    vec = lambda n: pl.BlockSpec((1, n), lambda b, h, i: (0, 0))
    return pl.pallas_call(
        functools.partial(_diff_attn_kernel, tq=tq, n_q=s_len // tq, lambda_init=lambda_init),
        grid=(bsz, DIFF_HEADS, s_len // tq),
        in_specs=[seq, seq,
                  pl.BlockSpec((1, 1, VT_ROWS, s_len), lambda b, h, i: (b, h, 0, 0)),
                  vec(DIFF_HD), vec(DIFF_HD), vec(DIFF_HD), vec(DIFF_HD), vec(DIFF_V)],
        out_specs=qspec,
        out_shape=jax.ShapeDtypeStruct((bsz, s_len, DIFF_HEADS * DIFF_V), BF16),
        scratch_shapes=_attn_scratch(tq),
        compiler_params=pltpu.CompilerParams(dimension_semantics=("parallel", "parallel", "arbitrary"),
                                             vmem_limit_bytes=VMEM_LIMIT),
        name="diff_attention",
    )(dq, dk, dvt, lq1.reshape(1, -1), lk1.reshape(1, -1), lq2.reshape(1, -1), lk2.reshape(1, -1),
      subln.reshape(1, -1))


def _outproj_ffn_kernel(x_ref, ma_ref, mb_ref, woa_ref, wob_ref, nf_ref, wg_ref, wu_ref, wd_ref, fin_ref,
                        o_ref, *, final_norm):
    x1 = x_ref[...] + _dot(ma_ref[...], woa_ref[...]) + _dot(mb_ref[...], wob_ref[...])
    h = _rms(x1, nf_ref[...]).astype(BF16)
    g = _dot(h, wg_ref[...])
    u = _dot(h, wu_ref[...])
    a = (_silu(g) * u).astype(BF16)
    x2 = x1 + _dot(a, wd_ref[...])
    if final_norm:
        x2 = _rms(x2, fin_ref[...])
    o_ref[...] = x2


def _outproj_ffn(x2d, mix_a, mix_b, w_out, norm_ffn, layer, w_gate, w_up, w_down, norm_final, final_norm, tm):
    n_tok = x2d.shape[0]
    half = mix_a.shape[1]
    d_ff = w_gate.shape[2]
    woa = w_out[:half].astype(BF16)
    wob = w_out[half:].astype(BF16)
    row = lambda i: (i, 0)
    layer_spec = lambda r, c: pl.BlockSpec((None, r, c), lambda i: (layer, 0, 0), pipeline_mode=pl.Buffered(1))
    return pl.pallas_call(
        functools.partial(_outproj_ffn_kernel, final_norm=final_norm),
        grid=(n_tok // tm,),
        in_specs=[pl.BlockSpec((tm, D_MODEL), row), pl.BlockSpec((tm, half), row), pl.BlockSpec((tm, half), row),
                  _const_spec(woa.shape), _const_spec(wob.shape), _const_spec((1, D_MODEL)),
                  layer_spec(D_MODEL, d_ff), layer_spec(D_MODEL, d_ff), layer_spec(d_ff, D_MODEL),
                  _const_spec((1, D_MODEL))],
        out_specs=pl.BlockSpec((tm, D_MODEL), row),
        out_shape=jax.ShapeDtypeStruct((n_tok, D_MODEL), F32),
        compiler_params=pltpu.CompilerParams(dimension_semantics=("parallel",), vmem_limit_bytes=VMEM_LIMIT),
        name="outproj_ffn",
    )(x2d, mix_a, mix_b, woa, wob, norm_ffn.reshape(1, -1), w_gate, w_up, w_down, norm_final.reshape(1, -1))


S_Z = 0
S_XBC = S_Z + SSM_INNER
S_HQ = S_XBC + SSM_CONV_DIM
S_HF = S_HQ + HG_KDIM_TOTAL
S_HI = S_HF + HG_KDIM_TOTAL
S_HG = S_HI + HG_WIDTH
S_DT = S_HG + HG_WIDTH
S_IN_PACKED = S_DT + LANES


def _rec_inproj_kernel(x_ref, nw_ref, win_ref, z_ref, xbc_ref, hq_ref, hf_ref, hi_ref, hg_ref, dt_ref):
    hn = _rms(x_ref[...], nw_ref[...]).astype(BF16)
    proj = _dot(hn, win_ref[...])
    z_ref[...] = proj[:, S_Z:S_XBC].astype(BF16)
    xbc_ref[...] = proj[:, S_XBC:S_HQ]
    hq_ref[...] = proj[:, S_HQ:S_HF].astype(BF16)
    hf_ref[...] = proj[:, S_HF:S_HI]
    hi_ref[...] = proj[:, S_HI:S_HG].astype(BF16)
    hg_ref[...] = proj[:, S_HG:S_DT].astype(BF16)
    dt_ref[...] = proj[:, S_DT:S_IN_PACKED]


def _rec_inproj(x2d, norm_w, w_in, tm):
    n_tok = x2d.shape[0]
    z, xbc, dt, hq, hf, hi, hg = jnp.split(
        w_in, [int(v) for v in np.cumsum([SSM_INNER, SSM_CONV_DIM, SSM_HEADS, 512, 512, 512])], axis=-1)
    dt_pad = jnp.pad(dt, ((0, 0), (0, LANES - SSM_HEADS)))
    win_p = jnp.concatenate([z, xbc, hq, hf, hi, hg, dt_pad], axis=-1).astype(BF16)
    row = lambda i: (i, 0)
    widths = [(SSM_INNER, BF16), (SSM_CONV_DIM, F32), (512, BF16), (512, F32), (512, BF16), (512, BF16),
              (LANES, F32)]
    return pl.pallas_call(
        _rec_inproj_kernel,
        grid=(n_tok // tm,),
        in_specs=[pl.BlockSpec((tm, D_MODEL), row), _const_spec((1, D_MODEL)), _const_spec(win_p.shape)],
        out_specs=[pl.BlockSpec((tm, w), row) for w, _ in widths],
        out_shape=[jax.ShapeDtypeStruct((n_tok, w), dt_) for w, dt_ in widths],
        compiler_params=pltpu.CompilerParams(dimension_semantics=("parallel",), vmem_limit_bytes=VMEM_LIMIT),
        name="rec_inproj",
    )(x2d, norm_w.reshape(1, -1), win_p)


CONV_PAD = 8


def _ssd_chunk(xbc_ref, dt_ref, z_ref, cw_ref, cb_ref, dtb_ref, ah_ref, dsk_ref, nw_ref, tri_ref, exp_ref,
               y_ref, xpad_sc, st_sc):
    L = SSM_CHUNK
    heads_per_group = SSM_HEADS // SSM_GROUPS
    gw = heads_per_group * SSM_HEADDIM

    r = lax.broadcasted_iota(jnp.int32, (L, L), 0)
    c = lax.broadcasted_iota(jnp.int32, (L, L), 1)
    causal = c <= r
    lane = lax.broadcasted_iota(jnp.int32, (L, LANES), 1)
    nw = nw_ref[...]

    for b in range(xbc_ref.shape[0]):
        xt = xbc_ref[b]
        xpad_sc[b, CONV_PAD:CONV_PAD + L] = xt
        conv = cb_ref[...] + cw_ref[SSM_CONV - 1:SSM_CONV] * xt
        for d in range(1, SSM_CONV):
            conv = conv + cw_ref[SSM_CONV - 1 - d:SSM_CONV - d] * xpad_sc[b, CONV_PAD - d:CONV_PAD - d + L]
        xpad_sc[b, 0:CONV_PAD] = xt[L - CONV_PAD:L]
        xc = _silu(conv)
        xs = xc[:, 0:SSM_INNER]
        b_in = xc[:, SSM_INNER:SSM_INNER + SSM_GROUPS * SSM_STATE].astype(BF16)
        c_in = xc[:, SSM_INNER + SSM_GROUPS * SSM_STATE:].astype(BF16)

        dt = jax.nn.softplus(dt_ref[b] + dtb_ref[...])
        a = dt * ah_ref[...]
        a_cs = _dot01_left(tri_ref[...], a)
        a_cs_t = a_cs.T
        dt_e = _dot01_right(dt, exp_ref[...])
        acs_e = _dot01_right(a_cs, exp_ref[...])
        alast_e = acs_e[L - 1:L, :]
        xdt = xs * dt_e
        xdec = (xdt * jnp.exp2(alast_e - acs_e)).astype(BF16)
        xdt_b = xdt.astype(BF16)
        eacs = jnp.exp2(acs_e)

        ys = []
        for g in range(SSM_GROUPS):
            gs = slice(g * gw, (g + 1) * gw)
            bg = b_in[:, g * SSM_STATE:(g + 1) * SSM_STATE]
            cg = c_in[:, g * SSM_STATE:(g + 1) * SSM_STATE]
            cb = _dot_nt(cg, bg)
            st_prev = st_sc[b * SSM_GROUPS + g]
            y_off = _dot(cg, st_prev.astype(BF16)) * eacs[:, gs]
            st_sc[b * SSM_GROUPS + g] = st_prev * jnp.exp2(alast_e[:, gs]) + _dot_tn(bg, xdec[:, gs])
            for pr in range(heads_per_group // 2):
                xpair = xdt_b[:, g * gw + pr * LANES:g * gw + (pr + 1) * LANES]
                res = []
                for hh in range(2):
                    h = g * heads_per_group + 2 * pr + hh
                    seg = jnp.exp2(jnp.minimum(a_cs[:, h:h + 1] - a_cs_t[h:h + 1, :], 0.0))
                    m = jnp.where(causal, cb * seg, 0.0).astype(BF16)
                    res.append(_dot(m, xpair))
                ys.append(jnp.where(lane < SSM_HEADDIM, res[0], res[1]) + y_off[:, pr * LANES:(pr + 1) * LANES])
        y = jnp.concatenate(ys, axis=1) + dsk_ref[...] * xs
        y = y * _silu(z_ref[b].astype(F32))
        for g in range(SSM_GROUPS):
            sl = slice(g * gw, (g + 1) * gw)
            y_ref[b, :, sl] = _rms(y[:, sl], nw[:, sl]).astype(BF16)


HG_LEVELS = int(math.log2(HG_CHUNK))


def _hgrn_tables():
    C = HG_CHUNK
    idx = np.arange(C)
    tri = np.tril(np.ones((C, C), np.float32))
    mats = [tri]
    masks = []
    for lev in range(HG_LEVELS):
        h = C >> (lev + 1)
        mid = (idx // (2 * h)) * (2 * h) + h - 1
        upper = (idx % (2 * h)) >= h
        j = idx[None, :]
        after = upper[:, None] & (j > mid[:, None]) & (j <= idx[:, None])
        before = (~upper)[:, None] & (j > idx[:, None]) & (j <= mid[:, None])
        mats.append((after | before).astype(np.float32))
        same = (idx[:, None] // (2 * h)) == (idx[None, :] // (2 * h))
        masks.append((same & upper[:, None] & (~upper)[None, :]).astype(np.float32))
    masks.append(np.eye(C, dtype=np.float32))
    w = np.concatenate(mats, axis=0)
    return np.concatenate([w, w, w], axis=1), np.stack(masks, axis=0)


def _hgrn_chunk(hq_ref, hf_ref, hi_ref, hg_ref, lb_ref, gn_ref, w3_ref, masks_ref, o_ref, st_sc, rows):
    C = HG_CHUNK
    lb = lb_ref[...]
    row = lax.broadcasted_iota(jnp.int32, (C, HG_KDIM_TOTAL), 0)
    for b in range(hq_ref.shape[0]):
        xf = hf_ref[b, rows]
        g = jnp.log2(lb + (1.0 - lb) * jax.nn.sigmoid(xf))
        kin = (1.0 - lb) * jax.nn.sigmoid(-xf)
        q = _silu(hq_ref[b, rows].astype(F32))
        sums = _dot(w3_ref[...], jnp.concatenate(_split3(g), axis=0))
        gcum = sums[0:C]
        glast = gcum[C - 1:C]
        q_in = (q * jnp.exp2(gcum)).astype(BF16)
        k_out = (kin * jnp.exp2(glast - gcum)).astype(BF16)
        zs = []
        for lev in range(HG_LEVELS):
            decay = jnp.exp2(sums[C * (1 + lev):C * (2 + lev)])
            after_mid = (row & (C >> (lev + 1))) != 0
            zs.append((jnp.where(after_mid, q, kin) * decay).astype(BF16))
        qb, kb = q.astype(BF16), kin.astype(BF16)
        v = hi_ref[b, rows]
        gate = _silu(hg_ref[b, rows].astype(F32))
        for h in range(HG_HEADS):
            sl = slice(h * HG_EXPAND, (h + 1) * HG_EXPAND)
            scores = masks_ref[HG_LEVELS] * _dot_nt(qb[:, sl], kb[:, sl])
            for lev in range(HG_LEVELS):
                scores = scores + masks_ref[lev] * _dot_nt(zs[lev][:, sl], zs[lev][:, sl])
            st = st_sc[b * HG_HEADS + h]
            vh = v[:, sl]
            o = _dot(scores.astype(BF16), vh) + _dot_nt(q_in[:, sl], st.astype(BF16))
            st_sc[b * HG_HEADS + h] = st * jnp.exp2(glast[:, sl]) + _dot_tn(vh, k_out[:, sl])
            o_ref[b, rows, sl] = (_rms(o, gn_ref[...]) * gate[:, sl]).astype(BF16)


N_SSD_IN = 11
N_HGRN_IN = 8


def _rec_scan_kernel(*refs):
    ssd_in, hg_in = refs[:N_SSD_IN], refs[N_SSD_IN:N_SSD_IN + N_HGRN_IN]
    y_ref, o_ref, xpad_sc, sst_sc, hst_sc = refs[N_SSD_IN + N_HGRN_IN:]

    @pl.when(pl.program_id(0) == 0)
    def _():
        xpad_sc[:, 0:CONV_PAD] = jnp.zeros((xpad_sc.shape[0], CONV_PAD, SSM_CONV_DIM), F32)
        sst_sc[...] = jnp.zeros(sst_sc.shape, F32)
        hst_sc[...] = jnp.zeros(hst_sc.shape, F32)

    _ssd_chunk(*ssd_in, y_ref, xpad_sc, sst_sc)
    for c0 in range(0, SSM_CHUNK, HG_CHUNK):
        _hgrn_chunk(*hg_in, o_ref, hst_sc, slice(c0, c0 + HG_CHUNK))


def _rec_scan(xbc, dt, z, conv_w, conv_b, dt_bias, a_log, d_skip, ssm_norm, hq, hf, hi, hg, lb, g_norm):
    bsz, s_len, _ = xbc.shape
    L = SSM_CHUNK
    pad = lambda v: jnp.pad(v.astype(F32), (0, LANES - SSM_HEADS)).reshape(1, LANES)
    a_head = -jnp.exp(a_log.astype(F32)) * LOG2E
    tri = jnp.asarray(np.tril(np.ones((L, L), np.float32)), BF16)
    expand = np.zeros((LANES, SSM_INNER), np.float32)
    for h in range(SSM_HEADS):
        expand[h, h * SSM_HEADDIM:(h + 1) * SSM_HEADDIM] = 1.0
    expand = jnp.asarray(expand, BF16)
    dsk = jnp.repeat(d_skip.astype(F32), SSM_HEADDIM).reshape(1, SSM_INNER)
    w3, masks = _hgrn_tables()
    w3 = jnp.asarray(w3, BF16)
    masks = jnp.asarray(masks, F32)
    chunk = lambda w: pl.BlockSpec((bsz, L, w), lambda c: (0, c, 0))
    const = lambda shape: pl.BlockSpec(shape, lambda c: (0,) * len(shape))
    ssd_specs = [chunk(SSM_CONV_DIM), chunk(LANES), chunk(SSM_INNER), const((SSM_CONV, SSM_CONV_DIM)),
                 const((1, SSM_CONV_DIM)), const((1, LANES)), const((1, LANES)), const((1, SSM_INNER)),
                 const((1, SSM_INNER)), const((L, L)), const((LANES, SSM_INNER))]
    hg_specs = [chunk(HG_WIDTH), chunk(HG_WIDTH), chunk(HG_WIDTH), chunk(HG_WIDTH), const((1, HG_KDIM_TOTAL)),
                const((1, HG_VDIM)), const(w3.shape), const(masks.shape)]
    assert len(ssd_specs) == N_SSD_IN and len(hg_specs) == N_HGRN_IN
    return pl.pallas_call(
        _rec_scan_kernel,
        grid=(s_len // L,),
        in_specs=ssd_specs + hg_specs,
        out_specs=[chunk(SSM_INNER), chunk(HG_WIDTH)],
        out_shape=[jax.ShapeDtypeStruct((bsz, s_len, SSM_INNER), BF16),
                   jax.ShapeDtypeStruct((bsz, s_len, HG_WIDTH), BF16)],
        scratch_shapes=[pltpu.VMEM((bsz, CONV_PAD + L, SSM_CONV_DIM), F32),
                        pltpu.VMEM((bsz * SSM_GROUPS, SSM_STATE, SSM_INNER // SSM_GROUPS), F32),
                        pltpu.VMEM((bsz * HG_HEADS, HG_VDIM, HG_EXPAND), F32)],
        compiler_params=pltpu.CompilerParams(dimension_semantics=("arbitrary",), vmem_limit_bytes=VMEM_LIMIT),
        name="rec_scan",
    )(xbc, dt, z, conv_w, conv_b.reshape(1, -1), pad(dt_bias), pad(a_head), dsk, ssm_norm.reshape(1, -1),
      tri, expand, hq, hf, hi, hg, lb.reshape(1, -1), g_norm.reshape(1, -1), w3, masks)


def kernel(x, norm_mix, norm_ffn, norm_final, a_w_in, a_q_norm, a_w_uq, a_kv_norm, a_w_ukv, a_lq1, a_lk1, a_lq2, a_lk2, a_subln, a_w_out, s_w_in, s_conv_w, s_conv_b, s_dt_bias, s_a_log, s_d, s_norm, h_g_norm, h_lower_bound, s_w_out, ffn_gate, ffn_up, ffn_down):
    bsz, s_len, _ = x.shape
    n_tok = bsz * s_len
    depth = norm_mix.shape[0]
    assert depth == 2 and s_len % ATTN_TQ == 0
    p_lb = jax.nn.softmax(h_lower_bound.astype(F32), axis=0)
    lb_all = jnp.cumsum(p_lb, axis=0) - p_lb[0:1]
    w_gate, w_up, w_down = ffn_gate.astype(BF16), ffn_up.astype(BF16), ffn_down.astype(BF16)

    lambda_init = 0.8 - 0.6 * math.exp(-0.3 * 0)
    q, k, vt, dq, dk, dvt = _attn_inproj(x, norm_mix[0], a_w_in[0], a_q_norm[0], a_w_uq[0], a_kv_norm[0],
                                       a_w_ukv[0], ts=2 * INPROJ_SUB)
    o_mla = _mla_attention(q, k, vt, tq=ATTN_TQ)
    o_diff = _diff_attention(dq, dk, dvt, a_lq1[0], a_lk1[0], a_lq2[0], a_lk2[0], a_subln[0], lambda_init,
                             tq=ATTN_TQ)
    x2d = _outproj_ffn(x.reshape(n_tok, D_MODEL), o_mla.reshape(n_tok, -1), o_diff.reshape(n_tok, -1),
                       a_w_out[0], norm_ffn[0], 0, w_gate, w_up, w_down, norm_final, final_norm=False, tm=256)

    z, xbc, hq, hf, hi, hg, dt = _rec_inproj(x2d, norm_mix[1], s_w_in[0], tm=256)
    seq = lambda t: t.reshape(bsz, s_len, t.shape[-1])
    y, o = _rec_scan(seq(xbc), seq(dt), seq(z), s_conv_w[0], s_conv_b[0], s_dt_bias[0], s_a_log[0], s_d[0],
                     s_norm[0], seq(hq), seq(hf), seq(hi), seq(hg), lb_all[1], h_g_norm[0])
    x2d = _outproj_ffn(x2d, y.reshape(n_tok, -1), o.reshape(n_tok, -1), s_w_out[0], norm_ffn[1], 1, w_gate, w_up,
                       w_down, norm_final, final_norm=True, tm=256)
    return x2d.reshape(bsz, s_len, D_MODEL)
```

```python
import functools
import math

import numpy as np
import jax
import jax.numpy as jnp
from jax import lax
from jax.experimental import pallas as pl
from jax.experimental.pallas import tpu as pltpu

F32 = jnp.float32
BF16 = jnp.bfloat16

D_MODEL = 1024
EPS = 1e-6
ROPE_THETA = 10000.0

MLA_HEADS = 8
MLA_Q_LORA = 384
MLA_KV_LORA = 256
MLA_NOPE = 64
MLA_ROPE = 32
MLA_V = 64
DIFF_HEADS = 4
DIFF_HD = 64
DIFF_V = 2 * DIFF_HD

SSM_HEADS = 8
SSM_HEADDIM = 64
SSM_INNER = SSM_HEADS * SSM_HEADDIM
SSM_GROUPS = 2
SSM_STATE = 128
SSM_CONV = 4
SSM_CHUNK = 128
SSM_CONV_DIM = SSM_INNER + 2 * SSM_GROUPS * SSM_STATE
HG_HEADS = 4
HG_EXPAND = 128
HG_VDIM = 128
HG_KDIM_TOTAL = HG_HEADS * HG_EXPAND
HG_WIDTH = HG_HEADS * HG_VDIM
HG_CHUNK = 64

LANES = 128
VMEM_LIMIT = 52 * 1024 * 1024

NT_DIMS = (((1,), (1,)), ((), ()))
TN_DIMS = (((0,), (0,)), ((), ()))


def _dot(a, b):
    return jnp.dot(a, b, preferred_element_type=F32)


def _dot_nt(a, b):
    return lax.dot_general(a, b, NT_DIMS, preferred_element_type=F32)


def _dot_tn(a, b):
    return lax.dot_general(a, b, TN_DIMS, preferred_element_type=F32)


def _rms(x, w):
    return x * lax.rsqrt(jnp.mean(x * x, axis=-1, keepdims=True) + EPS) * w


def _silu(x):
    return x * jax.nn.sigmoid(x)


def _split3(a):
    hi = a.astype(BF16)
    r = a - hi.astype(F32)
    mid = r.astype(BF16)
    lo = (r - mid.astype(F32)).astype(BF16)
    return hi, mid, lo


def _dot01_left(m01, a):
    hi, mid, lo = _split3(a)
    return (_dot(m01, hi) + _dot(m01, mid)) + _dot(m01, lo)


def _dot01_right(a, m01):
    hi, mid, lo = _split3(a)
    return (_dot(hi, m01) + _dot(mid, m01)) + _dot(lo, m01)


def _rope(t, cos, sin_a, sin_b, half):
    return (t * cos + pltpu.roll(t, LANES - half, 1) * sin_a + pltpu.roll(t, half, 1) * sin_b)


A_CQ = 0
A_CKV = A_CQ + MLA_Q_LORA
A_DQ = A_CKV + MLA_KV_LORA
A_DK = A_DQ + DIFF_HEADS * 2 * DIFF_HD
A_KR = A_DK + DIFF_HEADS * 2 * DIFF_HD
A_IN_PACKED = A_KR + LANES
INPROJ_SUB = 256


def _attn_inproj_kernel(x_ref, nw_ref, win_ref, wdvt_ref, qn_ref, wuq_ref, kvn_ref, wuk_ref, wuvt_ref,
                        cm_ref, sam_ref, sbm_ref, cd_ref, sad_ref, sbd_ref,
                        q_ref, k_ref, vt_ref, dq_ref, dk_ref, dvt_ref):
    for r0 in range(0, x_ref.shape[1], INPROJ_SUB):
        rs = slice(r0, r0 + INPROJ_SUB)
        hn = _rms(x_ref[0, rs], nw_ref[...]).astype(BF16)
        proj = _dot(hn, win_ref[...])
        cq = _rms(proj[:, A_CQ:A_CKV], qn_ref[...]).astype(BF16)
        ckv = _rms(proj[:, A_CKV:A_DQ], kvn_ref[...]).astype(BF16)
        q = _dot(cq, wuq_ref[...])
        kn = _dot(ckv, wuk_ref[...])
        vt = _dot_nt(wuvt_ref[...], ckv).astype(BF16)
        dvt = _dot_nt(wdvt_ref[...], hn).astype(BF16)
        ones = jnp.ones((VT_ONES, INPROJ_SUB), BF16)
        for g in range(4):
            vt_ref[0, g, 0:LANES, rs] = vt[g * LANES:(g + 1) * LANES]
            vt_ref[0, g, LANES:VT_ROWS, rs] = ones
            dvt_ref[0, g, 0:LANES, rs] = dvt[g * LANES:(g + 1) * LANES]
            dvt_ref[0, g, LANES:VT_ROWS, rs] = ones
        cm, sam, sbm = cm_ref[rs], sam_ref[rs], sbm_ref[rs]
        kpe = _rope(proj[:, A_KR:A_KR + LANES], cm, sam, sbm, MLA_ROPE // 2)
        for h in range(MLA_HEADS):
            sl = slice(h * LANES, (h + 1) * LANES)
            q_ref[0, h, rs] = _rope(q[:, sl], cm, sam, sbm, MLA_ROPE // 2).astype(BF16)
            k_ref[0, h, rs] = (kn[:, sl] + kpe).astype(BF16)
        cd, sad, sbd = cd_ref[rs], sad_ref[rs], sbd_ref[rs]
        for g in range(DIFF_HEADS):
            sl = slice(g * LANES, (g + 1) * LANES)
            dq = proj[:, A_DQ + g * LANES:A_DQ + (g + 1) * LANES]
            dk = proj[:, A_DK + g * LANES:A_DK + (g + 1) * LANES]
            dq_ref[0, rs, sl] = (_rope(dq, cd, sad, sbd, DIFF_HD // 2) * (DIFF_HD ** -0.5 * LOG2E)).astype(BF16)
            dk_ref[0, rs, sl] = _rope(dk, cd, sad, sbd, DIFF_HD // 2).astype(BF16)


def _rope_tables(seq_len, dim, lane_offsets):
    half = dim // 2
    inv_freq = 1.0 / (ROPE_THETA ** (jnp.arange(0, dim, 2, dtype=F32) / dim))
    ang = jnp.arange(seq_len, dtype=F32)[:, None] * inv_freq[None, :]
    cos, sin = jnp.cos(ang), jnp.sin(ang)
    zero = jnp.zeros_like(sin)
    c, sa, sb = [], [], []
    lane = 0
    for off in lane_offsets:
        gap = off - lane
        c += [jnp.ones((seq_len, gap), F32), cos, cos]
        sa += [jnp.zeros((seq_len, gap), F32), -sin, zero]
        sb += [jnp.zeros((seq_len, gap), F32), zero, sin]
        lane = off + dim
    c.append(jnp.ones((seq_len, LANES - lane), F32))
    sa.append(jnp.zeros((seq_len, LANES - lane), F32))
    sb.append(jnp.zeros((seq_len, LANES - lane), F32))
    return tuple(jnp.concatenate(t, axis=-1) for t in (c, sa, sb))


def _const_spec(shape):
    nd = len(shape)
    return pl.BlockSpec(shape, lambda *_: (0,) * nd, pipeline_mode=pl.Buffered(1))


def _attn_inproj(x, norm_w, w_in, q_norm, w_uq, kv_norm, w_ukv, ts):
    bsz, s_len, _ = x.shape
    cq, ckv, kr, dq, dk, dv = jnp.split(
        w_in, [int(v) for v in np.cumsum([MLA_Q_LORA, MLA_KV_LORA, MLA_ROPE, 512, 512])], axis=-1)
    kr_pad = jnp.pad(kr, ((0, 0), (MLA_NOPE, LANES - MLA_NOPE - MLA_ROPE)))
    win_p = jnp.concatenate([cq, ckv, dq, dk, kr_pad], axis=-1).astype(BF16)
    wdvt = dv.T.astype(BF16)
    scale = (MLA_NOPE + MLA_ROPE) ** -0.5 * LOG2E
    wuq_p = jnp.pad((w_uq * scale).reshape(MLA_Q_LORA, MLA_HEADS, MLA_NOPE + MLA_ROPE),
                    ((0, 0), (0, 0), (0, LANES - MLA_NOPE - MLA_ROPE)))
    wuq_p = wuq_p.reshape(MLA_Q_LORA, MLA_HEADS * LANES).astype(BF16)
    wkv = w_ukv.reshape(MLA_KV_LORA, MLA_HEADS, MLA_NOPE + MLA_V)
    wuk_p = jnp.pad(wkv[..., :MLA_NOPE], ((0, 0), (0, 0), (0, LANES - MLA_NOPE)))
    wuk_p = wuk_p.reshape(MLA_KV_LORA, MLA_HEADS * LANES).astype(BF16)
    wuvt = wkv[..., MLA_NOPE:].reshape(MLA_KV_LORA, MLA_HEADS * MLA_V).T.astype(BF16)
    tabs_m = _rope_tables(s_len, MLA_ROPE, (MLA_NOPE,))
    tabs_d = _rope_tables(s_len, DIFF_HD, (0, DIFF_HD))

    row = lambda b, i: (b, i, 0)
    tab = pl.BlockSpec((ts, LANES), lambda b, i: (i, 0))
    head_major = pl.BlockSpec((1, MLA_HEADS, ts, LANES), lambda b, i: (b, 0, i, 0))
    wide = pl.BlockSpec((1, ts, 512), row)
    transposed = pl.BlockSpec((1, 4, VT_ROWS, ts), lambda b, i: (b, 0, 0, i))
    return pl.pallas_call(
        _attn_inproj_kernel,
        grid=(bsz, s_len // ts),
        in_specs=[pl.BlockSpec((1, ts, D_MODEL), row), _const_spec((1, D_MODEL)),
                  _const_spec(win_p.shape), _const_spec(wdvt.shape), _const_spec((1, MLA_Q_LORA)),
                  _const_spec(wuq_p.shape), _const_spec((1, MLA_KV_LORA)), _const_spec(wuk_p.shape),
                  _const_spec(wuvt.shape), tab, tab, tab, tab, tab, tab],
        out_specs=[head_major, head_major, transposed, wide, wide, transposed],
        out_shape=[jax.ShapeDtypeStruct((bsz, MLA_HEADS, s_len, LANES), BF16),
                   jax.ShapeDtypeStruct((bsz, MLA_HEADS, s_len, LANES), BF16),
                   jax.ShapeDtypeStruct((bsz, 4, VT_ROWS, s_len), BF16),
                   jax.ShapeDtypeStruct((bsz, s_len, 512), BF16),
                   jax.ShapeDtypeStruct((bsz, s_len, 512), BF16),
                   jax.ShapeDtypeStruct((bsz, 4, VT_ROWS, s_len), BF16)],
        compiler_params=pltpu.CompilerParams(dimension_semantics=("parallel", "parallel"),
                                             vmem_limit_bytes=VMEM_LIMIT),
        name="attn_inproj",
    )(x, norm_w.reshape(1, -1), win_p, wdvt, q_norm.reshape(1, -1), wuq_p, kv_norm.reshape(1, -1), wuk_p, wuvt,
      *tabs_m, *tabs_d)


ATTN_TQ = 512
ATTN_GROUP = 256
ATTN_UNROLL = 4
VT_ONES = 16
VT_ROWS = LANES + VT_ONES
LOG2E = math.log2(math.e)


def _attn_sweep(i, n_q, tq, load_queries, k_rows, vt_ref, qs_sc, sa_sc, sb_sc, m_sc, acc_sc):
    nq = sa_sc.shape[1]
    groups = [slice(c, c + ATTN_GROUP) for c in range(0, nq, ATTN_GROUP)]

    def start_of(blk):
        return pl.multiple_of(blk * tq, tq)

    def fill_scores(buf, start, cols):
        buf[:, cols] = _dot_nt(k_rows(start, cols), qs_sc[cols])

    def process(s_ref, blk, cols, diagonal=False):
        rows = (cols.start % tq) + ATTN_GROUP if diagonal else tq
        s = s_ref[0:rows, cols]
        if diagonal:
            key = lax.broadcasted_iota(jnp.int32, s.shape, 0)
            qry = (lax.broadcasted_iota(jnp.int32, s.shape, 1) + cols.start) & (tq - 1)
            s = jnp.where(key <= qry, s, -jnp.inf)
        m_prev = m_sc[:, cols]
        m_new = jnp.maximum(m_prev, jnp.max(s, axis=0, keepdims=True))
        alpha = jnp.exp2(m_prev - m_new)
        p = jnp.exp2((s - m_new).astype(BF16))
        acc_sc[:, cols] = alpha * acc_sc[:, cols] + _dot(vt_ref[0, 0, :, pl.ds(start_of(blk), rows)], p)
        m_sc[:, cols] = m_new

    @pl.when(i == 0)
    def _():
        load_queries(start_of(0))
        for cols in groups:
            fill_scores(sa_sc, start_of(0), cols)

    m_sc[...] = jnp.full(m_sc.shape, -jnp.inf, F32)
    acc_sc[...] = jnp.zeros(acc_sc.shape, F32)

    bufs = (sa_sc, sb_sc)

    def run(first, n_full, then_diagonal):
        for t in range(n_full):
            for cols in groups:
                fill_scores(bufs[(t + 1) % 2], start_of(first + t + 1), cols)
                process(bufs[t % 2], first + t, cols)
        if then_diagonal:
            for cols in groups:
                process(bufs[n_full % 2], first + n_full, cols, diagonal=True)

    def body(jj, carry):
        run(ATTN_UNROLL * jj, ATTN_UNROLL, False)
        return carry

    n_main = i // ATTN_UNROLL
    lax.fori_loop(0, n_main, body, 0)
    for rest in range(ATTN_UNROLL):
        @pl.when(i % ATTN_UNROLL == rest)
        def _(rest=rest):
            run(i - rest, rest, True)

    load_queries(start_of(jnp.minimum(i + 1, n_q - 1)))
    for cols in groups:
        fill_scores(sa_sc, start_of(0), cols)
    return acc_sc[LANES:LANES + 1, :]


def _attn_scratch(tq):
    nq = 2 * tq
    return [pltpu.VMEM((nq, LANES), BF16), pltpu.VMEM((tq, nq), F32), pltpu.VMEM((tq, nq), F32),
            pltpu.VMEM((1, nq), F32), pltpu.VMEM((VT_ROWS, nq), F32)]


def _mla_attn_kernel(q_ref, k_ref, vt_ref, o_ref, qs_sc, sa_sc, sb_sc, m_sc, acc_sc, *, tq, n_q):
    def load_queries(start):
        for h in range(2):
            qs_sc[h * tq:(h + 1) * tq] = q_ref[0, h, pl.ds(start, tq), :]

    def k_rows(start, cols):
        return k_ref[0, cols.start // tq, pl.ds(start, tq), :]

    l = _attn_sweep(pl.program_id(2), n_q, tq, load_queries, k_rows, vt_ref, qs_sc, sa_sc, sb_sc, m_sc, acc_sc)
    inv = 1.0 / l
    o_t = jnp.concatenate([acc_sc[0:MLA_V, 0:tq] * inv[:, 0:tq],
                           acc_sc[MLA_V:2 * MLA_V, tq:2 * tq] * inv[:, tq:2 * tq]], axis=0)
    o_ref[0] = o_t.T.astype(BF16)


def _mla_attention(q, k, vt, tq):
    bsz, nh, s_len, _ = q.shape
    return pl.pallas_call(
        functools.partial(_mla_attn_kernel, tq=tq, n_q=s_len // tq),
        grid=(bsz, nh // 2, s_len // tq),
        in_specs=[pl.BlockSpec((1, 2, s_len, LANES), lambda b, p, i: (b, p, 0, 0)),
                  pl.BlockSpec((1, 2, s_len, LANES), lambda b, p, i: (b, p, 0, 0)),
                  pl.BlockSpec((1, 1, VT_ROWS, s_len), lambda b, p, i: (b, p, 0, 0))],
        out_specs=pl.BlockSpec((1, tq, LANES), lambda b, p, i: (b, i, p)),
        out_shape=jax.ShapeDtypeStruct((bsz, s_len, nh * MLA_V), BF16),
        scratch_shapes=_attn_scratch(tq),
        compiler_params=pltpu.CompilerParams(dimension_semantics=("parallel", "parallel", "arbitrary"),
                                             vmem_limit_bytes=VMEM_LIMIT),
        name="mla_attention",
    )(q, k, vt)


def _diff_attn_kernel(q_ref, k_ref, vt_ref, lq1_ref, lk1_ref, lq2_ref, lk2_ref, sub_ref, o_ref,
                      qs_sc, sa_sc, sb_sc, m_sc, acc_sc, *, tq, n_q, lambda_init):
    def load_queries(start):
        q = q_ref[0, pl.ds(start, tq), :]
        lane = lax.broadcasted_iota(jnp.int32, q.shape, 1)
        zero = jnp.zeros_like(q)
        qs_sc[0:tq] = jnp.where(lane < DIFF_HD, q, zero)
        qs_sc[tq:2 * tq] = jnp.where(lane < DIFF_HD, zero, q)

    def k_rows(start, cols):
        return k_ref[0, pl.ds(start, tq), :]

    l = _attn_sweep(pl.program_id(2), n_q, tq, load_queries, k_rows, vt_ref, qs_sc, sa_sc, sb_sc, m_sc, acc_sc)
    o_t = acc_sc[0:LANES, :] * (1.0 / l)
    lam = (jnp.exp(jnp.sum(lq1_ref[...] * lk1_ref[...], axis=-1, keepdims=True))
           - jnp.exp(jnp.sum(lq2_ref[...] * lk2_ref[...], axis=-1, keepdims=True)) + lambda_init)
    od = (o_t[:, 0:tq] - lam * o_t[:, tq:2 * tq]).T
    o_ref[0] = (_rms(od, sub_ref[...]) * (1.0 - lambda_init)).astype(BF16)


def _diff_attention(dq, dk, dvt, lq1, lk1, lq2, lk2, subln, lambda_init, tq):
    bsz, s_len, _ = dq.shape
    qspec = pl.BlockSpec((1, tq, LANES), lambda b, h, i: (b, i, h))
    seq = pl.BlockSpec((1, s_len, LANES), lambda b, h, i: (b, 0, h))
    vec = lambda n: pl.BlockSpec((1, n), lambda b, h, i: (0, 0))
    return pl.pallas_call(
        functools.partial(_diff_attn_kernel, tq=tq, n_q=s_len // tq, lambda_init=lambda_init),
        grid=(bsz, DIFF_HEADS, s_len // tq),
        in_specs=[seq, seq,
                  pl.BlockSpec((1, 1, VT_ROWS, s_len), lambda b, h, i: (b, h, 0, 0)),
                  vec(DIFF_HD), vec(DIFF_HD), vec(DIFF_HD), vec(DIFF_HD), vec(DIFF_V)],
        out_specs=qspec,
        out_shape=jax.ShapeDtypeStruct((bsz, s_len, DIFF_HEADS * DIFF_V), BF16),
        scratch_shapes=_attn_scratch(tq),
        compiler_params=pltpu.CompilerParams(dimension_semantics=("parallel", "parallel", "arbitrary"),
                                             vmem_limit_bytes=VMEM_LIMIT),
        name="diff_attention",
    )(dq, dk, dvt, lq1.reshape(1, -1), lk1.reshape(1, -1), lq2.reshape(1, -1), lk2.reshape(1, -1),
      subln.reshape(1, -1))


FFN_SUB = 256


def _outproj_ffn_kernel(x_ref, ma_ref, mb_ref, woa_ref, wob_ref, nf_ref, wg_ref, wu_ref, wd_ref, fin_ref,
                        o_ref, *, final_norm):
    for r0 in range(0, x_ref.shape[0], FFN_SUB):
        rs = slice(r0, r0 + FFN_SUB)
        x1 = x_ref[rs] + _dot(ma_ref[rs], woa_ref[...]) + _dot(mb_ref[rs], wob_ref[...])
        h = _rms(x1, nf_ref[...]).astype(BF16)
        g = _dot(h, wg_ref[...])
        u = _dot(h, wu_ref[...])
        a = (_silu(g) * u).astype(BF16)
        x2 = x1 + _dot(a, wd_ref[...])
        if final_norm:
            x2 = _rms(x2, fin_ref[...])
        o_ref[rs] = x2


def _outproj_ffn(x2d, mix_a, mix_b, w_out, norm_ffn, layer, w_gate, w_up, w_down, norm_final, final_norm, tm):
    n_tok = x2d.shape[0]
    half = mix_a.shape[1]
    d_ff = w_gate.shape[2]
    woa = w_out[:half].astype(BF16)
    wob = w_out[half:].astype(BF16)
    row = lambda i: (i, 0)
    layer_spec = lambda r, c: pl.BlockSpec((None, r, c), lambda i: (layer, 0, 0), pipeline_mode=pl.Buffered(1))
    return pl.pallas_call(
        functools.partial(_outproj_ffn_kernel, final_norm=final_norm),
        grid=(n_tok // tm,),
        in_specs=[pl.BlockSpec((tm, D_MODEL), row), pl.BlockSpec((tm, half), row), pl.BlockSpec((tm, half), row),
                  _const_spec(woa.shape), _const_spec(wob.shape), _const_spec((1, D_MODEL)),
                  layer_spec(D_MODEL, d_ff), layer_spec(D_MODEL, d_ff), layer_spec(d_ff, D_MODEL),
                  _const_spec((1, D_MODEL))],
        out_specs=pl.BlockSpec((tm, D_MODEL), row),
        out_shape=jax.ShapeDtypeStruct((n_tok, D_MODEL), F32),
        compiler_params=pltpu.CompilerParams(dimension_semantics=("parallel",), vmem_limit_bytes=VMEM_LIMIT),
        name="outproj_ffn",
    )(x2d, mix_a, mix_b, woa, wob, norm_ffn.reshape(1, -1), w_gate, w_up, w_down, norm_final.reshape(1, -1))


S_Z = 0
S_XBC = S_Z + SSM_INNER
S_HQ = S_XBC + SSM_CONV_DIM
S_HF = S_HQ + HG_KDIM_TOTAL
S_HI = S_HF + HG_KDIM_TOTAL
S_HG = S_HI + HG_WIDTH
S_DT = S_HG + HG_WIDTH
S_IN_PACKED = S_DT + LANES


def _rec_inproj_kernel(x_ref, nw_ref, win_ref, z_ref, xbc_ref, hq_ref, hf_ref, hi_ref, hg_ref, dt_ref):
    for r0 in range(0, x_ref.shape[0], INPROJ_SUB):
        rs = slice(r0, r0 + INPROJ_SUB)
        hn = _rms(x_ref[rs], nw_ref[...]).astype(BF16)
        proj = _dot(hn, win_ref[...])
        z_ref[rs] = proj[:, S_Z:S_XBC].astype(BF16)
        xbc_ref[rs] = proj[:, S_XBC:S_HQ]
        hq_ref[rs] = proj[:, S_HQ:S_HF].astype(BF16)
        hf_ref[rs] = proj[:, S_HF:S_HI]
        hi_ref[rs] = proj[:, S_HI:S_HG].astype(BF16)
        hg_ref[rs] = proj[:, S_HG:S_DT].astype(BF16)
        dt_ref[rs] = proj[:, S_DT:S_IN_PACKED]


def _rec_inproj(x2d, norm_w, w_in, tm):
    n_tok = x2d.shape[0]
    z, xbc, dt, hq, hf, hi, hg = jnp.split(
        w_in, [int(v) for v in np.cumsum([SSM_INNER, SSM_CONV_DIM, SSM_HEADS, 512, 512, 512])], axis=-1)
    dt_pad = jnp.pad(dt, ((0, 0), (0, LANES - SSM_HEADS)))
    win_p = jnp.concatenate([z, xbc, hq, hf, hi, hg, dt_pad], axis=-1).astype(BF16)
    row = lambda i: (i, 0)
    widths = [(SSM_INNER, BF16), (SSM_CONV_DIM, F32), (512, BF16), (512, F32), (512, BF16), (512, BF16),
              (LANES, F32)]
    return pl.pallas_call(
        _rec_inproj_kernel,
        grid=(n_tok // tm,),
        in_specs=[pl.BlockSpec((tm, D_MODEL), row), _const_spec((1, D_MODEL)), _const_spec(win_p.shape)],
        out_specs=[pl.BlockSpec((tm, w), row) for w, _ in widths],
        out_shape=[jax.ShapeDtypeStruct((n_tok, w), dt_) for w, dt_ in widths],
        compiler_params=pltpu.CompilerParams(dimension_semantics=("parallel",), vmem_limit_bytes=VMEM_LIMIT),
        name="rec_inproj",
    )(x2d, norm_w.reshape(1, -1), win_p)


CONV_PAD = 8


def _ssd_chunk(xbc_ref, dt_ref, z_ref, cw_ref, cb_ref, dtb_ref, ah_ref, dsk_ref, nw_ref, tri_ref, exp_ref,
               y_ref, xpad_sc, st_sc):
    L = SSM_CHUNK
    heads_per_group = SSM_HEADS // SSM_GROUPS
    gw = heads_per_group * SSM_HEADDIM

    r = lax.broadcasted_iota(jnp.int32, (L, L), 0)
    c = lax.broadcasted_iota(jnp.int32, (L, L), 1)
    causal = c <= r
    lane = lax.broadcasted_iota(jnp.int32, (L, LANES), 1)
    nw = nw_ref[...]

    for b in range(xbc_ref.shape[0]):
        xt = xbc_ref[b]
        xpad_sc[b, CONV_PAD:CONV_PAD + L] = xt
        conv = cb_ref[...] + cw_ref[SSM_CONV - 1:SSM_CONV] * xt
        for d in range(1, SSM_CONV):
            conv = conv + cw_ref[SSM_CONV - 1 - d:SSM_CONV - d] * xpad_sc[b, CONV_PAD - d:CONV_PAD - d + L]
        xpad_sc[b, 0:CONV_PAD] = xt[L - CONV_PAD:L]
        xc = _silu(conv)
        xs = xc[:, 0:SSM_INNER]
        b_in = xc[:, SSM_INNER:SSM_INNER + SSM_GROUPS * SSM_STATE].astype(BF16)
        c_in = xc[:, SSM_INNER + SSM_GROUPS * SSM_STATE:].astype(BF16)

        dt = jax.nn.softplus(dt_ref[b] + dtb_ref[...])
        a = dt * ah_ref[...]
        a_cs = _dot01_left(tri_ref[...], a)
        a_cs_t = a_cs.T
        dt_e = _dot01_right(dt, exp_ref[...])
        acs_e = _dot01_right(a_cs, exp_ref[...])
        alast_e = acs_e[L - 1:L, :]
        xdt = xs * dt_e
        xdec = (xdt * jnp.exp2(alast_e - acs_e)).astype(BF16)
        xdt_b = xdt.astype(BF16)
        eacs = jnp.exp2(acs_e)

        ys = []
        for g in range(SSM_GROUPS):
            gs = slice(g * gw, (g + 1) * gw)
            bg = b_in[:, g * SSM_STATE:(g + 1) * SSM_STATE]
            cg = c_in[:, g * SSM_STATE:(g + 1) * SSM_STATE]
            cb = _dot_nt(cg, bg)
            st_prev = st_sc[b * SSM_GROUPS + g]
            y_off = _dot(cg, st_prev.astype(BF16)) * eacs[:, gs]
            st_sc[b * SSM_GROUPS + g] = st_prev * jnp.exp2(alast_e[:, gs]) + _dot_tn(bg, xdec[:, gs])
            for pr in range(heads_per_group // 2):
                xpair = xdt_b[:, g * gw + pr * LANES:g * gw + (pr + 1) * LANES]
                res = []
                for hh in range(2):
                    h = g * heads_per_group + 2 * pr + hh
                    seg = jnp.exp2(jnp.minimum(a_cs[:, h:h + 1] - a_cs_t[h:h + 1, :], 0.0))
                    m = jnp.where(causal, cb * seg, 0.0).astype(BF16)
                    res.append(_dot(m, xpair))
                ys.append(jnp.where(lane < SSM_HEADDIM, res[0], res[1]) + y_off[:, pr * LANES:(pr + 1) * LANES])
        y = jnp.concatenate(ys, axis=1) + dsk_ref[...] * xs
        y = y * _silu(z_ref[b].astype(F32))
        for g in range(SSM_GROUPS):
            sl = slice(g * gw, (g + 1) * gw)
            y_ref[b, :, sl] = _rms(y[:, sl], nw[:, sl]).astype(BF16)


HG_LEVELS = int(math.log2(HG_CHUNK))


def _hgrn_tables():
    C = HG_CHUNK
    idx = np.arange(C)
    tri = np.tril(np.ones((C, C), np.float32))
    mats = [tri]
    masks = []
    for lev in range(HG_LEVELS):
        h = C >> (lev + 1)
        mid = (idx // (2 * h)) * (2 * h) + h - 1
        upper = (idx % (2 * h)) >= h
        j = idx[None, :]
        after = upper[:, None] & (j > mid[:, None]) & (j <= idx[:, None])
        before = (~upper)[:, None] & (j > idx[:, None]) & (j <= mid[:, None])
        mats.append((after | before).astype(np.float32))
        same = (idx[:, None] // (2 * h)) == (idx[None, :] // (2 * h))
        masks.append((same & upper[:, None] & (~upper)[None, :]).astype(np.float32))
    masks.append(np.eye(C, dtype=np.float32))
    w = np.concatenate(mats, axis=0)
    return np.concatenate([w, w, w], axis=1), np.stack(masks, axis=0)


def _hgrn_chunk(hq_ref, hf_ref, hi_ref, hg_ref, lb_ref, gn_ref, w3_ref, masks_ref, o_ref, st_sc, rows):
    C = HG_CHUNK
    lb = lb_ref[...]
    row = lax.broadcasted_iota(jnp.int32, (C, HG_KDIM_TOTAL), 0)
    for b in range(hq_ref.shape[0]):
        xf = hf_ref[b, rows]
        g = jnp.log2(lb + (1.0 - lb) * jax.nn.sigmoid(xf))
        kin = (1.0 - lb) * jax.nn.sigmoid(-xf)
        q = _silu(hq_ref[b, rows].astype(F32))
        sums = _dot(w3_ref[...], jnp.concatenate(_split3(g), axis=0))
        gcum = sums[0:C]
        glast = gcum[C - 1:C]
        q_in = (q * jnp.exp2(gcum)).astype(BF16)
        k_out = (kin * jnp.exp2(glast - gcum)).astype(BF16)
        zs = []
        for lev in range(HG_LEVELS):
            decay = jnp.exp2(sums[C * (1 + lev):C * (2 + lev)])
            after_mid = (row & (C >> (lev + 1))) != 0
            zs.append((jnp.where(after_mid, q, kin) * decay).astype(BF16))
        qb, kb = q.astype(BF16), kin.astype(BF16)
        v = hi_ref[b, rows]
        gate = _silu(hg_ref[b, rows].astype(F32))
        for h in range(HG_HEADS):
            sl = slice(h * HG_EXPAND, (h + 1) * HG_EXPAND)
            scores = masks_ref[HG_LEVELS] * _dot_nt(qb[:, sl], kb[:, sl])
            for lev in range(HG_LEVELS):
                scores = scores + masks_ref[lev] * _dot_nt(zs[lev][:, sl], zs[lev][:, sl])
            st = st_sc[b * HG_HEADS + h]
            vh = v[:, sl]
            o = _dot(scores.astype(BF16), vh) + _dot_nt(q_in[:, sl], st.astype(BF16))
            st_sc[b * HG_HEADS + h] = st * jnp.exp2(glast[:, sl]) + _dot_tn(vh, k_out[:, sl])
            o_ref[b, rows, sl] = (_rms(o, gn_ref[...]) * gate[:, sl]).astype(BF16)


N_SSD_IN = 11
N_HGRN_IN = 8


def _rec_scan_kernel(*refs):
    ssd_in, hg_in = refs[:N_SSD_IN], refs[N_SSD_IN:N_SSD_IN + N_HGRN_IN]
    y_ref, o_ref, xpad_sc, sst_sc, hst_sc = refs[N_SSD_IN + N_HGRN_IN:]

    @pl.when(pl.program_id(0) == 0)
    def _():
        xpad_sc[:, 0:CONV_PAD] = jnp.zeros((xpad_sc.shape[0], CONV_PAD, SSM_CONV_DIM), F32)
        sst_sc[...] = jnp.zeros(sst_sc.shape, F32)
        hst_sc[...] = jnp.zeros(hst_sc.shape, F32)

    _ssd_chunk(*ssd_in, y_ref, xpad_sc, sst_sc)
    for c0 in range(0, SSM_CHUNK, HG_CHUNK):
        _hgrn_chunk(*hg_in, o_ref, hst_sc, slice(c0, c0 + HG_CHUNK))


def _rec_scan(xbc, dt, z, conv_w, conv_b, dt_bias, a_log, d_skip, ssm_norm, hq, hf, hi, hg, lb, g_norm):
    bsz, s_len, _ = xbc.shape
    L = SSM_CHUNK
    pad = lambda v: jnp.pad(v.astype(F32), (0, LANES - SSM_HEADS)).reshape(1, LANES)
    a_head = -jnp.exp(a_log.astype(F32)) * LOG2E
    tri = jnp.asarray(np.tril(np.ones((L, L), np.float32)), BF16)
    expand = np.zeros((LANES, SSM_INNER), np.float32)
    for h in range(SSM_HEADS):
        expand[h, h * SSM_HEADDIM:(h + 1) * SSM_HEADDIM] = 1.0
    expand = jnp.asarray(expand, BF16)
    dsk = jnp.repeat(d_skip.astype(F32), SSM_HEADDIM).reshape(1, SSM_INNER)
    w3, masks = _hgrn_tables()
    w3 = jnp.asarray(w3, BF16)
    masks = jnp.asarray(masks, F32)
    chunk = lambda w: pl.BlockSpec((bsz, L, w), lambda c: (0, c, 0))
    const = lambda shape: pl.BlockSpec(shape, lambda c: (0,) * len(shape))
    ssd_specs = [chunk(SSM_CONV_DIM), chunk(LANES), chunk(SSM_INNER), const((SSM_CONV, SSM_CONV_DIM)),
                 const((1, SSM_CONV_DIM)), const((1, LANES)), const((1, LANES)), const((1, SSM_INNER)),
                 const((1, SSM_INNER)), const((L, L)), const((LANES, SSM_INNER))]
    hg_specs = [chunk(HG_WIDTH), chunk(HG_WIDTH), chunk(HG_WIDTH), chunk(HG_WIDTH), const((1, HG_KDIM_TOTAL)),
                const((1, HG_VDIM)), const(w3.shape), const(masks.shape)]
    assert len(ssd_specs) == N_SSD_IN and len(hg_specs) == N_HGRN_IN
    return pl.pallas_call(
        _rec_scan_kernel,
        grid=(s_len // L,),
        in_specs=ssd_specs + hg_specs,
        out_specs=[chunk(SSM_INNER), chunk(HG_WIDTH)],
        out_shape=[jax.ShapeDtypeStruct((bsz, s_len, SSM_INNER), BF16),
                   jax.ShapeDtypeStruct((bsz, s_len, HG_WIDTH), BF16)],
        scratch_shapes=[pltpu.VMEM((bsz, CONV_PAD + L, SSM_CONV_DIM), F32),
                        pltpu.VMEM((bsz * SSM_GROUPS, SSM_STATE, SSM_INNER // SSM_GROUPS), F32),
                        pltpu.VMEM((bsz * HG_HEADS, HG_VDIM, HG_EXPAND), F32)],
        compiler_params=pltpu.CompilerParams(dimension_semantics=("arbitrary",), vmem_limit_bytes=VMEM_LIMIT),
        name="rec_scan",
    )(xbc, dt, z, conv_w, conv_b.reshape(1, -1), pad(dt_bias), pad(a_head), dsk, ssm_norm.reshape(1, -1),
      tri, expand, hq, hf, hi, hg, lb.reshape(1, -1), g_norm.reshape(1, -1), w3, masks)


def kernel(x, norm_mix, norm_ffn, norm_final, a_w_in, a_q_norm, a_w_uq, a_kv_norm, a_w_ukv, a_lq1, a_lk1, a_lq2, a_lk2, a_subln, a_w_out, s_w_in, s_conv_w, s_conv_b, s_dt_bias, s_a_log, s_d, s_norm, h_g_norm, h_lower_bound, s_w_out, ffn_gate, ffn_up, ffn_down):
    bsz, s_len, _ = x.shape
    n_tok = bsz * s_len
    depth = norm_mix.shape[0]
    assert depth == 2 and s_len % ATTN_TQ == 0
    p_lb = jax.nn.softmax(h_lower_bound.astype(F32), axis=0)
    lb_all = jnp.cumsum(p_lb, axis=0) - p_lb[0:1]
    w_gate, w_up, w_down = ffn_gate.astype(BF16), ffn_up.astype(BF16), ffn_down.astype(BF16)

    lambda_init = 0.8 - 0.6 * math.exp(-0.3 * 0)
    q, k, vt, dq, dk, dvt = _attn_inproj(x, norm_mix[0], a_w_in[0], a_q_norm[0], a_w_uq[0], a_kv_norm[0],
                                       a_w_ukv[0], ts=2 * INPROJ_SUB)
    o_mla = _mla_attention(q, k, vt, tq=ATTN_TQ)
    o_diff = _diff_attention(dq, dk, dvt, a_lq1[0], a_lk1[0], a_lq2[0], a_lk2[0], a_subln[0], lambda_init,
                             tq=ATTN_TQ)
    x2d = _outproj_ffn(x.reshape(n_tok, D_MODEL), o_mla.reshape(n_tok, -1), o_diff.reshape(n_tok, -1),
                       a_w_out[0], norm_ffn[0], 0, w_gate, w_up, w_down, norm_final, final_norm=False,
                       tm=2 * FFN_SUB)

    z, xbc, hq, hf, hi, hg, dt = _rec_inproj(x2d, norm_mix[1], s_w_in[0], tm=2 * INPROJ_SUB)
    seq = lambda t: t.reshape(bsz, s_len, t.shape[-1])
    y, o = _rec_scan(seq(xbc), seq(dt), seq(z), s_conv_w[0], s_conv_b[0], s_dt_bias[0], s_a_log[0], s_d[0],
                     s_norm[0], seq(hq), seq(hf), seq(hi), seq(hg), lb_all[1], h_g_norm[0])
    x2d = _outproj_ffn(x2d, y.reshape(n_tok, -1), o.reshape(n_tok, -1), s_w_out[0], norm_ffn[1], 1, w_gate, w_up,
                       w_down, norm_final, final_norm=True, tm=2 * FFN_SUB)
    return x2d.reshape(bsz, s_len, D_MODEL)
```

```python
import functools
import math
from typing import Any, Callable, NamedTuple

import numpy as np
import jax
import jax.numpy as jnp
from jax import lax
from jax.experimental import pallas as pl
from jax.experimental.pallas import tpu as pltpu

F32 = jnp.float32
BF16 = jnp.bfloat16

D_MODEL = 1024
EPS = 1e-6
ROPE_THETA = 10000.0

MLA_HEADS = 8
MLA_Q_LORA = 384
MLA_KV_LORA = 256
MLA_NOPE = 64
MLA_ROPE = 32
MLA_V = 64
DIFF_HEADS = 4
DIFF_HD = 64
DIFF_V = 2 * DIFF_HD

SSM_HEADS = 8
SSM_HEADDIM = 64
SSM_INNER = SSM_HEADS * SSM_HEADDIM
SSM_GROUPS = 2
SSM_STATE = 128
SSM_CONV = 4
SSM_CHUNK = 128
SSM_CONV_DIM = SSM_INNER + 2 * SSM_GROUPS * SSM_STATE
HG_HEADS = 4
HG_EXPAND = 128
HG_VDIM = 128
HG_KDIM_TOTAL = HG_HEADS * HG_EXPAND
HG_WIDTH = HG_HEADS * HG_VDIM
HG_CHUNK = 64

LANES = 128
VMEM_LIMIT = 52 * 1024 * 1024

NT_DIMS = (((1,), (1,)), ((), ()))
TN_DIMS = (((0,), (0,)), ((), ()))


def _dot(a, b):
    return jnp.dot(a, b, preferred_element_type=F32)


def _dot_nt(a, b):
    return lax.dot_general(a, b, NT_DIMS, preferred_element_type=F32)


def _dot_tn(a, b):
    return lax.dot_general(a, b, TN_DIMS, preferred_element_type=F32)


def _rms(x, w):
    return x * lax.rsqrt(jnp.mean(x * x, axis=-1, keepdims=True) + EPS) * w


def _silu(x):
    return x * jax.nn.sigmoid(x)


def _split3(a):
    hi = a.astype(BF16)
    r = a - hi.astype(F32)
    mid = r.astype(BF16)
    lo = (r - mid.astype(F32)).astype(BF16)
    return hi, mid, lo


def _dot01_left(m01, a):
    hi, mid, lo = _split3(a)
    return (_dot(m01, hi) + _dot(m01, mid)) + _dot(m01, lo)


def _dot01_right(a, m01):
    hi, mid, lo = _split3(a)
    return (_dot(hi, m01) + _dot(mid, m01)) + _dot(lo, m01)


def _rope(t, cos, sin_a, sin_b, half):
    return (t * cos + pltpu.roll(t, LANES - half, 1) * sin_a + pltpu.roll(t, half, 1) * sin_b)


A_CQ = 0
A_CKV = A_CQ + MLA_Q_LORA
A_DQ = A_CKV + MLA_KV_LORA
A_DK = A_DQ + DIFF_HEADS * 2 * DIFF_HD
A_KR = A_DK + DIFF_HEADS * 2 * DIFF_HD
A_IN_PACKED = A_KR + LANES
INPROJ_SUB = 256


def _attn_inproj_kernel(x_ref, nw_ref, win_ref, wdvt_ref, qn_ref, wuq_ref, kvn_ref, wuk_ref, wuvt_ref,
                        cm_ref, sam_ref, sbm_ref, cd_ref, sad_ref, sbd_ref,
                        q_ref, k_ref, vt_ref, dq_ref, dk_ref, dvt_ref):
    for r0 in range(0, x_ref.shape[1], INPROJ_SUB):
        rs = slice(r0, r0 + INPROJ_SUB)
        hn = _rms(x_ref[0, rs], nw_ref[...]).astype(BF16)
        proj = _dot(hn, win_ref[...])
        cq = _rms(proj[:, A_CQ:A_CKV], qn_ref[...]).astype(BF16)
        ckv = _rms(proj[:, A_CKV:A_DQ], kvn_ref[...]).astype(BF16)
        q = _dot(cq, wuq_ref[...])
        kn = _dot(ckv, wuk_ref[...])
        vt = _dot_nt(wuvt_ref[...], ckv).astype(BF16)
        dvt = _dot_nt(wdvt_ref[...], hn).astype(BF16)
        ones = jnp.ones((VT_ONES, INPROJ_SUB), BF16)
        for g in range(4):
            vt_ref[0, g, 0:LANES, rs] = vt[g * LANES:(g + 1) * LANES]
            vt_ref[0, g, LANES:VT_ROWS, rs] = ones
            dvt_ref[0, g, 0:LANES, rs] = dvt[g * LANES:(g + 1) * LANES]
            dvt_ref[0, g, LANES:VT_ROWS, rs] = ones
        cm, sam, sbm = cm_ref[rs], sam_ref[rs], sbm_ref[rs]
        kpe = _rope(proj[:, A_KR:A_KR + LANES], cm, sam, sbm, MLA_ROPE // 2)
        for h in range(MLA_HEADS):
            sl = slice(h * LANES, (h + 1) * LANES)
            q_ref[0, h, rs] = _rope(q[:, sl], cm, sam, sbm, MLA_ROPE // 2).astype(BF16)
            k_ref[0, h, rs] = (kn[:, sl] + kpe).astype(BF16)
        cd, sad, sbd = cd_ref[rs], sad_ref[rs], sbd_ref[rs]
        for g in range(DIFF_HEADS):
            sl = slice(g * LANES, (g + 1) * LANES)
            dq = proj[:, A_DQ + g * LANES:A_DQ + (g + 1) * LANES]
            dk = proj[:, A_DK + g * LANES:A_DK + (g + 1) * LANES]
            dq_ref[0, rs, sl] = (_rope(dq, cd, sad, sbd, DIFF_HD // 2) * (DIFF_HD ** -0.5 * LOG2E)).astype(BF16)
            dk_ref[0, rs, sl] = _rope(dk, cd, sad, sbd, DIFF_HD // 2).astype(BF16)


def _rope_tables(seq_len, dim, lane_offsets):
    half = dim // 2
    inv_freq = 1.0 / (ROPE_THETA ** (jnp.arange(0, dim, 2, dtype=F32) / dim))
    ang = jnp.arange(seq_len, dtype=F32)[:, None] * inv_freq[None, :]
    cos, sin = jnp.cos(ang), jnp.sin(ang)
    zero = jnp.zeros_like(sin)
    c, sa, sb = [], [], []
    lane = 0
    for off in lane_offsets:
        gap = off - lane
        c += [jnp.ones((seq_len, gap), F32), cos, cos]
        sa += [jnp.zeros((seq_len, gap), F32), -sin, zero]
        sb += [jnp.zeros((seq_len, gap), F32), zero, sin]
        lane = off + dim
    c.append(jnp.ones((seq_len, LANES - lane), F32))
    sa.append(jnp.zeros((seq_len, LANES - lane), F32))
    sb.append(jnp.zeros((seq_len, LANES - lane), F32))
    return tuple(jnp.concatenate(t, axis=-1) for t in (c, sa, sb))


def _const_spec(shape):
    nd = len(shape)
    return pl.BlockSpec(shape, lambda *_: (0,) * nd, pipeline_mode=pl.Buffered(1))


def _attn_inproj(x, norm_w, w_in, q_norm, w_uq, kv_norm, w_ukv, ts):
    bsz, s_len, _ = x.shape
    cq, ckv, kr, dq, dk, dv = jnp.split(
        w_in, [int(v) for v in np.cumsum([MLA_Q_LORA, MLA_KV_LORA, MLA_ROPE, 512, 512])], axis=-1)
    kr_pad = jnp.pad(kr, ((0, 0), (MLA_NOPE, LANES - MLA_NOPE - MLA_ROPE)))
    win_p = jnp.concatenate([cq, ckv, dq, dk, kr_pad], axis=-1).astype(BF16)
    wdvt = dv.T.astype(BF16)
    scale = (MLA_NOPE + MLA_ROPE) ** -0.5 * LOG2E
    wuq_p = jnp.pad((w_uq * scale).reshape(MLA_Q_LORA, MLA_HEADS, MLA_NOPE + MLA_ROPE),
                    ((0, 0), (0, 0), (0, LANES - MLA_NOPE - MLA_ROPE)))
    wuq_p = wuq_p.reshape(MLA_Q_LORA, MLA_HEADS * LANES).astype(BF16)
    wkv = w_ukv.reshape(MLA_KV_LORA, MLA_HEADS, MLA_NOPE + MLA_V)
    wuk_p = jnp.pad(wkv[..., :MLA_NOPE], ((0, 0), (0, 0), (0, LANES - MLA_NOPE)))
    wuk_p = wuk_p.reshape(MLA_KV_LORA, MLA_HEADS * LANES).astype(BF16)
    wuvt = wkv[..., MLA_NOPE:].reshape(MLA_KV_LORA, MLA_HEADS * MLA_V).T.astype(BF16)
    tabs_m = _rope_tables(s_len, MLA_ROPE, (MLA_NOPE,))
    tabs_d = _rope_tables(s_len, DIFF_HD, (0, DIFF_HD))

    row = lambda b, i: (b, i, 0)
    tab = pl.BlockSpec((ts, LANES), lambda b, i: (i, 0))
    head_major = pl.BlockSpec((1, MLA_HEADS, ts, LANES), lambda b, i: (b, 0, i, 0))
    wide = pl.BlockSpec((1, ts, 512), row)
    transposed = pl.BlockSpec((1, 4, VT_ROWS, ts), lambda b, i: (b, 0, 0, i))
    return pl.pallas_call(
        _attn_inproj_kernel,
        grid=(bsz, s_len // ts),
        in_specs=[pl.BlockSpec((1, ts, D_MODEL), row), _const_spec((1, D_MODEL)),
                  _const_spec(win_p.shape), _const_spec(wdvt.shape), _const_spec((1, MLA_Q_LORA)),
                  _const_spec(wuq_p.shape), _const_spec((1, MLA_KV_LORA)), _const_spec(wuk_p.shape),
                  _const_spec(wuvt.shape), tab, tab, tab, tab, tab, tab],
        out_specs=[head_major, head_major, transposed, wide, wide, transposed],
        out_shape=[jax.ShapeDtypeStruct((bsz, MLA_HEADS, s_len, LANES), BF16),
                   jax.ShapeDtypeStruct((bsz, MLA_HEADS, s_len, LANES), BF16),
                   jax.ShapeDtypeStruct((bsz, 4, VT_ROWS, s_len), BF16),
                   jax.ShapeDtypeStruct((bsz, s_len, 512), BF16),
                   jax.ShapeDtypeStruct((bsz, s_len, 512), BF16),
                   jax.ShapeDtypeStruct((bsz, 4, VT_ROWS, s_len), BF16)],
        compiler_params=pltpu.CompilerParams(dimension_semantics=("parallel", "parallel"),
                                             vmem_limit_bytes=VMEM_LIMIT),
        name="attn_inproj",
    )(x, norm_w.reshape(1, -1), win_p, wdvt, q_norm.reshape(1, -1), wuq_p, kv_norm.reshape(1, -1), wuk_p, wuvt,
      *tabs_m, *tabs_d)


ATTN_TQ = 512
ATTN_GROUP = 256
ATTN_UNROLL = 4
VT_ONES = 16
VT_ROWS = LANES + VT_ONES
LOG2E = math.log2(math.e)


class _AttnStream(NamedTuple):
    load_queries: Callable
    k_rows: Callable
    vt_ref: Any
    qs_sc: Any
    sa_sc: Any
    sb_sc: Any
    m_sc: Any
    acc_sc: Any


def _attn_sweep(i, n_q, tq, streams):
    nq = streams[0].sa_sc.shape[1]
    work = [(st, slice(c, c + ATTN_GROUP)) for c in range(0, nq, ATTN_GROUP) for st in streams]

    def start_of(blk):
        return pl.multiple_of(blk * tq, tq)

    def fill_scores(st, buf, start, cols):
        buf[:, cols] = _dot_nt(st.k_rows(start, cols), st.qs_sc[cols])

    def process(st, s_ref, blk, cols, diagonal=False):
        rows = (cols.start % tq) + ATTN_GROUP if diagonal else tq
        s = s_ref[0:rows, cols]
        if diagonal:
            key = lax.broadcasted_iota(jnp.int32, s.shape, 0)
            qry = (lax.broadcasted_iota(jnp.int32, s.shape, 1) + cols.start) & (tq - 1)
            s = jnp.where(key <= qry, s, -jnp.inf)
        m_prev = st.m_sc[:, cols]
        m_new = jnp.maximum(m_prev, jnp.max(s, axis=0, keepdims=True))
        alpha = jnp.exp2(m_prev - m_new)
        p = jnp.exp2((s - m_new).astype(BF16))
        st.acc_sc[:, cols] = (alpha * st.acc_sc[:, cols]
                              + _dot(st.vt_ref[0, 0, :, pl.ds(start_of(blk), rows)], p))
        st.m_sc[:, cols] = m_new

    @pl.when(i == 0)
    def _():
        for st in streams:
            st.load_queries(start_of(0))
        for st, cols in work:
            fill_scores(st, st.sa_sc, start_of(0), cols)

    for st in streams:
        st.m_sc[...] = jnp.full(st.m_sc.shape, -jnp.inf, F32)
        st.acc_sc[...] = jnp.zeros(st.acc_sc.shape, F32)

    def run(first, n_full, then_diagonal):
        for t in range(n_full):
            for st, cols in work:
                bufs = (st.sa_sc, st.sb_sc)
                fill_scores(st, bufs[(t + 1) % 2], start_of(first + t + 1), cols)
                process(st, bufs[t % 2], first + t, cols)
        if then_diagonal:
            for st, cols in work:
                process(st, (st.sa_sc, st.sb_sc)[n_full % 2], first + n_full, cols, diagonal=True)

    def body(jj, carry):
        run(ATTN_UNROLL * jj, ATTN_UNROLL, False)
        return carry

    n_main = i // ATTN_UNROLL
    lax.fori_loop(0, n_main, body, 0)
    for rest in range(ATTN_UNROLL):
        @pl.when(i % ATTN_UNROLL == rest)
        def _(rest=rest):
            run(i - rest, rest, True)

    for st in streams:
        st.load_queries(start_of(jnp.minimum(i + 1, n_q - 1)))
    for st, cols in work:
        fill_scores(st, st.sa_sc, start_of(0), cols)
    return [st.acc_sc[LANES:LANES + 1, :] for st in streams]


def _attn_scratch(tq):
    nq = 2 * tq
    return [pltpu.VMEM((nq, LANES), BF16), pltpu.VMEM((tq, nq), F32), pltpu.VMEM((tq, nq), F32),
            pltpu.VMEM((1, nq), F32), pltpu.VMEM((VT_ROWS, nq), F32)]


N_ATTN_SCRATCH = 5


def _attn_kernel(q_ref, k_ref, vt_ref, dq_ref, dk_ref, dvt_ref, lq1_ref, lk1_ref, lq2_ref, lk2_ref, sub_ref,
                 o_ref, od_ref, *scratch, tq, n_q, lambda_init):
    mla_sc, diff_sc = scratch[:N_ATTN_SCRATCH], scratch[N_ATTN_SCRATCH:]

    def mla_load(start):
        for h in range(2):
            mla_sc[0][h * tq:(h + 1) * tq] = q_ref[0, h, pl.ds(start, tq), :]

    def mla_k_rows(start, cols):
        return k_ref[0, cols.start // tq, pl.ds(start, tq), :]

    def diff_load(start):
        q = dq_ref[0, pl.ds(start, tq), :]
        lane = lax.broadcasted_iota(jnp.int32, q.shape, 1)
        zero = jnp.zeros_like(q)
        diff_sc[0][0:tq] = jnp.where(lane < DIFF_HD, q, zero)
        diff_sc[0][tq:2 * tq] = jnp.where(lane < DIFF_HD, zero, q)

    def diff_k_rows(start, cols):
        return dk_ref[0, pl.ds(start, tq), :]

    streams = [_AttnStream(mla_load, mla_k_rows, vt_ref, *mla_sc),
               _AttnStream(diff_load, diff_k_rows, dvt_ref, *diff_sc)]
    l_mla, l_diff = _attn_sweep(pl.program_id(2), n_q, tq, streams)

    acc = streams[0].acc_sc
    inv = 1.0 / l_mla
    o_t = jnp.concatenate([acc[0:MLA_V, 0:tq] * inv[:, 0:tq],
                           acc[MLA_V:2 * MLA_V, tq:2 * tq] * inv[:, tq:2 * tq]], axis=0)
    o_ref[0] = o_t.T.astype(BF16)

    od_t = streams[1].acc_sc[0:LANES, :] * (1.0 / l_diff)
    lam = (jnp.exp(jnp.sum(lq1_ref[...] * lk1_ref[...], axis=-1, keepdims=True))
           - jnp.exp(jnp.sum(lq2_ref[...] * lk2_ref[...], axis=-1, keepdims=True)) + lambda_init)
    od = (od_t[:, 0:tq] - lam * od_t[:, tq:2 * tq]).T
    od_ref[0] = (_rms(od, sub_ref[...]) * (1.0 - lambda_init)).astype(BF16)


def _attention(q, k, vt, dq, dk, dvt, lq1, lk1, lq2, lk2, subln, lambda_init, tq):
    bsz, nh, s_len, _ = q.shape
    assert nh // 2 == DIFF_HEADS
    heads = pl.BlockSpec((1, 2, s_len, LANES), lambda b, g, i: (b, g, 0, 0))
    seq = pl.BlockSpec((1, s_len, LANES), lambda b, g, i: (b, 0, g))
    values = pl.BlockSpec((1, 1, VT_ROWS, s_len), lambda b, g, i: (b, g, 0, 0))
    vec = lambda n: pl.BlockSpec((1, n), lambda b, g, i: (0, 0))
    out = pl.BlockSpec((1, tq, LANES), lambda b, g, i: (b, i, g))
    return pl.pallas_call(
        functools.partial(_attn_kernel, tq=tq, n_q=s_len // tq, lambda_init=lambda_init),
        grid=(bsz, DIFF_HEADS, s_len // tq),
        in_specs=[heads, heads, values, seq, seq, values,
                  vec(DIFF_HD), vec(DIFF_HD), vec(DIFF_HD), vec(DIFF_HD), vec(DIFF_V)],
        out_specs=[out, out],
        out_shape=[jax.ShapeDtypeStruct((bsz, s_len, nh * MLA_V), BF16),
                   jax.ShapeDtypeStruct((bsz, s_len, DIFF_HEADS * DIFF_V), BF16)],
        scratch_shapes=_attn_scratch(tq) + _attn_scratch(tq),
        compiler_params=pltpu.CompilerParams(dimension_semantics=("parallel", "parallel", "arbitrary"),
                                             vmem_limit_bytes=VMEM_LIMIT),
        name="attention",
    )(q, k, vt, dq, dk, dvt, lq1.reshape(1, -1), lk1.reshape(1, -1), lq2.reshape(1, -1), lk2.reshape(1, -1),
      subln.reshape(1, -1))


FFN_SUB = 256


def _outproj_ffn_kernel(x_ref, ma_ref, mb_ref, woa_ref, wob_ref, nf_ref, wg_ref, wu_ref, wd_ref, fin_ref,
                        o_ref, *, final_norm):
    for r0 in range(0, x_ref.shape[0], FFN_SUB):
        rs = slice(r0, r0 + FFN_SUB)
        x1 = x_ref[rs] + _dot(ma_ref[rs], woa_ref[...]) + _dot(mb_ref[rs], wob_ref[...])
        h = _rms(x1, nf_ref[...]).astype(BF16)
        g = _dot(h, wg_ref[...])
        u = _dot(h, wu_ref[...])
        a = (_silu(g) * u).astype(BF16)
        x2 = x1 + _dot(a, wd_ref[...])
        if final_norm:
            x2 = _rms(x2, fin_ref[...])
        o_ref[rs] = x2


def _outproj_ffn(x2d, mix_a, mix_b, w_out, norm_ffn, layer, w_gate, w_up, w_down, norm_final, final_norm, tm):
    n_tok = x2d.shape[0]
    half = mix_a.shape[1]
    d_ff = w_gate.shape[2]
    woa = w_out[:half].astype(BF16)
    wob = w_out[half:].astype(BF16)
    row = lambda i: (i, 0)
    layer_spec = lambda r, c: pl.BlockSpec((None, r, c), lambda i: (layer, 0, 0), pipeline_mode=pl.Buffered(1))
    return pl.pallas_call(
        functools.partial(_outproj_ffn_kernel, final_norm=final_norm),
        grid=(n_tok // tm,),
        in_specs=[pl.BlockSpec((tm, D_MODEL), row), pl.BlockSpec((tm, half), row), pl.BlockSpec((tm, half), row),
                  _const_spec(woa.shape), _const_spec(wob.shape), _const_spec((1, D_MODEL)),
                  layer_spec(D_MODEL, d_ff), layer_spec(D_MODEL, d_ff), layer_spec(d_ff, D_MODEL),
                  _const_spec((1, D_MODEL))],
        out_specs=pl.BlockSpec((tm, D_MODEL), row),
        out_shape=jax.ShapeDtypeStruct((n_tok, D_MODEL), F32),
        compiler_params=pltpu.CompilerParams(dimension_semantics=("parallel",), vmem_limit_bytes=VMEM_LIMIT),
        name="outproj_ffn",
    )(x2d, mix_a, mix_b, woa, wob, norm_ffn.reshape(1, -1), w_gate, w_up, w_down, norm_final.reshape(1, -1))


S_Z = 0
S_XBC = S_Z + SSM_INNER
S_HQ = S_XBC + SSM_CONV_DIM
S_HF = S_HQ + HG_KDIM_TOTAL
S_HI = S_HF + HG_KDIM_TOTAL
S_HG = S_HI + HG_WIDTH
S_DT = S_HG + HG_WIDTH
S_IN_PACKED = S_DT + LANES


def _rec_inproj_kernel(x_ref, nw_ref, win_ref, z_ref, xbc_ref, hq_ref, hf_ref, hi_ref, hg_ref, dt_ref):
    for r0 in range(0, x_ref.shape[0], INPROJ_SUB):
        rs = slice(r0, r0 + INPROJ_SUB)
        hn = _rms(x_ref[rs], nw_ref[...]).astype(BF16)
        proj = _dot(hn, win_ref[...])
        z_ref[rs] = proj[:, S_Z:S_XBC].astype(BF16)
        xbc_ref[rs] = proj[:, S_XBC:S_HQ]
        hq_ref[rs] = proj[:, S_HQ:S_HF].astype(BF16)
        hf_ref[rs] = proj[:, S_HF:S_HI]
        hi_ref[rs] = proj[:, S_HI:S_HG].astype(BF16)
        hg_ref[rs] = proj[:, S_HG:S_DT].astype(BF16)
        dt_ref[rs] = proj[:, S_DT:S_IN_PACKED]


def _rec_inproj(x2d, norm_w, w_in, tm):
    n_tok = x2d.shape[0]
    z, xbc, dt, hq, hf, hi, hg = jnp.split(
        w_in, [int(v) for v in np.cumsum([SSM_INNER, SSM_CONV_DIM, SSM_HEADS, 512, 512, 512])], axis=-1)
    dt_pad = jnp.pad(dt, ((0, 0), (0, LANES - SSM_HEADS)))
    win_p = jnp.concatenate([z, xbc, hq, hf, hi, hg, dt_pad], axis=-1).astype(BF16)
    row = lambda i: (i, 0)
    widths = [(SSM_INNER, BF16), (SSM_CONV_DIM, F32), (512, BF16), (512, F32), (512, BF16), (512, BF16),
              (LANES, F32)]
    return pl.pallas_call(
        _rec_inproj_kernel,
        grid=(n_tok // tm,),
        in_specs=[pl.BlockSpec((tm, D_MODEL), row), _const_spec((1, D_MODEL)), _const_spec(win_p.shape)],
        out_specs=[pl.BlockSpec((tm, w), row) for w, _ in widths],
        out_shape=[jax.ShapeDtypeStruct((n_tok, w), dt_) for w, dt_ in widths],
        compiler_params=pltpu.CompilerParams(dimension_semantics=("parallel",), vmem_limit_bytes=VMEM_LIMIT),
        name="rec_inproj",
    )(x2d, norm_w.reshape(1, -1), win_p)


CONV_PAD = 8


def _ssd_chunk(xbc_ref, dt_ref, z_ref, cw_ref, cb_ref, dtb_ref, ah_ref, dsk_ref, nw_ref, tri_ref, exp_ref,
               y_ref, xpad_sc, st_sc):
    L = SSM_CHUNK
    heads_per_group = SSM_HEADS // SSM_GROUPS
    gw = heads_per_group * SSM_HEADDIM

    r = lax.broadcasted_iota(jnp.int32, (L, L), 0)
    c = lax.broadcasted_iota(jnp.int32, (L, L), 1)
    causal = c <= r
    lane = lax.broadcasted_iota(jnp.int32, (L, LANES), 1)
    nw = nw_ref[...]

    for b in range(xbc_ref.shape[0]):
        xt = xbc_ref[b]
        xpad_sc[b, CONV_PAD:CONV_PAD + L] = xt
        conv = cb_ref[...] + cw_ref[SSM_CONV - 1:SSM_CONV] * xt
        for d in range(1, SSM_CONV):
            conv = conv + cw_ref[SSM_CONV - 1 - d:SSM_CONV - d] * xpad_sc[b, CONV_PAD - d:CONV_PAD - d + L]
        xpad_sc[b, 0:CONV_PAD] = xt[L - CONV_PAD:L]
        xc = _silu(conv)
        xs = xc[:, 0:SSM_INNER]
        b_in = xc[:, SSM_INNER:SSM_INNER + SSM_GROUPS * SSM_STATE].astype(BF16)
        c_in = xc[:, SSM_INNER + SSM_GROUPS * SSM_STATE:].astype(BF16)

        dt = jax.nn.softplus(dt_ref[b] + dtb_ref[...])
        a = dt * ah_ref[...]
        a_cs = _dot01_left(tri_ref[...], a)
        a_cs_t = a_cs.T
        dt_e = _dot01_right(dt, exp_ref[...])
        acs_e = _dot01_right(a_cs, exp_ref[...])
        alast_e = acs_e[L - 1:L, :]
        xdt = xs * dt_e
        xdec = (xdt * jnp.exp2(alast_e - acs_e)).astype(BF16)
        xdt_b = xdt.astype(BF16)
        eacs = jnp.exp2(acs_e)

        ys = []
        for g in range(SSM_GROUPS):
            gs = slice(g * gw, (g + 1) * gw)
            bg = b_in[:, g * SSM_STATE:(g + 1) * SSM_STATE]
            cg = c_in[:, g * SSM_STATE:(g + 1) * SSM_STATE]
            cb = _dot_nt(cg, bg)
            st_prev = st_sc[b * SSM_GROUPS + g]
            y_off = _dot(cg, st_prev.astype(BF16)) * eacs[:, gs]
            st_sc[b * SSM_GROUPS + g] = st_prev * jnp.exp2(alast_e[:, gs]) + _dot_tn(bg, xdec[:, gs])
            for pr in range(heads_per_group // 2):
                xpair = xdt_b[:, g * gw + pr * LANES:g * gw + (pr + 1) * LANES]
                res = []
                for hh in range(2):
                    h = g * heads_per_group + 2 * pr + hh
                    seg = jnp.exp2(jnp.minimum(a_cs[:, h:h + 1] - a_cs_t[h:h + 1, :], 0.0))
                    m = jnp.where(causal, cb * seg, 0.0).astype(BF16)
                    res.append(_dot(m, xpair))
                ys.append(jnp.where(lane < SSM_HEADDIM, res[0], res[1]) + y_off[:, pr * LANES:(pr + 1) * LANES])
        y = jnp.concatenate(ys, axis=1) + dsk_ref[...] * xs
        y = y * _silu(z_ref[b].astype(F32))
        for g in range(SSM_GROUPS):
            sl = slice(g * gw, (g + 1) * gw)
            y_ref[b, :, sl] = _rms(y[:, sl], nw[:, sl]).astype(BF16)


HG_LEVELS = int(math.log2(HG_CHUNK))


def _hgrn_tables():
    C = HG_CHUNK
    idx = np.arange(C)
    tri = np.tril(np.ones((C, C), np.float32))
    mats = [tri]
    masks = []
    for lev in range(HG_LEVELS):
        h = C >> (lev + 1)
        mid = (idx // (2 * h)) * (2 * h) + h - 1
        upper = (idx % (2 * h)) >= h
        j = idx[None, :]
        after = upper[:, None] & (j > mid[:, None]) & (j <= idx[:, None])
        before = (~upper)[:, None] & (j > idx[:, None]) & (j <= mid[:, None])
        mats.append((after | before).astype(np.float32))
        same = (idx[:, None] // (2 * h)) == (idx[None, :] // (2 * h))
        masks.append((same & upper[:, None] & (~upper)[None, :]).astype(np.float32))
    masks.append(np.eye(C, dtype=np.float32))
    w = np.concatenate(mats, axis=0)
    return np.concatenate([w, w, w], axis=1), np.stack(masks, axis=0)


def _hgrn_chunk(hq_ref, hf_ref, hi_ref, hg_ref, lb_ref, gn_ref, w3_ref, masks_ref, o_ref, st_sc, rows):
    C = HG_CHUNK
    lb = lb_ref[...]
    row = lax.broadcasted_iota(jnp.int32, (C, HG_KDIM_TOTAL), 0)
    for b in range(hq_ref.shape[0]):
        xf = hf_ref[b, rows]
        g = jnp.log2(lb + (1.0 - lb) * jax.nn.sigmoid(xf))
        kin = (1.0 - lb) * jax.nn.sigmoid(-xf)
        q = _silu(hq_ref[b, rows].astype(F32))
        sums = _dot(w3_ref[...], jnp.concatenate(_split3(g), axis=0))
        gcum = sums[0:C]
        glast = gcum[C - 1:C]
        q_in = (q * jnp.exp2(gcum)).astype(BF16)
        k_out = (kin * jnp.exp2(glast - gcum)).astype(BF16)
        zs = []
        for lev in range(HG_LEVELS):
            decay = jnp.exp2(sums[C * (1 + lev):C * (2 + lev)])
            after_mid = (row & (C >> (lev + 1))) != 0
            zs.append((jnp.where(after_mid, q, kin) * decay).astype(BF16))
        qb, kb = q.astype(BF16), kin.astype(BF16)
        v = hi_ref[b, rows]
        gate = _silu(hg_ref[b, rows].astype(F32))
        for h in range(HG_HEADS):
            sl = slice(h * HG_EXPAND, (h + 1) * HG_EXPAND)
            scores = masks_ref[HG_LEVELS] * _dot_nt(qb[:, sl], kb[:, sl])
            for lev in range(HG_LEVELS):
                scores = scores + masks_ref[lev] * _dot_nt(zs[lev][:, sl], zs[lev][:, sl])
            st = st_sc[b * HG_HEADS + h]
            vh = v[:, sl]
            o = _dot(scores.astype(BF16), vh) + _dot_nt(q_in[:, sl], st.astype(BF16))
            st_sc[b * HG_HEADS + h] = st * jnp.exp2(glast[:, sl]) + _dot_tn(vh, k_out[:, sl])
            o_ref[b, rows, sl] = (_rms(o, gn_ref[...]) * gate[:, sl]).astype(BF16)


N_SSD_IN = 11
N_HGRN_IN = 8


def _rec_scan_kernel(*refs):
    ssd_in, hg_in = refs[:N_SSD_IN], refs[N_SSD_IN:N_SSD_IN + N_HGRN_IN]
    y_ref, o_ref, xpad_sc, sst_sc, hst_sc = refs[N_SSD_IN + N_HGRN_IN:]

    @pl.when(pl.program_id(0) == 0)
    def _():
        xpad_sc[:, 0:CONV_PAD] = jnp.zeros((xpad_sc.shape[0], CONV_PAD, SSM_CONV_DIM), F32)
        sst_sc[...] = jnp.zeros(sst_sc.shape, F32)
        hst_sc[...] = jnp.zeros(hst_sc.shape, F32)

    _ssd_chunk(*ssd_in, y_ref, xpad_sc, sst_sc)
    for c0 in range(0, SSM_CHUNK, HG_CHUNK):
        _hgrn_chunk(*hg_in, o_ref, hst_sc, slice(c0, c0 + HG_CHUNK))


def _rec_scan(xbc, dt, z, conv_w, conv_b, dt_bias, a_log, d_skip, ssm_norm, hq, hf, hi, hg, lb, g_norm):
    bsz, s_len, _ = xbc.shape
    L = SSM_CHUNK
    pad = lambda v: jnp.pad(v.astype(F32), (0, LANES - SSM_HEADS)).reshape(1, LANES)
    a_head = -jnp.exp(a_log.astype(F32)) * LOG2E
    tri = jnp.asarray(np.tril(np.ones((L, L), np.float32)), BF16)
    expand = np.zeros((LANES, SSM_INNER), np.float32)
    for h in range(SSM_HEADS):
        expand[h, h * SSM_HEADDIM:(h + 1) * SSM_HEADDIM] = 1.0
    expand = jnp.asarray(expand, BF16)
    dsk = jnp.repeat(d_skip.astype(F32), SSM_HEADDIM).reshape(1, SSM_INNER)
    w3, masks = _hgrn_tables()
    w3 = jnp.asarray(w3, BF16)
    masks = jnp.asarray(masks, F32)
    chunk = lambda w: pl.BlockSpec((bsz, L, w), lambda c: (0, c, 0))
    const = lambda shape: pl.BlockSpec(shape, lambda c: (0,) * len(shape))
    ssd_specs = [chunk(SSM_CONV_DIM), chunk(LANES), chunk(SSM_INNER), const((SSM_CONV, SSM_CONV_DIM)),
                 const((1, SSM_CONV_DIM)), const((1, LANES)), const((1, LANES)), const((1, SSM_INNER)),
                 const((1, SSM_INNER)), const((L, L)), const((LANES, SSM_INNER))]
    hg_specs = [chunk(HG_WIDTH), chunk(HG_WIDTH), chunk(HG_WIDTH), chunk(HG_WIDTH), const((1, HG_KDIM_TOTAL)),
                const((1, HG_VDIM)), const(w3.shape), const(masks.shape)]
    assert len(ssd_specs) == N_SSD_IN and len(hg_specs) == N_HGRN_IN
    return pl.pallas_call(
        _rec_scan_kernel,
        grid=(s_len // L,),
        in_specs=ssd_specs + hg_specs,
        out_specs=[chunk(SSM_INNER), chunk(HG_WIDTH)],
        out_shape=[jax.ShapeDtypeStruct((bsz, s_len, SSM_INNER), BF16),
                   jax.ShapeDtypeStruct((bsz, s_len, HG_WIDTH), BF16)],
        scratch_shapes=[pltpu.VMEM((bsz, CONV_PAD + L, SSM_CONV_DIM), F32),
                        pltpu.VMEM((bsz * SSM_GROUPS, SSM_STATE, SSM_INNER // SSM_GROUPS), F32),
                        pltpu.VMEM((bsz * HG_HEADS, HG_VDIM, HG_EXPAND), F32)],
        compiler_params=pltpu.CompilerParams(dimension_semantics=("arbitrary",), vmem_limit_bytes=VMEM_LIMIT),
        name="rec_scan",
    )(xbc, dt, z, conv_w, conv_b.reshape(1, -1), pad(dt_bias), pad(a_head), dsk, ssm_norm.reshape(1, -1),
      tri, expand, hq, hf, hi, hg, lb.reshape(1, -1), g_norm.reshape(1, -1), w3, masks)


def kernel(x, norm_mix, norm_ffn, norm_final, a_w_in, a_q_norm, a_w_uq, a_kv_norm, a_w_ukv, a_lq1, a_lk1, a_lq2, a_lk2, a_subln, a_w_out, s_w_in, s_conv_w, s_conv_b, s_dt_bias, s_a_log, s_d, s_norm, h_g_norm, h_lower_bound, s_w_out, ffn_gate, ffn_up, ffn_down):
    bsz, s_len, _ = x.shape
    n_tok = bsz * s_len
    depth = norm_mix.shape[0]
    assert depth == 2 and s_len % ATTN_TQ == 0
    p_lb = jax.nn.softmax(h_lower_bound.astype(F32), axis=0)
    lb_all = jnp.cumsum(p_lb, axis=0) - p_lb[0:1]
    w_gate, w_up, w_down = ffn_gate.astype(BF16), ffn_up.astype(BF16), ffn_down.astype(BF16)

    lambda_init = 0.8 - 0.6 * math.exp(-0.3 * 0)
    q, k, vt, dq, dk, dvt = _attn_inproj(x, norm_mix[0], a_w_in[0], a_q_norm[0], a_w_uq[0], a_kv_norm[0],
                                       a_w_ukv[0], ts=2 * INPROJ_SUB)
    o_mla, o_diff = _attention(q, k, vt, dq, dk, dvt, a_lq1[0], a_lk1[0], a_lq2[0], a_lk2[0], a_subln[0],
                               lambda_init, tq=ATTN_TQ)
    x2d = _outproj_ffn(x.reshape(n_tok, D_MODEL), o_mla.reshape(n_tok, -1), o_diff.reshape(n_tok, -1),
                       a_w_out[0], norm_ffn[0], 0, w_gate, w_up, w_down, norm_final, final_norm=False,
                       tm=2 * FFN_SUB)

    z, xbc, hq, hf, hi, hg, dt = _rec_inproj(x2d, norm_mix[1], s_w_in[0], tm=2 * INPROJ_SUB)
    seq = lambda t: t.reshape(bsz, s_len, t.shape[-1])
    y, o = _rec_scan(seq(xbc), seq(dt), seq(z), s_conv_w[0], s_conv_b[0], s_dt_bias[0], s_a_log[0], s_d[0],
                     s_norm[0], seq(hq), seq(hf), seq(hi), seq(hg), lb_all[1], h_g_norm[0])
    x2d = _outproj_ffn(x2d, y.reshape(n_tok, -1), o.reshape(n_tok, -1), s_w_out[0], norm_ffn[1], 1, w_gate, w_up,
                       w_down, norm_final, final_norm=True, tm=2 * FFN_SUB)
    return x2d.reshape(bsz, s_len, D_MODEL)
```

```python
import functools
import math
from typing import Any, Callable, NamedTuple

import numpy as np
import jax
import jax.numpy as jnp
from jax import lax
from jax.experimental import pallas as pl
from jax.experimental.pallas import tpu as pltpu

F32 = jnp.float32
BF16 = jnp.bfloat16

D_MODEL = 1024
EPS = 1e-6
ROPE_THETA = 10000.0

MLA_HEADS = 8
MLA_Q_LORA = 384
MLA_KV_LORA = 256
MLA_NOPE = 64
MLA_ROPE = 32
MLA_V = 64
DIFF_HEADS = 4
DIFF_HD = 64
DIFF_V = 2 * DIFF_HD

SSM_HEADS = 8
SSM_HEADDIM = 64
SSM_INNER = SSM_HEADS * SSM_HEADDIM
SSM_GROUPS = 2
SSM_STATE = 128
SSM_CONV = 4
SSM_CHUNK = 128
SSM_CONV_DIM = SSM_INNER + 2 * SSM_GROUPS * SSM_STATE
HG_HEADS = 4
HG_EXPAND = 128
HG_VDIM = 128
HG_KDIM_TOTAL = HG_HEADS * HG_EXPAND
HG_WIDTH = HG_HEADS * HG_VDIM
HG_CHUNK = 64

LANES = 128
VMEM_LIMIT = 52 * 1024 * 1024

NT_DIMS = (((1,), (1,)), ((), ()))
TN_DIMS = (((0,), (0,)), ((), ()))


def _dot(a, b):
    return jnp.dot(a, b, preferred_element_type=F32)


def _dot_nt(a, b):
    return lax.dot_general(a, b, NT_DIMS, preferred_element_type=F32)


def _dot_tn(a, b):
    return lax.dot_general(a, b, TN_DIMS, preferred_element_type=F32)


def _rms(x, w):
    return x * lax.rsqrt(jnp.mean(x * x, axis=-1, keepdims=True) + EPS) * w


def _silu(x):
    return x * jax.nn.sigmoid(x)


def _split3(a):
    hi = a.astype(BF16)
    r = a - hi.astype(F32)
    mid = r.astype(BF16)
    lo = (r - mid.astype(F32)).astype(BF16)
    return hi, mid, lo


def _dot01_left(m01, a):
    hi, mid, lo = _split3(a)
    return (_dot(m01, hi) + _dot(m01, mid)) + _dot(m01, lo)


def _dot01_right(a, m01):
    hi, mid, lo = _split3(a)
    return (_dot(hi, m01) + _dot(mid, m01)) + _dot(lo, m01)


def _rope(t, cos, sin_a, sin_b, half):
    return (t * cos + pltpu.roll(t, LANES - half, 1) * sin_a + pltpu.roll(t, half, 1) * sin_b)


A_CQ = 0
A_CKV = A_CQ + MLA_Q_LORA
A_DQ = A_CKV + MLA_KV_LORA
A_DK = A_DQ + DIFF_HEADS * 2 * DIFF_HD
A_KR = A_DK + DIFF_HEADS * 2 * DIFF_HD
A_IN_PACKED = A_KR + LANES
INPROJ_SUB = 256


def _attn_inproj_kernel(x_ref, nw_ref, win_ref, wdvt_ref, qn_ref, wuq_ref, kvn_ref, wuk_ref, wuvt_ref,
                        cm_ref, sam_ref, sbm_ref, cd_ref, sad_ref, sbd_ref,
                        q_ref, k_ref, vt_ref, dq_ref, dk_ref, dvt_ref):
    for r0 in range(0, x_ref.shape[1], INPROJ_SUB):
        rs = slice(r0, r0 + INPROJ_SUB)
        hn = _rms(x_ref[0, rs], nw_ref[...]).astype(BF16)
        proj = _dot(hn, win_ref[...])
        cq = _rms(proj[:, A_CQ:A_CKV], qn_ref[...]).astype(BF16)
        ckv = _rms(proj[:, A_CKV:A_DQ], kvn_ref[...]).astype(BF16)
        q = _dot(cq, wuq_ref[...])
        kn = _dot(ckv, wuk_ref[...])
        vt = _dot_nt(wuvt_ref[...], ckv).astype(BF16)
        dvt = _dot_nt(wdvt_ref[...], hn).astype(BF16)
        ones = jnp.ones((VT_ONES, INPROJ_SUB), BF16)
        for g in range(4):
            vt_ref[0, g, 0:LANES, rs] = vt[g * LANES:(g + 1) * LANES]
            vt_ref[0, g, LANES:VT_ROWS, rs] = ones
            dvt_ref[0, g, 0:LANES, rs] = dvt[g * LANES:(g + 1) * LANES]
            dvt_ref[0, g, LANES:VT_ROWS, rs] = ones
        cm, sam, sbm = cm_ref[rs], sam_ref[rs], sbm_ref[rs]
        kpe = _rope(proj[:, A_KR:A_KR + LANES], cm, sam, sbm, MLA_ROPE // 2)
        for h in range(MLA_HEADS):
            sl = slice(h * LANES, (h + 1) * LANES)
            q_ref[0, h, rs] = _rope(q[:, sl], cm, sam, sbm, MLA_ROPE // 2).astype(BF16)
            k_ref[0, h, rs] = (kn[:, sl] + kpe).astype(BF16)
        cd, sad, sbd = cd_ref[rs], sad_ref[rs], sbd_ref[rs]
        for g in range(DIFF_HEADS):
            sl = slice(g * LANES, (g + 1) * LANES)
            dq = proj[:, A_DQ + g * LANES:A_DQ + (g + 1) * LANES]
            dk = proj[:, A_DK + g * LANES:A_DK + (g + 1) * LANES]
            dq_ref[0, rs, sl] = (_rope(dq, cd, sad, sbd, DIFF_HD // 2) * (DIFF_HD ** -0.5 * LOG2E)).astype(BF16)
            dk_ref[0, rs, sl] = _rope(dk, cd, sad, sbd, DIFF_HD // 2).astype(BF16)


def _rope_tables(seq_len, dim, lane_offsets):
    half = dim // 2
    inv_freq = 1.0 / (ROPE_THETA ** (jnp.arange(0, dim, 2, dtype=F32) / dim))
    ang = jnp.arange(seq_len, dtype=F32)[:, None] * inv_freq[None, :]
    cos, sin = jnp.cos(ang), jnp.sin(ang)
    zero = jnp.zeros_like(sin)
    c, sa, sb = [], [], []
    lane = 0
    for off in lane_offsets:
        gap = off - lane
        c += [jnp.ones((seq_len, gap), F32), cos, cos]
        sa += [jnp.zeros((seq_len, gap), F32), -sin, zero]
        sb += [jnp.zeros((seq_len, gap), F32), zero, sin]
        lane = off + dim
    c.append(jnp.ones((seq_len, LANES - lane), F32))
    sa.append(jnp.zeros((seq_len, LANES - lane), F32))
    sb.append(jnp.zeros((seq_len, LANES - lane), F32))
    return tuple(jnp.concatenate(t, axis=-1) for t in (c, sa, sb))


def _const_spec(shape):
    nd = len(shape)
    return pl.BlockSpec(shape, lambda *_: (0,) * nd, pipeline_mode=pl.Buffered(1))


def _attn_inproj(x, norm_w, w_in, q_norm, w_uq, kv_norm, w_ukv, ts):
    bsz, s_len, _ = x.shape
    cq, ckv, kr, dq, dk, dv = jnp.split(
        w_in, [int(v) for v in np.cumsum([MLA_Q_LORA, MLA_KV_LORA, MLA_ROPE, 512, 512])], axis=-1)
    kr_pad = jnp.pad(kr, ((0, 0), (MLA_NOPE, LANES - MLA_NOPE - MLA_ROPE)))
    win_p = jnp.concatenate([cq, ckv, dq, dk, kr_pad], axis=-1).astype(BF16)
    wdvt = dv.T.astype(BF16)
    scale = (MLA_NOPE + MLA_ROPE) ** -0.5 * LOG2E
    wuq_p = jnp.pad((w_uq * scale).reshape(MLA_Q_LORA, MLA_HEADS, MLA_NOPE + MLA_ROPE),
                    ((0, 0), (0, 0), (0, LANES - MLA_NOPE - MLA_ROPE)))
    wuq_p = wuq_p.reshape(MLA_Q_LORA, MLA_HEADS * LANES).astype(BF16)
    wkv = w_ukv.reshape(MLA_KV_LORA, MLA_HEADS, MLA_NOPE + MLA_V)
    wuk_p = jnp.pad(wkv[..., :MLA_NOPE], ((0, 0), (0, 0), (0, LANES - MLA_NOPE)))
    wuk_p = wuk_p.reshape(MLA_KV_LORA, MLA_HEADS * LANES).astype(BF16)
    wuvt = wkv[..., MLA_NOPE:].reshape(MLA_KV_LORA, MLA_HEADS * MLA_V).T.astype(BF16)
    tabs_m = _rope_tables(s_len, MLA_ROPE, (MLA_NOPE,))
    tabs_d = _rope_tables(s_len, DIFF_HD, (0, DIFF_HD))

    row = lambda b, i: (b, i, 0)
    tab = pl.BlockSpec((ts, LANES), lambda b, i: (i, 0))
    head_major = pl.BlockSpec((1, MLA_HEADS, ts, LANES), lambda b, i: (b, 0, i, 0))
    wide = pl.BlockSpec((1, ts, 512), row)
    transposed = pl.BlockSpec((1, 4, VT_ROWS, ts), lambda b, i: (b, 0, 0, i))
    return pl.pallas_call(
        _attn_inproj_kernel,
        grid=(bsz, s_len // ts),
        in_specs=[pl.BlockSpec((1, ts, D_MODEL), row), _const_spec((1, D_MODEL)),
                  _const_spec(win_p.shape), _const_spec(wdvt.shape), _const_spec((1, MLA_Q_LORA)),
                  _const_spec(wuq_p.shape), _const_spec((1, MLA_KV_LORA)), _const_spec(wuk_p.shape),
                  _const_spec(wuvt.shape), tab, tab, tab, tab, tab, tab],
        out_specs=[head_major, head_major, transposed, wide, wide, transposed],
        out_shape=[jax.ShapeDtypeStruct((bsz, MLA_HEADS, s_len, LANES), BF16),
                   jax.ShapeDtypeStruct((bsz, MLA_HEADS, s_len, LANES), BF16),
                   jax.ShapeDtypeStruct((bsz, 4, VT_ROWS, s_len), BF16),
                   jax.ShapeDtypeStruct((bsz, s_len, 512), BF16),
                   jax.ShapeDtypeStruct((bsz, s_len, 512), BF16),
                   jax.ShapeDtypeStruct((bsz, 4, VT_ROWS, s_len), BF16)],
        compiler_params=pltpu.CompilerParams(dimension_semantics=("parallel", "parallel"),
                                             vmem_limit_bytes=VMEM_LIMIT),
        name="attn_inproj",
    )(x, norm_w.reshape(1, -1), win_p, wdvt, q_norm.reshape(1, -1), wuq_p, kv_norm.reshape(1, -1), wuk_p, wuvt,
      *tabs_m, *tabs_d)


ATTN_TQ = 512
ATTN_GROUP = 256
ATTN_UNROLL = 4
VT_ONES = 16
VT_ROWS = LANES + VT_ONES
LOG2E = math.log2(math.e)


class _AttnStream(NamedTuple):
    load_queries: Callable
    k_rows: Callable
    vt_ref: Any
    qs_sc: Any
    sa_sc: Any
    sb_sc: Any
    m_sc: Any
    acc_sc: Any


def _attn_sweep(i, n_q, tq, streams):
    nq = streams[0].sa_sc.shape[1]
    work = [(st, slice(c, c + ATTN_GROUP)) for c in range(0, nq, ATTN_GROUP) for st in streams]

    def start_of(blk):
        return pl.multiple_of(blk * tq, tq)

    def fill_scores(st, buf, start, cols):
        buf[:, cols] = _dot_nt(st.k_rows(start, cols), st.qs_sc[cols])

    def process(st, s_ref, blk, cols, diagonal=False):
        rows = (cols.start % tq) + ATTN_GROUP if diagonal else tq
        s = s_ref[0:rows, cols]
        if diagonal:
            key = lax.broadcasted_iota(jnp.int32, s.shape, 0)
            qry = (lax.broadcasted_iota(jnp.int32, s.shape, 1) + cols.start) & (tq - 1)
            s = jnp.where(key <= qry, s, -jnp.inf)
        m_prev = st.m_sc[:, cols]
        m_new = jnp.maximum(m_prev, jnp.max(s, axis=0, keepdims=True))
        alpha = jnp.exp2(m_prev - m_new)
        p = jnp.exp2((s - m_new).astype(BF16))
        st.acc_sc[:, cols] = (alpha * st.acc_sc[:, cols]
                              + _dot(st.vt_ref[0, 0, :, pl.ds(start_of(blk), rows)], p))
        st.m_sc[:, cols] = m_new

    @pl.when(i == 0)
    def _():
        for st in streams:
            st.load_queries(start_of(0))
        for st, cols in work:
            fill_scores(st, st.sa_sc, start_of(0), cols)

    for st in streams:
        st.m_sc[...] = jnp.full(st.m_sc.shape, -jnp.inf, F32)
        st.acc_sc[...] = jnp.zeros(st.acc_sc.shape, F32)

    def run(first, n_full, then_diagonal):
        for t in range(n_full):
            for st, cols in work:
                bufs = (st.sa_sc, st.sb_sc)
                fill_scores(st, bufs[(t + 1) % 2], start_of(first + t + 1), cols)
                process(st, bufs[t % 2], first + t, cols)
        if then_diagonal:
            for st, cols in work:
                process(st, (st.sa_sc, st.sb_sc)[n_full % 2], first + n_full, cols, diagonal=True)

    def body(jj, carry):
        run(ATTN_UNROLL * jj, ATTN_UNROLL, False)
        return carry

    n_main = i // ATTN_UNROLL
    lax.fori_loop(0, n_main, body, 0)
    for rest in range(ATTN_UNROLL):
        @pl.when(i % ATTN_UNROLL == rest)
        def _(rest=rest):
            run(i - rest, rest, True)

    for st in streams:
        st.load_queries(start_of(jnp.minimum(i + 1, n_q - 1)))
    for st, cols in work:
        fill_scores(st, st.sa_sc, start_of(0), cols)
    return [st.acc_sc[LANES:LANES + 1, :] for st in streams]


def _attn_scratch(tq):
    nq = 2 * tq
    return [pltpu.VMEM((nq, LANES), BF16), pltpu.VMEM((tq, nq), F32), pltpu.VMEM((tq, nq), F32),
            pltpu.VMEM((1, nq), F32), pltpu.VMEM((VT_ROWS, nq), F32)]


N_ATTN_SCRATCH = 5


def _attn_kernel(q_ref, k_ref, vt_ref, dq_ref, dk_ref, dvt_ref, lq1_ref, lk1_ref, lq2_ref, lk2_ref, sub_ref,
                 o_ref, od_ref, *scratch, tq, n_q, lambda_init):
    mla_sc, diff_sc = scratch[:N_ATTN_SCRATCH], scratch[N_ATTN_SCRATCH:]

    def mla_load(start):
        for h in range(2):
            mla_sc[0][h * tq:(h + 1) * tq] = q_ref[0, h, pl.ds(start, tq), :]

    def mla_k_rows(start, cols):
        return k_ref[0, cols.start // tq, pl.ds(start, tq), :]

    def diff_load(start):
        q = dq_ref[0, pl.ds(start, tq), :]
        lane = lax.broadcasted_iota(jnp.int32, q.shape, 1)
        zero = jnp.zeros_like(q)
        diff_sc[0][0:tq] = jnp.where(lane < DIFF_HD, q, zero)
        diff_sc[0][tq:2 * tq] = jnp.where(lane < DIFF_HD, zero, q)

    def diff_k_rows(start, cols):
        return dk_ref[0, pl.ds(start, tq), :]

    streams = [_AttnStream(mla_load, mla_k_rows, vt_ref, *mla_sc),
               _AttnStream(diff_load, diff_k_rows, dvt_ref, *diff_sc)]
    l_mla, l_diff = _attn_sweep(pl.program_id(2), n_q, tq, streams)

    acc = streams[0].acc_sc
    inv = 1.0 / l_mla
    o_t = jnp.concatenate([acc[0:MLA_V, 0:tq] * inv[:, 0:tq],
                           acc[MLA_V:2 * MLA_V, tq:2 * tq] * inv[:, tq:2 * tq]], axis=0)
    o_ref[0] = o_t.T.astype(BF16)

    od_t = streams[1].acc_sc[0:LANES, :] * (1.0 / l_diff)
    lam = (jnp.exp(jnp.sum(lq1_ref[...] * lk1_ref[...], axis=-1, keepdims=True))
           - jnp.exp(jnp.sum(lq2_ref[...] * lk2_ref[...], axis=-1, keepdims=True)) + lambda_init)
    od = (od_t[:, 0:tq] - lam * od_t[:, tq:2 * tq]).T
    od_ref[0] = (_rms(od, sub_ref[...]) * (1.0 - lambda_init)).astype(BF16)


def _attention(q, k, vt, dq, dk, dvt, lq1, lk1, lq2, lk2, subln, lambda_init, tq):
    bsz, nh, s_len, _ = q.shape
    assert nh // 2 == DIFF_HEADS
    heads = pl.BlockSpec((1, 2, s_len, LANES), lambda b, g, i: (b, g, 0, 0))
    seq = pl.BlockSpec((1, s_len, LANES), lambda b, g, i: (b, 0, g))
    values = pl.BlockSpec((1, 1, VT_ROWS, s_len), lambda b, g, i: (b, g, 0, 0))
    vec = lambda n: pl.BlockSpec((1, n), lambda b, g, i: (0, 0))
    out = pl.BlockSpec((1, tq, LANES), lambda b, g, i: (b, i, g))
    return pl.pallas_call(
        functools.partial(_attn_kernel, tq=tq, n_q=s_len // tq, lambda_init=lambda_init),
        grid=(bsz, DIFF_HEADS, s_len // tq),
        in_specs=[heads, heads, values, seq, seq, values,
                  vec(DIFF_HD), vec(DIFF_HD), vec(DIFF_HD), vec(DIFF_HD), vec(DIFF_V)],
        out_specs=[out, out],
        out_shape=[jax.ShapeDtypeStruct((bsz, s_len, nh * MLA_V), BF16),
                   jax.ShapeDtypeStruct((bsz, s_len, DIFF_HEADS * DIFF_V), BF16)],
        scratch_shapes=_attn_scratch(tq) + _attn_scratch(tq),
        compiler_params=pltpu.CompilerParams(dimension_semantics=("parallel", "parallel", "arbitrary"),
                                             vmem_limit_bytes=VMEM_LIMIT),
        name="attention",
    )(q, k, vt, dq, dk, dvt, lq1.reshape(1, -1), lk1.reshape(1, -1), lq2.reshape(1, -1), lk2.reshape(1, -1),
      subln.reshape(1, -1))


FFN_SUB = 256


def _outproj_ffn_kernel(x_ref, ma_ref, mb_ref, woa_ref, wob_ref, nf_ref, wg_ref, wu_ref, wd_ref, fin_ref,
                        o_ref, *, final_norm):
    for r0 in range(0, x_ref.shape[0], FFN_SUB):
        rs = slice(r0, r0 + FFN_SUB)
        x1 = x_ref[rs] + _dot(ma_ref[rs], woa_ref[...]) + _dot(mb_ref[rs], wob_ref[...])
        h = _rms(x1, nf_ref[...]).astype(BF16)
        g = _dot(h, wg_ref[...])
        u = _dot(h, wu_ref[...])
        a = (_silu(g) * u).astype(BF16)
        x2 = x1 + _dot(a, wd_ref[...])
        if final_norm:
            x2 = _rms(x2, fin_ref[...])
        o_ref[rs] = x2


def _outproj_ffn(x2d, mix_a, mix_b, w_out, norm_ffn, layer, w_gate, w_up, w_down, norm_final, final_norm, tm):
    n_tok = x2d.shape[0]
    half = mix_a.shape[1]
    d_ff = w_gate.shape[2]
    woa = w_out[:half].astype(BF16)
    wob = w_out[half:].astype(BF16)
    row = lambda i: (i, 0)
    layer_spec = lambda r, c: pl.BlockSpec((None, r, c), lambda i: (layer, 0, 0), pipeline_mode=pl.Buffered(1))
    return pl.pallas_call(
        functools.partial(_outproj_ffn_kernel, final_norm=final_norm),
        grid=(n_tok // tm,),
        in_specs=[pl.BlockSpec((tm, D_MODEL), row), pl.BlockSpec((tm, half), row), pl.BlockSpec((tm, half), row),
                  _const_spec(woa.shape), _const_spec(wob.shape), _const_spec((1, D_MODEL)),
                  layer_spec(D_MODEL, d_ff), layer_spec(D_MODEL, d_ff), layer_spec(d_ff, D_MODEL),
                  _const_spec((1, D_MODEL))],
        out_specs=pl.BlockSpec((tm, D_MODEL), row),
        out_shape=jax.ShapeDtypeStruct((n_tok, D_MODEL), F32),
        compiler_params=pltpu.CompilerParams(dimension_semantics=("parallel",), vmem_limit_bytes=VMEM_LIMIT),
        name="outproj_ffn",
    )(x2d, mix_a, mix_b, woa, wob, norm_ffn.reshape(1, -1), w_gate, w_up, w_down, norm_final.reshape(1, -1))


S_Z = 0
S_XBC = S_Z + SSM_INNER
S_HQ = S_XBC + SSM_CONV_DIM
S_HF = S_HQ + HG_KDIM_TOTAL
S_HI = S_HF + HG_KDIM_TOTAL
S_HG = S_HI + HG_WIDTH
S_DT = S_HG + HG_WIDTH
S_IN_PACKED = S_DT + LANES


def _rec_inproj_kernel(x_ref, nw_ref, win_ref, z_ref, xbc_ref, hq_ref, hf_ref, hi_ref, hg_ref, dt_ref):
    for r0 in range(0, x_ref.shape[0], INPROJ_SUB):
        rs = slice(r0, r0 + INPROJ_SUB)
        hn = _rms(x_ref[rs], nw_ref[...]).astype(BF16)
        proj = _dot(hn, win_ref[...])
        z_ref[rs] = proj[:, S_Z:S_XBC].astype(BF16)
        xbc_ref[rs] = proj[:, S_XBC:S_HQ]
        hq_ref[rs] = proj[:, S_HQ:S_HF].astype(BF16)
        hf_ref[rs] = proj[:, S_HF:S_HI]
        hi_ref[rs] = proj[:, S_HI:S_HG].astype(BF16)
        hg_ref[rs] = proj[:, S_HG:S_DT].astype(BF16)
        dt_ref[rs] = proj[:, S_DT:S_IN_PACKED]


def _rec_inproj(x2d, norm_w, w_in, tm):
    n_tok = x2d.shape[0]
    z, xbc, dt, hq, hf, hi, hg = jnp.split(
        w_in, [int(v) for v in np.cumsum([SSM_INNER, SSM_CONV_DIM, SSM_HEADS, 512, 512, 512])], axis=-1)
    dt_pad = jnp.pad(dt, ((0, 0), (0, LANES - SSM_HEADS)))
    win_p = jnp.concatenate([z, xbc, hq, hf, hi, hg, dt_pad], axis=-1).astype(BF16)
    row = lambda i: (i, 0)
    widths = [(SSM_INNER, BF16), (SSM_CONV_DIM, F32), (512, BF16), (512, F32), (512, BF16), (512, BF16),
              (LANES, F32)]
    return pl.pallas_call(
        _rec_inproj_kernel,
        grid=(n_tok // tm,),
        in_specs=[pl.BlockSpec((tm, D_MODEL), row), _const_spec((1, D_MODEL)), _const_spec(win_p.shape)],
        out_specs=[pl.BlockSpec((tm, w), row) for w, _ in widths],
        out_shape=[jax.ShapeDtypeStruct((n_tok, w), dt_) for w, dt_ in widths],
        compiler_params=pltpu.CompilerParams(dimension_semantics=("parallel",), vmem_limit_bytes=VMEM_LIMIT),
        name="rec_inproj",
    )(x2d, norm_w.reshape(1, -1), win_p)


CONV_PAD = 8


def _ssd_chunk(xbc_ref, dt_ref, z_ref, cw_ref, cb_ref, dtb_ref, ah_ref, dsk_ref, nw_ref, tri_ref, exp_ref,
               y_ref, xpad_sc, st_sc):
    L = SSM_CHUNK
    heads_per_group = SSM_HEADS // SSM_GROUPS
    gw = heads_per_group * SSM_HEADDIM

    r = lax.broadcasted_iota(jnp.int32, (L, L), 0)
    c = lax.broadcasted_iota(jnp.int32, (L, L), 1)
    causal = c <= r
    lane = lax.broadcasted_iota(jnp.int32, (L, LANES), 1)
    nw = nw_ref[...]

    for b in range(xbc_ref.shape[0]):
        xt = xbc_ref[b]
        xpad_sc[b, CONV_PAD:CONV_PAD + L] = xt
        conv = cb_ref[...] + cw_ref[SSM_CONV - 1:SSM_CONV] * xt
        for d in range(1, SSM_CONV):
            conv = conv + cw_ref[SSM_CONV - 1 - d:SSM_CONV - d] * xpad_sc[b, CONV_PAD - d:CONV_PAD - d + L]
        xpad_sc[b, 0:CONV_PAD] = xt[L - CONV_PAD:L]
        xc = _silu(conv)
        xs = xc[:, 0:SSM_INNER]
        b_in = xc[:, SSM_INNER:SSM_INNER + SSM_GROUPS * SSM_STATE].astype(BF16)
        c_in = xc[:, SSM_INNER + SSM_GROUPS * SSM_STATE:].astype(BF16)

        dt = jax.nn.softplus(dt_ref[b] + dtb_ref[...])
        a = dt * ah_ref[...]
        a_cs = _dot01_left(tri_ref[...], a)
        a_cs_t = a_cs.T
        dt_e = _dot01_right(dt, exp_ref[...])
        acs_e = _dot01_right(a_cs, exp_ref[...])
        alast_e = acs_e[L - 1:L, :]
        xdt = xs * dt_e
        xdec = (xdt * jnp.exp2(alast_e - acs_e)).astype(BF16)
        xdt_b = xdt.astype(BF16)
        eacs = jnp.exp2(acs_e)

        groups = [slice(g * gw, (g + 1) * gw) for g in range(SSM_GROUPS)]
        bgs = [b_in[:, g * SSM_STATE:(g + 1) * SSM_STATE] for g in range(SSM_GROUPS)]
        cgs = [c_in[:, g * SSM_STATE:(g + 1) * SSM_STATE] for g in range(SSM_GROUPS)]
        cbs = [_dot_nt(cg, bg) for cg, bg in zip(cgs, bgs)]
        st_prevs = [st_sc[b * SSM_GROUPS + g] for g in range(SSM_GROUPS)]
        y_offs = [_dot(cg, st.astype(BF16)) * eacs[:, gs] for cg, st, gs in zip(cgs, st_prevs, groups)]
        for g, gs in enumerate(groups):
            st_sc[b * SSM_GROUPS + g] = st_prevs[g] * jnp.exp2(alast_e[:, gs]) + _dot_tn(bgs[g], xdec[:, gs])
        mixes = []
        for h in range(SSM_HEADS):
            seg = jnp.exp2(jnp.minimum(a_cs[:, h:h + 1] - a_cs_t[h:h + 1, :], 0.0))
            mixes.append(jnp.where(causal, cbs[h // heads_per_group] * seg, 0.0).astype(BF16))
        ys = []
        for g in range(SSM_GROUPS):
            for pr in range(heads_per_group // 2):
                xpair = xdt_b[:, g * gw + pr * LANES:g * gw + (pr + 1) * LANES]
                h = g * heads_per_group + 2 * pr
                res = [_dot(mixes[h], xpair), _dot(mixes[h + 1], xpair)]
                ys.append(jnp.where(lane < SSM_HEADDIM, res[0], res[1])
                          + y_offs[g][:, pr * LANES:(pr + 1) * LANES])
        y = jnp.concatenate(ys, axis=1) + dsk_ref[...] * xs
        y = y * _silu(z_ref[b].astype(F32))
        for g in range(SSM_GROUPS):
            sl = slice(g * gw, (g + 1) * gw)
            y_ref[b, :, sl] = _rms(y[:, sl], nw[:, sl]).astype(BF16)


HG_LEVELS = int(math.log2(HG_CHUNK))


def _hgrn_tables():
    C = HG_CHUNK
    idx = np.arange(C)
    tri = np.tril(np.ones((C, C), np.float32))
    mats = [tri]
    masks = []
    for lev in range(HG_LEVELS):
        h = C >> (lev + 1)
        mid = (idx // (2 * h)) * (2 * h) + h - 1
        upper = (idx % (2 * h)) >= h
        j = idx[None, :]
        after = upper[:, None] & (j > mid[:, None]) & (j <= idx[:, None])
        before = (~upper)[:, None] & (j > idx[:, None]) & (j <= mid[:, None])
        mats.append((after | before).astype(np.float32))
        same = (idx[:, None] // (2 * h)) == (idx[None, :] // (2 * h))
        masks.append((same & upper[:, None] & (~upper)[None, :]).astype(np.float32))
    masks.append(np.eye(C, dtype=np.float32))
    w = np.concatenate(mats, axis=0)
    return np.concatenate([w, w, w], axis=1), np.stack(masks, axis=0)


def _hgrn_chunk(hq_ref, hf_ref, hi_ref, hg_ref, lb_ref, gn_ref, w3_ref, masks_ref, o_ref, st_sc, rows):
    C = HG_CHUNK
    lb = lb_ref[...]
    row = lax.broadcasted_iota(jnp.int32, (C, HG_KDIM_TOTAL), 0)
    for b in range(hq_ref.shape[0]):
        xf = hf_ref[b, rows]
        g = jnp.log2(lb + (1.0 - lb) * jax.nn.sigmoid(xf))
        kin = (1.0 - lb) * jax.nn.sigmoid(-xf)
        q = _silu(hq_ref[b, rows].astype(F32))
        sums = _dot(w3_ref[...], jnp.concatenate(_split3(g), axis=0))
        gcum = sums[0:C]
        glast = gcum[C - 1:C]
        q_in = (q * jnp.exp2(gcum)).astype(BF16)
        k_out = (kin * jnp.exp2(glast - gcum)).astype(BF16)
        zs = []
        for lev in range(HG_LEVELS):
            decay = jnp.exp2(sums[C * (1 + lev):C * (2 + lev)])
            after_mid = (row & (C >> (lev + 1))) != 0
            zs.append((jnp.where(after_mid, q, kin) * decay).astype(BF16))
        qb, kb = q.astype(BF16), kin.astype(BF16)
        v = hi_ref[b, rows]
        gate = _silu(hg_ref[b, rows].astype(F32))
        heads = [slice(h * HG_EXPAND, (h + 1) * HG_EXPAND) for h in range(HG_HEADS)]
        diag = [_dot_nt(qb[:, sl], kb[:, sl]) for sl in heads]
        pairs = [[_dot_nt(zs[lev][:, sl], zs[lev][:, sl]) for lev in range(HG_LEVELS)] for sl in heads]
        sts = [st_sc[b * HG_HEADS + h] for h in range(HG_HEADS)]
        inter = [_dot_nt(q_in[:, sl], st.astype(BF16)) for sl, st in zip(heads, sts)]
        fresh = [_dot_tn(v[:, sl], k_out[:, sl]) for sl in heads]
        for h, sl in enumerate(heads):
            scores = masks_ref[HG_LEVELS] * diag[h]
            for lev in range(HG_LEVELS):
                scores = scores + masks_ref[lev] * pairs[h][lev]
            o = _dot(scores.astype(BF16), v[:, sl]) + inter[h]
            st_sc[b * HG_HEADS + h] = sts[h] * jnp.exp2(glast[:, sl]) + fresh[h]
            o_ref[b, rows, sl] = (_rms(o, gn_ref[...]) * gate[:, sl]).astype(BF16)


N_SSD_IN = 11
N_HGRN_IN = 8


def _rec_scan_kernel(*refs):
    ssd_in, hg_in = refs[:N_SSD_IN], refs[N_SSD_IN:N_SSD_IN + N_HGRN_IN]
    y_ref, o_ref, xpad_sc, sst_sc, hst_sc = refs[N_SSD_IN + N_HGRN_IN:]

    @pl.when(pl.program_id(0) == 0)
    def _():
        xpad_sc[:, 0:CONV_PAD] = jnp.zeros((xpad_sc.shape[0], CONV_PAD, SSM_CONV_DIM), F32)
        sst_sc[...] = jnp.zeros(sst_sc.shape, F32)
        hst_sc[...] = jnp.zeros(hst_sc.shape, F32)

    _ssd_chunk(*ssd_in, y_ref, xpad_sc, sst_sc)
    for c0 in range(0, SSM_CHUNK, HG_CHUNK):
        _hgrn_chunk(*hg_in, o_ref, hst_sc, slice(c0, c0 + HG_CHUNK))


def _rec_scan(xbc, dt, z, conv_w, conv_b, dt_bias, a_log, d_skip, ssm_norm, hq, hf, hi, hg, lb, g_norm):
    bsz, s_len, _ = xbc.shape
    L = SSM_CHUNK
    pad = lambda v: jnp.pad(v.astype(F32), (0, LANES - SSM_HEADS)).reshape(1, LANES)
    a_head = -jnp.exp(a_log.astype(F32)) * LOG2E
    tri = jnp.asarray(np.tril(np.ones((L, L), np.float32)), BF16)
    expand = np.zeros((LANES, SSM_INNER), np.float32)
    for h in range(SSM_HEADS):
        expand[h, h * SSM_HEADDIM:(h + 1) * SSM_HEADDIM] = 1.0
    expand = jnp.asarray(expand, BF16)
    dsk = jnp.repeat(d_skip.astype(F32), SSM_HEADDIM).reshape(1, SSM_INNER)
    w3, masks = _hgrn_tables()
    w3 = jnp.asarray(w3, BF16)
    masks = jnp.asarray(masks, F32)
    chunk = lambda w: pl.BlockSpec((bsz, L, w), lambda c: (0, c, 0))
    const = lambda shape: pl.BlockSpec(shape, lambda c: (0,) * len(shape))
    ssd_specs = [chunk(SSM_CONV_DIM), chunk(LANES), chunk(SSM_INNER), const((SSM_CONV, SSM_CONV_DIM)),
                 const((1, SSM_CONV_DIM)), const((1, LANES)), const((1, LANES)), const((1, SSM_INNER)),
                 const((1, SSM_INNER)), const((L, L)), const((LANES, SSM_INNER))]
    hg_specs = [chunk(HG_WIDTH), chunk(HG_WIDTH), chunk(HG_WIDTH), chunk(HG_WIDTH), const((1, HG_KDIM_TOTAL)),
                const((1, HG_VDIM)), const(w3.shape), const(masks.shape)]
    assert len(ssd_specs) == N_SSD_IN and len(hg_specs) == N_HGRN_IN
    return pl.pallas_call(
        _rec_scan_kernel,
        grid=(s_len // L,),
        in_specs=ssd_specs + hg_specs,
        out_specs=[chunk(SSM_INNER), chunk(HG_WIDTH)],
        out_shape=[jax.ShapeDtypeStruct((bsz, s_len, SSM_INNER), BF16),
                   jax.ShapeDtypeStruct((bsz, s_len, HG_WIDTH), BF16)],
        scratch_shapes=[pltpu.VMEM((bsz, CONV_PAD + L, SSM_CONV_DIM), F32),
                        pltpu.VMEM((bsz * SSM_GROUPS, SSM_STATE, SSM_INNER // SSM_GROUPS), F32),
                        pltpu.VMEM((bsz * HG_HEADS, HG_VDIM, HG_EXPAND), F32)],
        compiler_params=pltpu.CompilerParams(dimension_semantics=("arbitrary",), vmem_limit_bytes=VMEM_LIMIT),
        name="rec_scan",
    )(xbc, dt, z, conv_w, conv_b.reshape(1, -1), pad(dt_bias), pad(a_head), dsk, ssm_norm.reshape(1, -1),
      tri, expand, hq, hf, hi, hg, lb.reshape(1, -1), g_norm.reshape(1, -1), w3, masks)


def kernel(x, norm_mix, norm_ffn, norm_final, a_w_in, a_q_norm, a_w_uq, a_kv_norm, a_w_ukv, a_lq1, a_lk1, a_lq2, a_lk2, a_subln, a_w_out, s_w_in, s_conv_w, s_conv_b, s_dt_bias, s_a_log, s_d, s_norm, h_g_norm, h_lower_bound, s_w_out, ffn_gate, ffn_up, ffn_down):
    bsz, s_len, _ = x.shape
    n_tok = bsz * s_len
    depth = norm_mix.shape[0]
    assert depth == 2 and s_len % ATTN_TQ == 0
    p_lb = jax.nn.softmax(h_lower_bound.astype(F32), axis=0)
    lb_all = jnp.cumsum(p_lb, axis=0) - p_lb[0:1]
    w_gate, w_up, w_down = ffn_gate.astype(BF16), ffn_up.astype(BF16), ffn_down.astype(BF16)

    lambda_init = 0.8 - 0.6 * math.exp(-0.3 * 0)
    q, k, vt, dq, dk, dvt = _attn_inproj(x, norm_mix[0], a_w_in[0], a_q_norm[0], a_w_uq[0], a_kv_norm[0],
                                       a_w_ukv[0], ts=2 * INPROJ_SUB)
    o_mla, o_diff = _attention(q, k, vt, dq, dk, dvt, a_lq1[0], a_lk1[0], a_lq2[0], a_lk2[0], a_subln[0],
                               lambda_init, tq=ATTN_TQ)
    x2d = _outproj_ffn(x.reshape(n_tok, D_MODEL), o_mla.reshape(n_tok, -1), o_diff.reshape(n_tok, -1),
                       a_w_out[0], norm_ffn[0], 0, w_gate, w_up, w_down, norm_final, final_norm=False,
                       tm=2 * FFN_SUB)

    z, xbc, hq, hf, hi, hg, dt = _rec_inproj(x2d, norm_mix[1], s_w_in[0], tm=2 * INPROJ_SUB)
    seq = lambda t: t.reshape(bsz, s_len, t.shape[-1])
    y, o = _rec_scan(seq(xbc), seq(dt), seq(z), s_conv_w[0], s_conv_b[0], s_dt_bias[0], s_a_log[0], s_d[0],
                     s_norm[0], seq(hq), seq(hf), seq(hi), seq(hg), lb_all[1], h_g_norm[0])
    x2d = _outproj_ffn(x2d, y.reshape(n_tok, -1), o.reshape(n_tok, -1), s_w_out[0], norm_ffn[1], 1, w_gate, w_up,
                       w_down, norm_final, final_norm=True, tm=2 * FFN_SUB)
    return x2d.reshape(bsz, s_len, D_MODEL)
```
